```python
import jax, jax.numpy as jnp
from jax import lax
import numpy as np

D_MODEL = 1024
BATCH = 8
SEQ = 2048
DEPTH = 2
DEC_BATCH = 32
DEC_SEQ = 1
PAST_LEN = 16384
PAGE_SIZE = 128

D_MIX = D_MODEL
C_CONV = 3 * D_MIX // 8
C_POOL = D_MIX // 4
C_ATT = D_MIX - C_CONV - C_POOL
HEAD_DIM = 64
N_HEADS = C_ATT // HEAD_DIM
CONV_W = 31
POOL_WINDOWS = (2, 4, 8, 16)
N_POOL_GROUPS = 4
POOL_GC = C_POOL // N_POOL_GROUPS
POOL_BUF = 15
DILATIONS = (1, 4, 16)
KEYS_PER_PATTERN = 128
N_PAT = 3
SEG = KEYS_PER_PATTERN + 1
MAX_WINDOW = 2048
ROPE_THETA = 10000.0
EPS = 1e-6
Q_BLOCK = 64
D_IN = 3 * C_CONV + 2 * C_POOL + 4 * C_ATT

kernel_name = "hymba_conv_pool_dilated_decoder_step"


def rms_norm(x, g):
    xf = x.astype(jnp.float32)
    y = xf * lax.rsqrt(jnp.mean(xf * xf, axis=-1, keepdims=True) + EPS)
    return (y * g.astype(jnp.float32)).astype(x.dtype)


def layer_norm(x, g, b):
    xf = x.astype(jnp.float32)
    mu = jnp.mean(xf, axis=-1, keepdims=True)
    xc = xf - mu
    y = xc * lax.rsqrt(jnp.mean(xc * xc, axis=-1, keepdims=True) + EPS)
    return (y * g.astype(jnp.float32) + b.astype(jnp.float32)).astype(x.dtype)


def rope(x, pos):
    half = HEAD_DIM // 2
    inv = ROPE_THETA ** (-jnp.arange(half, dtype=jnp.float32) / half)
    ang = pos.astype(jnp.float32)[:, None] * inv[None, :]
    cos = jnp.cos(ang)[None, :, None, :]
    sin = jnp.sin(ang)[None, :, None, :]
    xf = x.astype(jnp.float32)
    x1, x2 = xf[..., :half], xf[..., half:]
    return jnp.concatenate([x1 * cos - x2 * sin, x2 * cos + x1 * sin], axis=-1).astype(x.dtype)


def dilated_offsets():
    j = jnp.arange(SEG, dtype=jnp.int32)
    return jnp.concatenate([d * j for d in DILATIONS])


def dilated_attend(q, kk, vv, qidx):
    B, Tq = q.shape[0], q.shape[1]
    off = dilated_offsets()
    kidx = qidx[:, None] - off[None, :]
    valid = kidx >= 0
    kidx = jnp.maximum(kidx, 0)
    kg = jnp.take(kk, kidx, axis=1)
    vg = jnp.take(vv, kidx, axis=1)
    s = jnp.einsum('bqhd,bqkhd->bhqk', q, kg, preferred_element_type=jnp.float32) * (HEAD_DIM ** -0.5)
    s = jnp.where(valid[None, None], s, -1e30).reshape(B, N_HEADS, Tq, N_PAT, SEG)
    m = jnp.max(s, axis=-1, keepdims=True)
    p = jnp.exp(s - m)
    l = jnp.sum(p, axis=-1)
    o = jnp.einsum('bhqgk,bqgkhd->bqghd', p,
                   vg.reshape(B, Tq, N_PAT, SEG, N_HEADS, HEAD_DIM).astype(jnp.float32))
    lse = m[..., 0] + jnp.log(l)
    alpha = jax.nn.softmax(lse, axis=-1)
    out = jnp.einsum('bqghd,bhqg->bqhd', o, alpha / l)
    return out.astype(q.dtype)


def dilated_attention(q, kk, vv, n_prev):
    B, T = q.shape[0], q.shape[1]
    if T > Q_BLOCK and T % Q_BLOCK == 0:
        def blk(b):
            start = b * Q_BLOCK
            qb = lax.dynamic_slice_in_dim(q, start, Q_BLOCK, axis=1)
            qidx = n_prev + start + jnp.arange(Q_BLOCK, dtype=jnp.int32)
            return dilated_attend(qb, kk, vv, qidx)
        out = lax.map(blk, jnp.arange(T // Q_BLOCK, dtype=jnp.int32))
        return out.transpose(1, 0, 2, 3, 4).reshape(B, T, N_HEADS, HEAD_DIM)
    return dilated_attend(q, kk, vv, n_prev + jnp.arange(T, dtype=jnp.int32))


def causal_dwconv(u_ext, w, b):
    y = lax.conv_general_dilated(u_ext, w[:, None, :], window_strides=(1,), padding='VALID',
                                 dimension_numbers=('NWC', 'WIO', 'NWC'),
                                 feature_group_count=C_CONV)
    return y + b


def multiscale_pool(xb_ext, start_pos, w_pool, scale):
    B, L, _ = xb_ext.shape
    T = L - POOL_BUF
    xf = xb_ext.astype(jnp.float32).reshape(B, L, N_POOL_GROUPS, POOL_GC)
    cs = jnp.concatenate([jnp.zeros((B, 1, N_POOL_GROUPS, POOL_GC), jnp.float32),
                          jnp.cumsum(xf, axis=1)], axis=1)
    hi = cs[:, POOL_BUF + 1:]
    pos = start_pos + jnp.arange(T, dtype=jnp.int32)
    means = []
    for g, w in enumerate(POOL_WINDOWS):
        lo = cs[:, POOL_BUF + 1 - w: POOL_BUF + 1 - w + T, g]
        cnt = jnp.minimum(pos + 1, w).astype(jnp.float32)[None, :, None]
        means.append((hi[:, :, g] - lo) / cnt)
    pooled = jnp.stack(means, axis=2)
    d = pooled - xf[:, POOL_BUF:]
    y = jnp.einsum('btgc,gcd->btgd', d, w_pool.astype(jnp.float32)).reshape(B, T, C_POOL)
    return (y * scale.astype(jnp.float32)).astype(xb_ext.dtype)


def trunk_layer(x, start_pos, conv_buf, pool_buf, k_buf, v_buf,
                norm_g, w_in, conv_w, conv_b, ln_g, ln_b, pool_w, pool_scale, qn_g, kn_g, w_out):
    B, T, _ = x.shape
    h = rms_norm(x, norm_g)
    proj = jnp.einsum('btd,de->bte', h, w_in)
    widths = [C_CONV, C_CONV, C_CONV, C_POOL, C_POOL, C_ATT, C_ATT, C_ATT, C_ATT]
    cuts = [int(c) for c in np.cumsum(widths)[:-1]]
    a_val, a_glu, a_gate, b_val, b_gate, q, k, v, c_gate = jnp.split(proj, cuts, axis=-1)

    u = a_val * jax.nn.sigmoid(a_glu)
    u_ext = jnp.concatenate([conv_buf, u], axis=1)
    c = layer_norm(causal_dwconv(u_ext, conv_w, conv_b), ln_g, ln_b)
    ya = jax.nn.silu(c) * jax.nn.silu(a_gate)
    new_conv = u_ext[:, -(CONV_W - 1):]

    pb_ext = jnp.concatenate([pool_buf, b_val], axis=1)
    yb = multiscale_pool(pb_ext, start_pos, pool_w, pool_scale) * jax.nn.silu(b_gate)
    new_pool = pb_ext[:, -POOL_BUF:]

    pos = start_pos + jnp.arange(T, dtype=jnp.int32)
    q = rope(rms_norm(q.reshape(B, T, N_HEADS, HEAD_DIM), qn_g), pos)
    k = rope(rms_norm(k.reshape(B, T, N_HEADS, HEAD_DIM), kn_g), pos)
    v = v.reshape(B, T, N_HEADS, HEAD_DIM)
    kk = jnp.concatenate([k_buf, k], axis=1)
    vv = jnp.concatenate([v_buf, v], axis=1)
    o = dilated_attention(q, kk, vv, k_buf.shape[1])
    yc = o.reshape(B, T, C_ATT) * jax.nn.silu(c_gate)
    n_keep = min(MAX_WINDOW, kk.shape[1])
    new_k = kk[:, -n_keep:]
    new_v = vv[:, -n_keep:]

    y = jnp.einsum('bte,ed->btd', jnp.concatenate([ya, yb, yc], axis=-1), w_out)
    return x + y, new_conv, new_pool, new_k, new_v


def setup_inputs(seed: int = 0) -> dict:
    key = jax.random.key(seed)
    ks = jax.random.split(key, 20)
    f32 = jnp.float32
    win_buf = min(MAX_WINDOW, PAST_LEN)
    n = lambda i, shape, s: s * jax.random.normal(ks[i], shape, f32)
    return {
        "x_prompt": n(0, (BATCH, SEQ, D_MODEL), 1.0),
        "x_sample": n(1, (DEC_BATCH, DEC_SEQ, D_MODEL), 1.0),
        "state_conv": n(2, (DEPTH, DEC_BATCH, CONV_W - 1, C_CONV), 0.5),
        "state_pool": n(3, (DEPTH, DEC_BATCH, POOL_BUF, C_POOL), 0.6),
        "cache_k_win": n(4, (DEPTH, DEC_BATCH, win_buf, N_HEADS, HEAD_DIM), 1.0),
        "cache_v_win": n(5, (DEPTH, DEC_BATCH, win_buf, N_HEADS, HEAD_DIM), 0.6),
        "norm_g": 1.0 + n(6, (DEPTH, D_MODEL), 0.05),
        "w_in": n(7, (DEPTH, D_MODEL, D_IN), D_MODEL ** -0.5),
        "conv_w": n(8, (DEPTH, CONV_W, C_CONV), CONV_W ** -0.5),
        "conv_b": n(9, (DEPTH, C_CONV), 0.02),
        "ln_g": 1.0 + n(10, (DEPTH, C_CONV), 0.05),
        "ln_b": n(11, (DEPTH, C_CONV), 0.02),
        "pool_w": n(12, (DEPTH, N_POOL_GROUPS, POOL_GC, POOL_GC), POOL_GC ** -0.5),
        "pool_scale": 1.0 + n(13, (DEPTH, C_POOL), 0.1),
        "q_norm_g": 1.0 + n(14, (DEPTH, HEAD_DIM), 0.05),
        "k_norm_g": 1.0 + n(15, (DEPTH, HEAD_DIM), 0.05),
        "w_out": n(16, (DEPTH, D_MIX, D_MODEL), 0.5 * D_MIX ** -0.5),
    }


def reference(x_prompt, x_sample, state_conv, state_pool, cache_k_win, cache_v_win,
              norm_g, w_in, conv_w, conv_b, ln_g, ln_b, pool_w, pool_scale,
              q_norm_g, k_norm_g, w_out):
    B = x_prompt.shape[0]
    dt = x_prompt.dtype
    xp, xs = x_prompt, x_sample
    pc, pp, pk, pv = [], [], [], []
    sc, sp, sk, sv = [], [], [], []
    for l in range(DEPTH):
        params = (norm_g[l], w_in[l], conv_w[l], conv_b[l], ln_g[l], ln_b[l],
                  pool_w[l], pool_scale[l], q_norm_g[l], k_norm_g[l], w_out[l])
        xp, c1, p1, k1, v1 = trunk_layer(
            xp, 0,
            jnp.zeros((B, CONV_W - 1, C_CONV), dt), jnp.zeros((B, POOL_BUF, C_POOL), dt),
            jnp.zeros((B, 0, N_HEADS, HEAD_DIM), dt), jnp.zeros((B, 0, N_HEADS, HEAD_DIM), dt),
            *params)
        xs, c2, p2, k2, v2 = trunk_layer(
            xs, PAST_LEN, state_conv[l], state_pool[l], cache_k_win[l], cache_v_win[l], *params)
        pc.append(c1); pp.append(p1); pk.append(k1); pv.append(v1)
        sc.append(c2); sp.append(p2); sk.append(k2); sv.append(v2)
    return (xp, xs, jnp.stack(pc), jnp.stack(pp), jnp.stack(pk), jnp.stack(pv),
            jnp.stack(sc), jnp.stack(sp), jnp.stack(sk), jnp.stack(sv))
```

```python
import functools

import jax
import jax.numpy as jnp
import numpy as np
from jax import lax
from jax.experimental import pallas as pl
from jax.experimental.pallas import tpu as pltpu

F32 = jnp.float32
BF16 = jnp.bfloat16

D_MODEL = 1024
C_CONV = 384
C_POOL = 256
C_ATT = 384
HEAD_DIM = 64
N_HEADS = C_ATT // HEAD_DIM
CONV_W = 31
CONV_HALO = CONV_W - 1
POOL_WINDOWS = (2, 4, 8, 16)
POOL_GC = 64
POOL_BUF = 15
DILATIONS = (1, 4, 16)
WINDOW_KEYS = 128
EPS = 1e-6
ROPE_THETA = 10000.0
D_IN = 3 * C_CONV + 2 * C_POOL + 4 * C_ATT
PAST_LEN = 16384
NEG = -1e30

OFF_A_VAL = 0
OFF_A_GLU = OFF_A_VAL + C_CONV
OFF_A_GATE = OFF_A_GLU + C_CONV
OFF_B_VAL = OFF_A_GATE + C_CONV
OFF_B_GATE = OFF_B_VAL + C_POOL
OFF_Q = OFF_B_GATE + C_POOL
OFF_K = OFF_Q + C_ATT
OFF_V = OFF_K + C_ATT
OFF_C_GATE = OFF_V + C_ATT

LANES = 128
N_CONV_SLAB = C_CONV // LANES
N_POOL_SLAB = C_POOL // LANES
N_ATT_SLAB = C_ATT // LANES
VMEM_LIMIT_BYTES = 56 * 1024 * 1024

TQ = 256
RC = 32
U_HALO = 32
B_HALO = 16
QB = WINDOW_KEYS


def _silu(x):
    return x * jax.nn.sigmoid(x)


def _rope_tables(positions):
    half = HEAD_DIM // 2
    inv = ROPE_THETA ** (-jnp.arange(half, dtype=F32) / half)
    ang = positions.astype(F32)[:, None] * inv[None, :]
    cos = jnp.cos(ang)
    sin = jnp.sin(ang)
    cos_h = jnp.concatenate([cos, cos], axis=-1)
    sin_h = jnp.concatenate([-sin, sin], axis=-1)
    reps = LANES // HEAD_DIM
    return jnp.tile(cos_h, (1, reps)), jnp.tile(sin_h, (1, reps))


def _head_mean_matrix():
    idx = np.arange(C_ATT) // HEAD_DIM
    return jnp.asarray((idx[:, None] == idx[None, :]).astype(np.float32) / HEAD_DIM, dtype=BF16)


def _pool_block_diag(pool_w):
    out = jnp.zeros((pool_w.shape[0], C_POOL, C_POOL), pool_w.dtype)
    for g in range(len(POOL_WINDOWS)):
        out = out.at[:, g * POOL_GC:(g + 1) * POOL_GC, g * POOL_GC:(g + 1) * POOL_GC].set(pool_w[:, g])
    return out


def _swap_halves(x):
    lane = lax.broadcasted_iota(jnp.int32, x.shape, 1)
    first_half = (lane % HEAD_DIM) < (HEAD_DIM // 2)
    return jnp.where(first_half, pltpu.roll(x, LANES - HEAD_DIM // 2, 1), pltpu.roll(x, HEAD_DIM // 2, 1))


def _pool_means(loads, pos):
    lane = lax.broadcasted_iota(jnp.int32, loads(0, 0).shape, 1)
    lo = lane < POOL_GC
    posf = (pos + 1).astype(F32)
    outs = []
    for slab in range(N_POOL_SLAB):
        w_lo, w_hi = POOL_WINDOWS[2 * slab], POOL_WINDOWS[2 * slab + 1]
        cur = loads(0, slab)
        s = cur
        for i in range(1, w_lo):
            s = s + loads(i, slab)
        s_lo = s
        for i in range(w_lo, w_hi):
            s = s + loads(i, slab)
        s_hi = s
        cnt_lo = jnp.minimum(posf, float(w_lo))
        cnt_hi = jnp.minimum(posf, float(w_hi))
        pooled = jnp.where(lo, s_lo / cnt_lo, s_hi / cnt_hi)
        outs.append(pooled - cur)
    return outs


def _prompt_kernel(x_ref, ng_ref, win_ref, cw_ref, cb_ref, lng_ref, lnb_ref, pw_ref, psc_ref,
                   qg_ref, kg_ref, cos_ref, sin_ref, mavg_ref, wout_ref,
                   y_ref, ko_ref, vo_ref, cst_ref, pst_ref,
                   h_s, proj, u_buf, b_buf, sq_s, msq_s, d_s, yb_s,
                   q_s, k_s, v_s, o_s, m_s, l_s, mix, cg, *, seq_len):
    nt = seq_len // TQ
    t = pl.program_id(1)

    @pl.when(t < nt)
    def _phase1():
        row0 = pl.multiple_of(t * TQ, TQ)

        @pl.when(t == 0)
        def _zero_halo():
            u_buf[:, 0:U_HALO, :] = jnp.zeros((N_CONV_SLAB, U_HALO, LANES), F32)
            b_buf[:, 0:B_HALO, :] = jnp.zeros((N_POOL_SLAB, B_HALO, LANES), F32)

        def norm_chunk(i, c):
            r = pl.multiple_of(i * RC, RC)
            x = x_ref[pl.ds(r, RC), :]
            ms = jnp.mean(x * x, axis=-1, keepdims=True)
            h_s[pl.ds(r, RC), :] = (x * lax.rsqrt(ms + EPS) * ng_ref[...]).astype(BF16)
            return c

        lax.fori_loop(0, TQ // RC, norm_chunk, 0)
        proj[...] = jnp.dot(h_s[...], win_ref[...], preferred_element_type=F32)

        def split_chunk(i, c):
            r = pl.multiple_of(i * RC, RC)
            rows = pl.ds(r, RC)
            grow = pl.ds(pl.multiple_of(row0 + r, RC), RC)
            for s in range(N_CONV_SLAB):
                cols = slice(s * LANES, (s + 1) * LANES)
                a_val = proj[rows, OFF_A_VAL + s * LANES:OFF_A_VAL + (s + 1) * LANES]
                a_glu = proj[rows, OFF_A_GLU + s * LANES:OFF_A_GLU + (s + 1) * LANES]
                u_buf[s, pl.ds(U_HALO + r, RC), :] = a_val * jax.nn.sigmoid(a_glu)
                q = proj[rows, OFF_Q + s * LANES:OFF_Q + (s + 1) * LANES]
                k = proj[rows, OFF_K + s * LANES:OFF_K + (s + 1) * LANES]
                sq_s[rows, cols] = (q * q).astype(BF16)
                sq_s[rows, C_ATT + s * LANES:C_ATT + (s + 1) * LANES] = (k * k).astype(BF16)
                v_s[s, grow, :] = proj[rows, OFF_V + s * LANES:OFF_V + (s + 1) * LANES]
                cg[grow, cols] = _silu(proj[rows, OFF_C_GATE + s * LANES:OFF_C_GATE + (s + 1) * LANES])
            for s in range(N_POOL_SLAB):
                b_buf[s, pl.ds(B_HALO + r, RC), :] = proj[rows, OFF_B_VAL + s * LANES:OFF_B_VAL + (s + 1) * LANES]
            return c

        lax.fori_loop(0, TQ // RC, split_chunk, 0)
        msq_s[:, 0:C_ATT] = jnp.dot(sq_s[:, 0:C_ATT], mavg_ref[...], preferred_element_type=F32)
        msq_s[:, C_ATT:2 * C_ATT] = jnp.dot(sq_s[:, C_ATT:2 * C_ATT], mavg_ref[...], preferred_element_type=F32)

        def mixer_chunk(i, c):
            r = pl.multiple_of(i * RC, RC)
            rows = pl.ds(r, RC)
            grow = pl.ds(pl.multiple_of(row0 + r, RC), RC)
            conv = []
            for s in range(N_CONV_SLAB):
                cols = slice(s * LANES, (s + 1) * LANES)
                acc = jnp.zeros((RC, LANES), F32) + cb_ref[:, cols]
                for w in range(CONV_W):
                    acc = acc + u_buf[s, pl.ds(r + (U_HALO - CONV_HALO) + w, RC), :] * cw_ref[w:w + 1, cols]
                conv.append(acc)
            mu = jnp.sum(conv[0] + conv[1] + conv[2], axis=-1, keepdims=True) * (1.0 / C_CONV)
            cen = [cv - mu for cv in conv]
            var = jnp.sum(cen[0] * cen[0] + cen[1] * cen[1] + cen[2] * cen[2], axis=-1, keepdims=True) * (1.0 / C_CONV)
            rstd = lax.rsqrt(var + EPS)
            for s in range(N_CONV_SLAB):
                cols = slice(s * LANES, (s + 1) * LANES)
                ln = cen[s] * rstd * lng_ref[:, cols] + lnb_ref[:, cols]
                gate = proj[rows, OFF_A_GATE + s * LANES:OFF_A_GATE + (s + 1) * LANES]
                mix[grow, cols] = (_silu(ln) * _silu(gate)).astype(BF16)
            pos = row0 + r + lax.broadcasted_iota(jnp.int32, (RC, 1), 0)
            dl = _pool_means(lambda sh, s: b_buf[s, pl.ds(r + B_HALO - sh, RC), :], pos)
            for s in range(N_POOL_SLAB):
                d_s[rows, s * LANES:(s + 1) * LANES] = dl[s].astype(BF16)
            cos = cos_ref[grow, :]
            sin = sin_ref[grow, :]
            for s in range(N_ATT_SLAB):
                cols = slice(s * LANES, (s + 1) * LANES)
                q = proj[rows, OFF_Q + s * LANES:OFF_Q + (s + 1) * LANES]
                qn = q * lax.rsqrt(msq_s[rows, cols] + EPS) * qg_ref[:, cols]
                qr = qn * cos + _swap_halves(qn) * sin
                q_s[s, grow, :] = qr * (HEAD_DIM ** -0.5)
                k = proj[rows, OFF_K + s * LANES:OFF_K + (s + 1) * LANES]
                kn = k * lax.rsqrt(msq_s[rows, C_ATT + s * LANES:C_ATT + (s + 1) * LANES] + EPS) * kg_ref[:, cols]
                k_s[s, grow, :] = kn * cos + _swap_halves(kn) * sin
            return c

        lax.fori_loop(0, TQ // RC, mixer_chunk, 0)
        yb_s[...] = jnp.dot(d_s[...], pw_ref[...], preferred_element_type=F32)

        def pool_out_chunk(i, c):
            r = pl.multiple_of(i * RC, RC)
            rows = pl.ds(r, RC)
            grow = pl.ds(pl.multiple_of(row0 + r, RC), RC)
            for s in range(N_POOL_SLAB):
                cols = slice(s * LANES, (s + 1) * LANES)
                gate = proj[rows, OFF_B_GATE + s * LANES:OFF_B_GATE + (s + 1) * LANES]
                yb = yb_s[rows, cols] * psc_ref[:, cols] * _silu(gate)
                mix[grow, C_CONV + s * LANES:C_CONV + (s + 1) * LANES] = yb.astype(BF16)
            return c

        lax.fori_loop(0, TQ // RC, pool_out_chunk, 0)

        for s in range(N_ATT_SLAB):
            ko_ref[s * LANES:(s + 1) * LANES, :] = k_s[s, pl.ds(row0, TQ), :].T
            vo_ref[s * LANES:(s + 1) * LANES, :] = v_s[s, pl.ds(row0, TQ), :].T

        @pl.when(t == nt - 1)
        def _write_state():
            for s in range(N_CONV_SLAB):
                cst_ref[:, s * LANES:(s + 1) * LANES] = u_buf[s, TQ + U_HALO - CONV_HALO:TQ + U_HALO, :]
            for s in range(N_POOL_SLAB):
                pst_ref[:, s * LANES:(s + 1) * LANES] = b_buf[s, TQ + B_HALO - POOL_BUF:TQ + B_HALO, :]

        u_buf[:, 0:U_HALO, :] = u_buf[:, TQ:TQ + U_HALO, :]
        b_buf[:, 0:B_HALO, :] = b_buf[:, TQ:TQ + B_HALO, :]

    @pl.when(t == nt - 1)
    def _attention():
        n_units = seq_len // QB
        lane = lax.broadcasted_iota(jnp.int32, (QB, LANES), 1)
        rowi = lax.broadcasted_iota(jnp.int32, (QB, LANES), 0)
        lo = lane < HEAD_DIM
        cur_ok = lane <= rowi
        prev_ok = lane >= rowi
        cur_ok2 = jnp.concatenate([cur_ok, cur_ok], axis=0)
        prev_ok2 = jnp.concatenate([prev_ok, prev_ok], axis=0)

        for d in DILATIONS:
            has_prev = (seq_len // d) > QB

            def unit(u, c, d=d, has_prev=has_prev):
                phase = u % d
                blk = u // d
                start = d * QB * blk + phase
                pstart = jnp.maximum(start - QB * d, phase)
                prev_valid = blk > 0

                def rows_of(s0):
                    return pl.ds(s0, QB) if d == 1 else pl.ds(s0, QB, stride=d)

                for s in range(N_ATT_SLAB):
                    q = q_s[s, rows_of(start), :]
                    qa = jnp.where(lo, q, 0.0).astype(BF16)
                    qb = jnp.where(lo, 0.0, q).astype(BF16)
                    q2 = jnp.concatenate([qa, qb], axis=0)
                    kc = k_s[s, rows_of(start), :].astype(BF16)
                    vc = v_s[s, rows_of(start), :].astype(BF16)
                    sc = lax.dot_general(q2, kc, (((1,), (1,)), ((), ())), preferred_element_type=F32)
                    sc = jnp.where(cur_ok2, sc, NEG)
                    m = jnp.max(sc, axis=-1, keepdims=True)
                    if has_prev:
                        kp = k_s[s, rows_of(pstart), :].astype(BF16)
                        vp = v_s[s, rows_of(pstart), :].astype(BF16)
                        sp = lax.dot_general(q2, kp, (((1,), (1,)), ((), ())), preferred_element_type=F32)
                        sp = jnp.where(jnp.logical_and(prev_ok2, prev_valid), sp, NEG)
                        m = jnp.maximum(m, jnp.max(sp, axis=-1, keepdims=True))
                    pc = jnp.exp(sc - m)
                    l = jnp.sum(pc, axis=-1, keepdims=True)
                    o2 = jnp.dot(pc.astype(BF16), vc, preferred_element_type=F32)
                    if has_prev:
                        pp = jnp.exp(sp - m)
                        l = l + jnp.sum(pp, axis=-1, keepdims=True)
                        o2 = o2 + jnp.dot(pp.astype(BF16), vp, preferred_element_type=F32)
                    o_u = jnp.where(lo, o2[0:QB], o2[QB:2 * QB])
                    m_u = jnp.where(lo, m[0:QB], m[QB:2 * QB])
                    l_u = jnp.where(lo, l[0:QB], l[QB:2 * QB])
                    rows = rows_of(start)
                    if d == DILATIONS[0]:
                        o_s[s, rows, :] = o_u
                        m_s[s, rows, :] = m_u
                        l_s[s, rows, :] = l_u
                    else:
                        m_old = m_s[s, rows, :]
                        m_new = jnp.maximum(m_old, m_u)
                        a_old = jnp.exp(m_old - m_new)
                        a_u = jnp.exp(m_u - m_new)
                        o_s[s, rows, :] = o_s[s, rows, :] * a_old + o_u * a_u
                        l_s[s, rows, :] = l_s[s, rows, :] * a_old + l_u * a_u
                        m_s[s, rows, :] = m_new
                return c

            lax.fori_loop(0, n_units, unit, 0)

    @pl.when(t >= nt)
    def _phase2():
        row0 = pl.multiple_of((t - nt) * TQ, TQ)

        def att_chunk(i, c):
            r = pl.multiple_of(i * RC, RC)
            grow = pl.ds(pl.multiple_of(row0 + r, RC), RC)
            for s in range(N_ATT_SLAB):
                cols = slice(s * LANES, (s + 1) * LANES)
                yc = o_s[s, grow, :] / l_s[s, grow, :] * cg[grow, cols]
                mix[grow, C_CONV + C_POOL + s * LANES:C_CONV + C_POOL + (s + 1) * LANES] = yc.astype(BF16)
            return c

        lax.fori_loop(0, TQ // RC, att_chunk, 0)
        y_ref[...] = x_ref[...] + jnp.dot(mix[pl.ds(row0, TQ), :], wout_ref[...], preferred_element_type=F32)


def _prompt_kernel_with_alias(*refs, n_in, seq_len):
    _prompt_kernel(*refs[:n_in], *refs[n_in + 2:], seq_len=seq_len)


def _prompt_layer(layer, x, norm_g, w_in, conv_w, conv_b, ln_g, ln_b, pool_wbd, pool_scale, qg, kg, cos_t, sin_t,
                  mavg, w_out, kv_prev=None):
    bsz, seq_len, _ = x.shape
    depth = w_in.shape[0]
    assert seq_len % TQ == 0 and seq_len % (QB * DILATIONS[-1]) == 0
    nt = seq_len // TQ

    def const(shape):
        nd = len(shape)
        return pl.BlockSpec(shape, lambda b, t: (0,) * nd)

    def resident(shape):
        nd = len(shape)
        return pl.BlockSpec(shape, lambda b, t: (0,) * nd, pipeline_mode=pl.Buffered(1))

    def per_layer(shape, **kw):
        nd = len(shape)
        return pl.BlockSpec((None,) + shape, lambda b, t: (layer,) + (0,) * nd, **kw)

    in_specs = [
        pl.BlockSpec((None, TQ, D_MODEL), lambda b, t: (b, t % nt, 0)),
        per_layer((1, D_MODEL)),
        per_layer((D_MODEL, D_IN), pipeline_mode=pl.Buffered(1)),
        per_layer((CONV_W, C_CONV)),
        per_layer((1, C_CONV)),
        per_layer((1, C_CONV)),
        per_layer((1, C_CONV)),
        per_layer((C_POOL, C_POOL)),
        per_layer((1, C_POOL)),
        per_layer((1, C_ATT)),
        per_layer((1, C_ATT)),
        resident((seq_len, LANES)),
        resident((seq_len, LANES)),
        const((C_ATT, C_ATT)),
        per_layer((D_MODEL, D_MODEL), pipeline_mode=pl.Buffered(1)),
    ]
    operands = [x, norm_g, w_in, conv_w, conv_b, ln_g, ln_b, pool_wbd, pool_scale, qg, kg, cos_t, sin_t, mavg, w_out]
    n_in = len(operands)
    kv_spec = pl.BlockSpec((None, None, C_ATT, TQ), lambda b, t: (layer, b, 0, jnp.minimum(t, nt - 1)))
    out_specs = [
        pl.BlockSpec((None, TQ, D_MODEL), lambda b, t: (b, jnp.maximum(t - nt, 0), 0)),
        kv_spec,
        kv_spec,
        pl.BlockSpec((None, CONV_HALO, C_CONV), lambda b, t: (b, 0, 0)),
        pl.BlockSpec((None, POOL_BUF, C_POOL), lambda b, t: (b, 0, 0)),
    ]
    out_shape = [
        jax.ShapeDtypeStruct((bsz, seq_len, D_MODEL), F32),
        jax.ShapeDtypeStruct((depth, bsz, C_ATT, seq_len), F32),
        jax.ShapeDtypeStruct((depth, bsz, C_ATT, seq_len), F32),
        jax.ShapeDtypeStruct((bsz, CONV_HALO, C_CONV), F32),
        jax.ShapeDtypeStruct((bsz, POOL_BUF, C_POOL), F32),
    ]
    if kv_prev is None:
        kern = functools.partial(_prompt_kernel, seq_len=seq_len)
        aliases = {}
    else:
        kern = functools.partial(_prompt_kernel_with_alias, n_in=n_in, seq_len=seq_len)
        in_specs += [pl.BlockSpec(memory_space=pl.ANY)] * 2
        operands += list(kv_prev)
        aliases = {n_in: 1, n_in + 1: 2}
    scratch = [
        pltpu.VMEM((TQ, D_MODEL), BF16),
        pltpu.VMEM((TQ, D_IN), F32),
        pltpu.VMEM((N_CONV_SLAB, TQ + U_HALO, LANES), F32),
        pltpu.VMEM((N_POOL_SLAB, TQ + B_HALO, LANES), F32),
        pltpu.VMEM((TQ, 2 * C_ATT), BF16),
        pltpu.VMEM((TQ, 2 * C_ATT), F32),
        pltpu.VMEM((TQ, C_POOL), BF16),
        pltpu.VMEM((TQ, C_POOL), F32),
        pltpu.VMEM((N_ATT_SLAB, seq_len, LANES), F32),
        pltpu.VMEM((N_ATT_SLAB, seq_len, LANES), F32),
        pltpu.VMEM((N_ATT_SLAB, seq_len, LANES), F32),
        pltpu.VMEM((N_ATT_SLAB, seq_len, LANES), F32),
        pltpu.VMEM((N_ATT_SLAB, seq_len, LANES), F32),
        pltpu.VMEM((N_ATT_SLAB, seq_len, LANES), F32),
        pltpu.VMEM((seq_len, D_MODEL), BF16),
        pltpu.VMEM((seq_len, C_ATT), F32),
    ]
    return pl.pallas_call(
        kern,
        out_shape=out_shape,
        grid=(bsz, 2 * nt),
        in_specs=in_specs,
        out_specs=out_specs,
        scratch_shapes=scratch,
        input_output_aliases=aliases,
        compiler_params=pltpu.CompilerParams(
            dimension_semantics=("arbitrary", "arbitrary"),
            vmem_limit_bytes=VMEM_LIMIT_BYTES,
        ),
        name="prompt_layer",
    )(*operands)


def _row_to_col_tile(row):
    return jnp.broadcast_to(row, (LANES, row.shape[1])).T


def _key_multiplicity(win_len):
    t = lax.broadcasted_iota(jnp.int32, (1, win_len), 1)
    delta = win_len - t
    cnt = jnp.zeros((1, win_len), F32)
    for d in DILATIONS:
        hit = jnp.logical_and(delta % d == 0, delta <= d * WINDOW_KEYS)
        cnt = cnt + hit.astype(F32)
    return cnt


def _sample_kernel(x_ref, sc_ref, sp_ref, kt_ref, vt_ref,
                   ng_ref, win_ref, cw_ref, cb_ref, lng_ref, lnb_ref, pw_ref, psc_ref, qg_ref, kg_ref,
                   cos_ref, sin_ref, mavg_ref, wout_ref,
                   y_ref, nc_ref, np_ref, okt_ref, ovt_ref,
                   xcur_s, qr_s, kr_s, v_s, att_s, mixab_s, cg_s, *, pos):
    nsmp = x_ref.shape[0]
    win_len = kt_ref.shape[1]
    layer = pl.program_id(0)
    b = pl.program_id(1)

    @pl.when(jnp.logical_and(layer == 0, b == 0))
    def _load_input():
        xcur_s[...] = x_ref[...]

    @pl.when(b == 0)
    def _prepare():
        x = xcur_s[...]
        ms = jnp.mean(x * x, axis=-1, keepdims=True)
        h = (x * lax.rsqrt(ms + EPS) * ng_ref[...]).astype(BF16)
        proj = jnp.dot(h, win_ref[...], preferred_element_type=F32)

        u = proj[:, OFF_A_VAL:OFF_A_VAL + C_CONV] * jax.nn.sigmoid(proj[:, OFF_A_GLU:OFF_A_GLU + C_CONV])
        conv = u * cw_ref[CONV_HALO:CONV_W, :] + cb_ref[...]
        for w in range(CONV_HALO):
            conv = conv + sc_ref[:, w, :] * cw_ref[w:w + 1, :]
        nc_ref[:, 0:CONV_HALO - 1, :] = sc_ref[:, 1:CONV_HALO, :]
        nc_ref[:, CONV_HALO - 1, :] = u
        mu = jnp.mean(conv, axis=-1, keepdims=True)
        cen = conv - mu
        var = jnp.mean(cen * cen, axis=-1, keepdims=True)
        ln = cen * lax.rsqrt(var + EPS) * lng_ref[...] + lnb_ref[...]
        ya = _silu(ln) * _silu(proj[:, OFF_A_GATE:OFF_A_GATE + C_CONV])

        bval = proj[:, OFF_B_VAL:OFF_B_VAL + C_POOL]
        lane_p = lax.broadcasted_iota(jnp.int32, (nsmp, C_POOL), 1)
        pooled = jnp.zeros((nsmp, C_POOL), F32)
        acc = bval
        done = 1
        for wi, w in enumerate(POOL_WINDOWS):
            for i in range(done, w):
                acc = acc + sp_ref[:, POOL_BUF - i, :]
            done = w
            pooled = jnp.where(lane_p // POOL_GC == wi, acc / float(min(pos + 1, w)), pooled)
        np_ref[:, 0:POOL_BUF - 1, :] = sp_ref[:, 1:POOL_BUF, :]
        np_ref[:, POOL_BUF - 1, :] = bval
        dpool = (pooled - bval).astype(BF16)
        yb = (jnp.dot(dpool, pw_ref[...], preferred_element_type=F32) * psc_ref[...]
              * _silu(proj[:, OFF_B_GATE:OFF_B_GATE + C_POOL]))
        mixab_s[:, 0:C_CONV] = ya
        mixab_s[:, C_CONV:C_CONV + C_POOL] = yb
        cg_s[...] = _silu(proj[:, OFF_C_GATE:OFF_C_GATE + C_ATT])

        q = proj[:, OFF_Q:OFF_Q + C_ATT]
        k = proj[:, OFF_K:OFF_K + C_ATT]
        qn = q * lax.rsqrt(jnp.dot((q * q).astype(BF16), mavg_ref[...], preferred_element_type=F32) + EPS) * qg_ref[...]
        kn = k * lax.rsqrt(jnp.dot((k * k).astype(BF16), mavg_ref[...], preferred_element_type=F32) + EPS) * kg_ref[...]
        for s in range(N_ATT_SLAB):
            cols = slice(s * LANES, (s + 1) * LANES)
            qs, ks = qn[:, cols], kn[:, cols]
            qr_s[:, cols] = (qs * cos_ref[...] + _swap_halves(qs) * sin_ref[...]) * (HEAD_DIM ** -0.5)
            kr_s[:, cols] = ks * cos_ref[...] + _swap_halves(ks) * sin_ref[...]
        v_s[...] = proj[:, OFF_V:OFF_V + C_ATT]
        att_s[...] = jnp.zeros((nsmp, C_ATT), F32)

    rowid = lax.broadcasted_iota(jnp.int32, (nsmp, C_ATT), 0)
    mine = rowid == b

    def pick_row(ref):
        return jnp.sum(jnp.where(mine, ref[...], 0.0), axis=0, keepdims=True)

    q_col = _row_to_col_tile(pick_row(qr_s))
    k_col = _row_to_col_tile(pick_row(kr_s))
    v_col = _row_to_col_tile(pick_row(v_s))
    cnt = _key_multiplicity(win_len)
    reach = cnt > 0.0
    lane_w = lax.broadcasted_iota(jnp.int32, (HEAD_DIM, win_len), 1)
    last_lane = lane_w == win_len - 1
    n_tiles = win_len // LANES
    n_pat = float(len(DILATIONS))

    att_cols = []
    for hd in range(N_HEADS):
        hr = slice(hd * HEAD_DIM, (hd + 1) * HEAD_DIM)
        kt = kt_ref[hr, :]
        vt = vt_ref[hr, :]
        qh = q_col[hr, :]
        kh = k_col[hr, :]
        vh = v_col[hr, :]
        s_win = jnp.sum(kt * jnp.concatenate([qh] * n_tiles, axis=1), axis=0, keepdims=True)
        s_win = jnp.where(reach, s_win, NEG)
        s_new = jnp.sum(kh * qh, axis=0, keepdims=True)[:, 0:1]
        m = jnp.maximum(jnp.max(s_win, axis=-1, keepdims=True), s_new)
        p_win = cnt * jnp.exp(s_win - m)
        p_new = n_pat * jnp.exp(s_new - m)
        l = jnp.sum(p_win, axis=-1, keepdims=True) + p_new
        o = jnp.sum(vt * p_win, axis=-1, keepdims=True) + vh[:, 0:1] * p_new
        att_cols.append(o / l)
        okt_ref[hr, :] = jnp.where(last_lane, jnp.concatenate([kh] * n_tiles, axis=1), pltpu.roll(kt, win_len - 1, 1))
        ovt_ref[hr, :] = jnp.where(last_lane, jnp.concatenate([vh] * n_tiles, axis=1), pltpu.roll(vt, win_len - 1, 1))
    att_col = jnp.concatenate(att_cols, axis=0)
    att_row = jnp.broadcast_to(att_col, (C_ATT, LANES)).T[0:1, :]
    att_s[...] = jnp.where(mine, att_row, att_s[...])

    @pl.when(b == nsmp - 1)
    def _finish():
        yc = att_s[...] * cg_s[...]
        mixed = jnp.concatenate([mixab_s[...], yc], axis=-1).astype(BF16)
        y = xcur_s[...] + jnp.dot(mixed, wout_ref[...], preferred_element_type=F32)
        xcur_s[...] = y
        y_ref[...] = y


def _sample_path(pos, x, state_conv, state_pool, kt, vt, norm_g, w_in, conv_w, conv_b, ln_g, ln_b,
                 pool_wbd, pool_scale, qg, kg, cos_t, sin_t, mavg, w_out):
    nsmp = x.shape[0]
    depth, _, _, win_len = kt.shape
    assert win_len == WINDOW_KEYS * DILATIONS[-1]

    def const(shape):
        nd = len(shape)
        return pl.BlockSpec(shape, lambda l, i: (0,) * nd)

    def per_layer(shape):
        nd = len(shape)
        return pl.BlockSpec((None,) + shape, lambda l, i: (l,) + (0,) * nd)

    cache_spec = pl.BlockSpec((None, None, C_ATT, win_len), lambda l, i: (l, i, 0, 0))
    in_specs = [
        const((nsmp, D_MODEL)),
        per_layer((nsmp, CONV_HALO, C_CONV)),
        per_layer((nsmp, POOL_BUF, C_POOL)),
        cache_spec, cache_spec,
        per_layer((1, D_MODEL)), per_layer((D_MODEL, D_IN)), per_layer((CONV_W, C_CONV)), per_layer((1, C_CONV)),
        per_layer((1, C_CONV)), per_layer((1, C_CONV)), per_layer((C_POOL, C_POOL)), per_layer((1, C_POOL)),
        per_layer((1, C_ATT)), per_layer((1, C_ATT)), const((1, LANES)), const((1, LANES)),
        const((C_ATT, C_ATT)), per_layer((D_MODEL, D_MODEL)),
    ]
    out_specs = [
        const((nsmp, D_MODEL)),
        per_layer((nsmp, CONV_HALO, C_CONV)),
        per_layer((nsmp, POOL_BUF, C_POOL)),
        cache_spec, cache_spec,
    ]
    out_shape = [
        jax.ShapeDtypeStruct((nsmp, D_MODEL), F32),
        jax.ShapeDtypeStruct((depth, nsmp, CONV_HALO, C_CONV), F32),
        jax.ShapeDtypeStruct((depth, nsmp, POOL_BUF, C_POOL), F32),
        jax.ShapeDtypeStruct((depth, nsmp, C_ATT, win_len), F32),
        jax.ShapeDtypeStruct((depth, nsmp, C_ATT, win_len), F32),
    ]
    scratch = [
        pltpu.VMEM((nsmp, D_MODEL), F32),
        pltpu.VMEM((nsmp, C_ATT), F32),
        pltpu.VMEM((nsmp, C_ATT), F32),
        pltpu.VMEM((nsmp, C_ATT), F32),
        pltpu.VMEM((nsmp, C_ATT), F32),
        pltpu.VMEM((nsmp, C_CONV + C_POOL), F32),
        pltpu.VMEM((nsmp, C_ATT), F32),
    ]
    return pl.pallas_call(
        functools.partial(_sample_kernel, pos=pos),
        out_shape=out_shape,
        grid=(depth, nsmp),
        in_specs=in_specs,
        out_specs=out_specs,
        scratch_shapes=scratch,
        compiler_params=pltpu.CompilerParams(
            dimension_semantics=("arbitrary", "arbitrary"),
            vmem_limit_bytes=VMEM_LIMIT_BYTES,
        ),
        name="sample_path",
    )(x, state_conv, state_pool, kt, vt, norm_g, w_in, conv_w, conv_b, ln_g, ln_b, pool_wbd, pool_scale,
      qg, kg, cos_t, sin_t, mavg, w_out)


def _to_channel_major(a):
    depth, bsz, ntok, nh, hd = a.shape
    return jnp.transpose(a, (0, 1, 3, 4, 2)).reshape(depth, bsz, nh * hd, ntok)


def _from_channel_major(a):
    depth, bsz, _, ntok = a.shape
    return jnp.transpose(a.reshape(depth, bsz, N_HEADS, HEAD_DIM, ntok), (0, 1, 4, 2, 3))


def kernel(x_prompt, x_sample, state_conv, state_pool, cache_k_win, cache_v_win, norm_g, w_in, conv_w, conv_b,
           ln_g, ln_b, pool_w, pool_scale, q_norm_g, k_norm_g, w_out):
    depth = w_in.shape[0]
    seq_len = x_prompt.shape[1]
    nsmp = x_sample.shape[0]

    cos_p, sin_p = _rope_tables(jnp.arange(seq_len, dtype=jnp.int32))
    cos_s, sin_s = _rope_tables(jnp.full((1,), PAST_LEN, dtype=jnp.int32))
    mavg = _head_mean_matrix()
    weights = (norm_g[:, None], w_in.astype(BF16), conv_w, conv_b[:, None], ln_g[:, None], ln_b[:, None],
               _pool_block_diag(pool_w).astype(BF16), pool_scale[:, None],
               jnp.tile(q_norm_g, (1, N_HEADS))[:, None], jnp.tile(k_norm_g, (1, N_HEADS))[:, None])
    w_out_b = w_out.astype(BF16)

    xp = x_prompt
    kv = None
    conv_states, pool_states = [], []
    for layer in range(depth):
        xp, kp, vp, cst, pst = _prompt_layer(layer, xp, *weights, cos_p, sin_p, mavg, w_out_b, kv_prev=kv)
        kv = (kp, vp)
        conv_states.append(cst)
        pool_states.append(pst)

    ys, new_conv_s, new_pool_s, new_kt_s, new_vt_s = _sample_path(
        PAST_LEN, x_sample.reshape(nsmp, D_MODEL), state_conv, state_pool,
        _to_channel_major(cache_k_win), _to_channel_major(cache_v_win),
        *weights, cos_s, sin_s, mavg, w_out_b)

    return (xp, ys.reshape(nsmp, 1, D_MODEL), jnp.stack(conv_states), jnp.stack(pool_states),
            _from_channel_major(kv[0]), _from_channel_major(kv[1]),
            new_conv_s, new_pool_s, _from_channel_major(new_kt_s), _from_channel_major(new_vt_s))
```

```python
import functools

import jax
import jax.numpy as jnp
import numpy as np
from jax import lax
from jax.experimental import pallas as pl
from jax.experimental.pallas import tpu as pltpu

F32 = jnp.float32
BF16 = jnp.bfloat16

D_MODEL = 1024
C_CONV = 384
C_POOL = 256
C_ATT = 384
HEAD_DIM = 64
N_HEADS = C_ATT // HEAD_DIM
CONV_W = 31
CONV_HALO = CONV_W - 1
POOL_WINDOWS = (2, 4, 8, 16)
POOL_GC = 64
POOL_BUF = 15
DILATIONS = (1, 4, 16)
WINDOW_KEYS = 128
EPS = 1e-6
ROPE_THETA = 10000.0
D_IN = 3 * C_CONV + 2 * C_POOL + 4 * C_ATT
PAST_LEN = 16384
NEG = -1e30

OFF_A_VAL = 0
OFF_A_GLU = OFF_A_VAL + C_CONV
OFF_A_GATE = OFF_A_GLU + C_CONV
OFF_B_VAL = OFF_A_GATE + C_CONV
OFF_B_GATE = OFF_B_VAL + C_POOL
OFF_Q = OFF_B_GATE + C_POOL
OFF_K = OFF_Q + C_ATT
OFF_V = OFF_K + C_ATT
OFF_C_GATE = OFF_V + C_ATT

LANES = 128
N_CONV_SLAB = C_CONV // LANES
N_POOL_SLAB = C_POOL // LANES
N_ATT_SLAB = C_ATT // LANES
VMEM_LIMIT_BYTES = 56 * 1024 * 1024

TQ = 256
RC = 32
U_HALO = 32
B_HALO = 16
QB = WINDOW_KEYS
UNROLL_NEAR = 5
UNROLL_MID = 3
UNROLL_FAR = 4
UNROLL_REGROUP = 4

AB_Q, AB_K, AB_V, AB_O, AB_M, AB_L, AB_SPARE = range(7)
AB_COUNT = 7
AB_ATT = AB_L


def _silu(x):
    return x * jax.nn.sigmoid(x)


def _rope_tables(positions):
    half = HEAD_DIM // 2
    inv = ROPE_THETA ** (-jnp.arange(half, dtype=F32) / half)
    ang = positions.astype(F32)[:, None] * inv[None, :]
    cos = jnp.cos(ang)
    sin = jnp.sin(ang)
    cos_h = jnp.concatenate([cos, cos], axis=-1)
    sin_h = jnp.concatenate([-sin, sin], axis=-1)
    reps = LANES // HEAD_DIM
    return jnp.tile(cos_h, (1, reps)), jnp.tile(sin_h, (1, reps))


def _head_mean_matrix():
    idx = np.arange(C_ATT) // HEAD_DIM
    return jnp.asarray((idx[:, None] == idx[None, :]).astype(np.float32) / HEAD_DIM, dtype=BF16)


def _pool_block_diag(pool_w):
    out = jnp.zeros((pool_w.shape[0], C_POOL, C_POOL), pool_w.dtype)
    for g in range(len(POOL_WINDOWS)):
        out = out.at[:, g * POOL_GC:(g + 1) * POOL_GC, g * POOL_GC:(g + 1) * POOL_GC].set(pool_w[:, g])
    return out


def _swap_halves(x):
    lane = lax.broadcasted_iota(jnp.int32, x.shape, 1)
    first_half = (lane % HEAD_DIM) < (HEAD_DIM // 2)
    return jnp.where(first_half, pltpu.roll(x, LANES - HEAD_DIM // 2, 1), pltpu.roll(x, HEAD_DIM // 2, 1))


def _pool_means(loads, pos):
    lane = lax.broadcasted_iota(jnp.int32, loads(0, 0).shape, 1)
    lo = lane < POOL_GC
    posf = (pos + 1).astype(F32)
    outs = []
    for slab in range(N_POOL_SLAB):
        w_lo, w_hi = POOL_WINDOWS[2 * slab], POOL_WINDOWS[2 * slab + 1]
        cur = loads(0, slab)
        s = cur
        for i in range(1, w_lo):
            s = s + loads(i, slab)
        s_lo = s
        for i in range(w_lo, w_hi):
            s = s + loads(i, slab)
        s_hi = s
        cnt_lo = jnp.minimum(posf, float(w_lo))
        cnt_hi = jnp.minimum(posf, float(w_hi))
        pooled = jnp.where(lo, s_lo / cnt_lo, s_hi / cnt_hi)
        outs.append(pooled - cur)
    return outs


def _prompt_kernel(x_ref, ng_ref, win_ref, cw_ref, cb_ref, lng_ref, lnb_ref, pw_ref, psc_ref,
                   qg_ref, kg_ref, cos_ref, sin_ref, mavg_ref, wout_ref,
                   y_ref, ko_ref, vo_ref, cst_ref, pst_ref,
                   h_s, proj, u_buf, b_buf, sq_s, msq_s, d_s, yb_s,
                   ab0, ab1, ab2, ab3, ab4, ab5, ab6, mix, cg, *, seq_len):
    ab = (ab0, ab1, ab2, ab3, ab4, ab5, ab6)
    nt = seq_len // TQ
    t = pl.program_id(1)

    @pl.when(t < nt)
    def _phase1():
        row0 = pl.multiple_of(t * TQ, TQ)

        @pl.when(t == 0)
        def _zero_halo():
            u_buf[:, 0:U_HALO, :] = jnp.zeros((N_CONV_SLAB, U_HALO, LANES), F32)
            b_buf[:, 0:B_HALO, :] = jnp.zeros((N_POOL_SLAB, B_HALO, LANES), F32)

        def norm_chunk(i, c):
            r = pl.multiple_of(i * RC, RC)
            x = x_ref[pl.ds(r, RC), :]
            ms = jnp.mean(x * x, axis=-1, keepdims=True)
            h_s[pl.ds(r, RC), :] = (x * lax.rsqrt(ms + EPS) * ng_ref[...]).astype(BF16)
            return c

        lax.fori_loop(0, TQ // RC, norm_chunk, 0, unroll=True)
        proj[...] = jnp.dot(h_s[...], win_ref[...], preferred_element_type=F32)

        def split_chunk(i, c):
            r = pl.multiple_of(i * RC, RC)
            rows = pl.ds(r, RC)
            grow = pl.ds(pl.multiple_of(row0 + r, RC), RC)
            for s in range(N_CONV_SLAB):
                cols = slice(s * LANES, (s + 1) * LANES)
                a_val = proj[rows, OFF_A_VAL + s * LANES:OFF_A_VAL + (s + 1) * LANES]
                a_glu = proj[rows, OFF_A_GLU + s * LANES:OFF_A_GLU + (s + 1) * LANES]
                u_buf[s, pl.ds(U_HALO + r, RC), :] = a_val * jax.nn.sigmoid(a_glu)
                q = proj[rows, OFF_Q + s * LANES:OFF_Q + (s + 1) * LANES]
                k = proj[rows, OFF_K + s * LANES:OFF_K + (s + 1) * LANES]
                sq_s[rows, cols] = (q * q).astype(BF16)
                sq_s[rows, C_ATT + s * LANES:C_ATT + (s + 1) * LANES] = (k * k).astype(BF16)
                ab[AB_V][s, grow, :] = proj[rows, OFF_V + s * LANES:OFF_V + (s + 1) * LANES]
                cg[grow, cols] = _silu(proj[rows, OFF_C_GATE + s * LANES:OFF_C_GATE + (s + 1) * LANES]).astype(BF16)
            for s in range(N_POOL_SLAB):
                b_buf[s, pl.ds(B_HALO + r, RC), :] = proj[rows, OFF_B_VAL + s * LANES:OFF_B_VAL + (s + 1) * LANES]
            pos = row0 + r + lax.broadcasted_iota(jnp.int32, (RC, 1), 0)
            dl = _pool_means(lambda sh, s: b_buf[s, pl.ds(r + B_HALO - sh, RC), :], pos)
            for s in range(N_POOL_SLAB):
                d_s[rows, s * LANES:(s + 1) * LANES] = dl[s].astype(BF16)
            return c

        lax.fori_loop(0, TQ // RC, split_chunk, 0, unroll=2)
        msq_s[:, 0:C_ATT] = jnp.dot(sq_s[:, 0:C_ATT], mavg_ref[...], preferred_element_type=F32)
        msq_s[:, C_ATT:2 * C_ATT] = jnp.dot(sq_s[:, C_ATT:2 * C_ATT], mavg_ref[...], preferred_element_type=F32)
        yb_s[...] = jnp.dot(d_s[...], pw_ref[...], preferred_element_type=F32)

        def mixer_chunk(i, c):
            r = pl.multiple_of(i * RC, RC)
            rows = pl.ds(r, RC)
            grow = pl.ds(pl.multiple_of(row0 + r, RC), RC)
            conv = []
            for s in range(N_CONV_SLAB):
                cols = slice(s * LANES, (s + 1) * LANES)
                acc = jnp.zeros((RC, LANES), F32) + cb_ref[:, cols]
                for w in range(CONV_W):
                    acc = acc + u_buf[s, pl.ds(r + (U_HALO - CONV_HALO) + w, RC), :] * cw_ref[w:w + 1, cols]
                conv.append(acc)
            mu = jnp.sum(conv[0] + conv[1] + conv[2], axis=-1, keepdims=True) * (1.0 / C_CONV)
            cen = [cv - mu for cv in conv]
            var = jnp.sum(cen[0] * cen[0] + cen[1] * cen[1] + cen[2] * cen[2], axis=-1, keepdims=True) * (1.0 / C_CONV)
            rstd = lax.rsqrt(var + EPS)
            for s in range(N_CONV_SLAB):
                cols = slice(s * LANES, (s + 1) * LANES)
                ln = cen[s] * rstd * lng_ref[:, cols] + lnb_ref[:, cols]
                gate = proj[rows, OFF_A_GATE + s * LANES:OFF_A_GATE + (s + 1) * LANES]
                mix[grow, cols] = (_silu(ln) * _silu(gate)).astype(BF16)
            for s in range(N_POOL_SLAB):
                cols = slice(s * LANES, (s + 1) * LANES)
                gate = proj[rows, OFF_B_GATE + s * LANES:OFF_B_GATE + (s + 1) * LANES]
                yb = yb_s[rows, cols] * psc_ref[:, cols] * _silu(gate)
                mix[grow, C_CONV + s * LANES:C_CONV + (s + 1) * LANES] = yb.astype(BF16)
            cos = cos_ref[grow, :]
            sin = sin_ref[grow, :]
            for s in range(N_ATT_SLAB):
                cols = slice(s * LANES, (s + 1) * LANES)
                q = proj[rows, OFF_Q + s * LANES:OFF_Q + (s + 1) * LANES]
                qn = q * lax.rsqrt(msq_s[rows, cols] + EPS) * qg_ref[:, cols]
                qr = qn * cos + _swap_halves(qn) * sin
                ab[AB_Q][s, grow, :] = qr * (HEAD_DIM ** -0.5)
                k = proj[rows, OFF_K + s * LANES:OFF_K + (s + 1) * LANES]
                kn = k * lax.rsqrt(msq_s[rows, C_ATT + s * LANES:C_ATT + (s + 1) * LANES] + EPS) * kg_ref[:, cols]
                ab[AB_K][s, grow, :] = kn * cos + _swap_halves(kn) * sin
            return c

        lax.fori_loop(0, TQ // RC, mixer_chunk, 0, unroll=2)

        for s in range(N_ATT_SLAB):
            ko_ref[s * LANES:(s + 1) * LANES, :] = ab[AB_K][s, pl.ds(row0, TQ), :].T
            vo_ref[s * LANES:(s + 1) * LANES, :] = ab[AB_V][s, pl.ds(row0, TQ), :].T

        @pl.when(t == nt - 1)
        def _write_state():
            for s in range(N_CONV_SLAB):
                cst_ref[:, s * LANES:(s + 1) * LANES] = u_buf[s, TQ + U_HALO - CONV_HALO:TQ + U_HALO, :]
            for s in range(N_POOL_SLAB):
                pst_ref[:, s * LANES:(s + 1) * LANES] = b_buf[s, TQ + B_HALO - POOL_BUF:TQ + B_HALO, :]

        u_buf[:, 0:U_HALO, :] = u_buf[:, TQ:TQ + U_HALO, :]
        b_buf[:, 0:B_HALO, :] = b_buf[:, TQ:TQ + B_HALO, :]

    @pl.when(t == nt - 1)
    def _attention():
        d_mid, d_far = DILATIONS[1], DILATIONS[2]
        ratio = d_far // d_mid
        stream_len = seq_len // d_mid
        lane = lax.broadcasted_iota(jnp.int32, (QB, LANES), 1)
        rowi = lax.broadcasted_iota(jnp.int32, (QB, LANES), 0)
        lo = lane < HEAD_DIM
        cur_ok = lane <= rowi
        prev_ok = lane >= rowi
        cur_ok2 = jnp.concatenate([cur_ok, cur_ok], axis=0)
        prev_cur_ok2 = jnp.concatenate([jnp.concatenate([prev_ok, prev_ok], axis=0), cur_ok2], axis=1)

        def attend(q, keys, vals, mask):
            qa = jnp.where(lo, q, 0.0).astype(BF16)
            qb = jnp.where(lo, 0.0, q).astype(BF16)
            q2 = jnp.concatenate([qa, qb], axis=0)
            sc = lax.dot_general(q2, keys.astype(BF16), (((1,), (1,)), ((), ())), preferred_element_type=F32)
            sc = jnp.where(mask, sc, NEG)
            m = jnp.max(sc, axis=-1, keepdims=True)
            p = jnp.exp(sc - m).astype(BF16)
            v1 = jnp.concatenate([vals.astype(BF16), jnp.ones(vals.shape, BF16)], axis=1)
            ol = jnp.dot(p, v1, preferred_element_type=F32)
            o_u = jnp.where(lo, ol[0:QB, 0:LANES], ol[QB:2 * QB, 0:LANES])
            l_u = jnp.where(lo, ol[0:QB, LANES:2 * LANES], ol[QB:2 * QB, LANES:2 * LANES])
            m_u = jnp.where(lo, m[0:QB], m[QB:2 * QB])
            return o_u, m_u, l_u

        def block(bufs, rows, krows, mask, acc=None):
            b_q, b_k, b_v = bufs
            outs = []
            for s in range(N_ATT_SLAB):
                o_u, m_u, l_u = attend(b_q[s, rows, :], b_k[s, krows, :], b_v[s, krows, :], mask)
                if acc is not None:
                    a_o, a_m, a_l = acc
                    m_old = a_m[s, rows, :]
                    m_new = jnp.maximum(m_old, m_u)
                    w_old = jnp.exp(m_old - m_new)
                    w_u = jnp.exp(m_u - m_new)
                    o_u = a_o[s, rows, :] * w_old + o_u * w_u
                    l_u = a_l[s, rows, :] * w_old + l_u * w_u
                    m_u = m_new
                outs.append((o_u, m_u, l_u))
            return outs

        def store(acc, rows, outs):
            a_o, a_m, a_l = acc
            for s, (o_u, m_u, l_u) in enumerate(outs):
                a_o[s, rows, :] = o_u
                a_m[s, rows, :] = m_u
                a_l[s, rows, :] = l_u

        qkv = (ab[AB_Q], ab[AB_K], ab[AB_V])
        acc = (ab[AB_O], ab[AB_M], ab[AB_L])
        store(acc, pl.ds(0, QB), block(qkv, pl.ds(0, QB), pl.ds(0, QB), cur_ok2))

        def near_unit(u, c):
            start = pl.multiple_of(u * QB, QB)
            rows = pl.ds(start, QB)
            store(acc, rows, block(qkv, rows, pl.ds(start - QB, 2 * QB), prev_cur_ok2))
            return c

        lax.fori_loop(1, seq_len // QB, near_unit, 0, unroll=UNROLL_NEAR)

        order = (AB_SPARE, AB_Q, AB_K, AB_V, AB_O, AB_M, AB_L)
        for dst, src in zip(order[:-1], order[1:]):
            def regroup(i, c, dst=ab[dst], src=ab[src]):
                phase = i // (stream_len // QB)
                blk = i % (stream_len // QB)
                tiles = [src[s, pl.ds(phase + d_mid * QB * blk, QB, stride=d_mid), :] for s in range(N_ATT_SLAB)]
                for s in range(N_ATT_SLAB):
                    dst[s, pl.ds(pl.multiple_of(i * QB, QB), QB), :] = tiles[s]
                return c

            lax.fori_loop(0, seq_len // QB, regroup, 0, unroll=UNROLL_REGROUP)
        qkv = tuple(ab[i] for i in order[0:3])
        acc = tuple(ab[i] for i in order[3:6])
        att = ab[order[6]]

        def mid_first(ph, c):
            rows = pl.ds(pl.multiple_of(ph * stream_len, QB), QB)
            store(acc, rows, block(qkv, rows, rows, cur_ok2, acc))
            return c

        lax.fori_loop(0, d_mid, mid_first, 0, unroll=2)
        later_blocks = stream_len // QB - 1

        def mid_unit(i, c):
            start = pl.multiple_of((i // later_blocks) * stream_len + (i % later_blocks + 1) * QB, QB)
            rows = pl.ds(start, QB)
            store(acc, rows, block(qkv, rows, pl.ds(start - QB, 2 * QB), prev_cur_ok2, acc))
            return c

        lax.fori_loop(0, d_mid * later_blocks, mid_unit, 0, unroll=UNROLL_MID)

        def far_unit(r, c):
            rows = pl.ds((r % d_mid) * stream_len + r // d_mid, QB, stride=ratio)
            outs = block(qkv, rows, rows, cur_ok2, acc)
            for s, (o_n, _, l_n) in enumerate(outs):
                att[s, pl.ds(r, QB, stride=d_far), :] = o_n / l_n
            return c

        lax.fori_loop(0, d_far, far_unit, 0, unroll=UNROLL_FAR)

    @pl.when(t >= nt)
    def _phase2():
        row0 = pl.multiple_of((t - nt) * TQ, TQ)

        def att_chunk(i, c):
            r = pl.multiple_of(i * RC, RC)
            grow = pl.ds(pl.multiple_of(row0 + r, RC), RC)
            for s in range(N_ATT_SLAB):
                cols = slice(s * LANES, (s + 1) * LANES)
                yc = ab[AB_ATT][s, grow, :] * cg[grow, cols].astype(F32)
                mix[grow, C_CONV + C_POOL + s * LANES:C_CONV + C_POOL + (s + 1) * LANES] = yc.astype(BF16)
            return c

        lax.fori_loop(0, TQ // RC, att_chunk, 0)
        y_ref[...] = x_ref[...] + jnp.dot(mix[pl.ds(row0, TQ), :], wout_ref[...], preferred_element_type=F32)


def _prompt_kernel_with_alias(*refs, n_in, seq_len):
    _prompt_kernel(*refs[:n_in], *refs[n_in + 2:], seq_len=seq_len)


def _prompt_layer(layer, x, norm_g, w_in, conv_w, conv_b, ln_g, ln_b, pool_wbd, pool_scale, qg, kg, cos_t, sin_t,
                  mavg, w_out, kv_prev=None):
    bsz, seq_len, _ = x.shape
    depth = w_in.shape[0]
    assert seq_len % TQ == 0 and seq_len == QB * DILATIONS[-1] and DILATIONS[0] == 1
    nt = seq_len // TQ

    def const(shape):
        nd = len(shape)
        return pl.BlockSpec(shape, lambda b, t: (0,) * nd)

    def resident(shape):
        nd = len(shape)
        return pl.BlockSpec(shape, lambda b, t: (0,) * nd, pipeline_mode=pl.Buffered(1))

    def per_layer(shape, **kw):
        nd = len(shape)
        return pl.BlockSpec((None,) + shape, lambda b, t: (layer,) + (0,) * nd, **kw)

    in_specs = [
        pl.BlockSpec((None, TQ, D_MODEL), lambda b, t: (b, t % nt, 0)),
        per_layer((1, D_MODEL)),
        per_layer((D_MODEL, D_IN), pipeline_mode=pl.Buffered(1)),
        per_layer((CONV_W, C_CONV)),
        per_layer((1, C_CONV)),
        per_layer((1, C_CONV)),
        per_layer((1, C_CONV)),
        per_layer((C_POOL, C_POOL)),
        per_layer((1, C_POOL)),
        per_layer((1, C_ATT)),
        per_layer((1, C_ATT)),
        resident((seq_len, LANES)),
        resident((seq_len, LANES)),
        const((C_ATT, C_ATT)),
        per_layer((D_MODEL, D_MODEL), pipeline_mode=pl.Buffered(1)),
    ]
    operands = [x, norm_g, w_in, conv_w, conv_b, ln_g, ln_b, pool_wbd, pool_scale, qg, kg, cos_t, sin_t, mavg, w_out]
    n_in = len(operands)
    kv_spec = pl.BlockSpec((None, None, C_ATT, TQ), lambda b, t: (layer, b, 0, jnp.minimum(t, nt - 1)))
    out_specs = [
        pl.BlockSpec((None, TQ, D_MODEL), lambda b, t: (b, jnp.maximum(t - nt, 0), 0)),
        kv_spec,
        kv_spec,
        pl.BlockSpec((None, CONV_HALO, C_CONV), lambda b, t: (b, 0, 0)),
        pl.BlockSpec((None, POOL_BUF, C_POOL), lambda b, t: (b, 0, 0)),
    ]
    out_shape = [
        jax.ShapeDtypeStruct((bsz, seq_len, D_MODEL), F32),
        jax.ShapeDtypeStruct((depth, bsz, C_ATT, seq_len), F32),
        jax.ShapeDtypeStruct((depth, bsz, C_ATT, seq_len), F32),
        jax.ShapeDtypeStruct((bsz, CONV_HALO, C_CONV), F32),
        jax.ShapeDtypeStruct((bsz, POOL_BUF, C_POOL), F32),
    ]
    if kv_prev is None:
        kern = functools.partial(_prompt_kernel, seq_len=seq_len)
        aliases = {}
    else:
        kern = functools.partial(_prompt_kernel_with_alias, n_in=n_in, seq_len=seq_len)
        in_specs += [pl.BlockSpec(memory_space=pl.ANY)] * 2
        operands += list(kv_prev)
        aliases = {n_in: 1, n_in + 1: 2}
    scratch = [
        pltpu.VMEM((TQ, D_MODEL), BF16),
        pltpu.VMEM((TQ, D_IN), F32),
        pltpu.VMEM((N_CONV_SLAB, TQ + U_HALO, LANES), F32),
        pltpu.VMEM((N_POOL_SLAB, TQ + B_HALO, LANES), F32),
        pltpu.VMEM((TQ, 2 * C_ATT), BF16),
        pltpu.VMEM((TQ, 2 * C_ATT), F32),
        pltpu.VMEM((TQ, C_POOL), BF16),
        pltpu.VMEM((TQ, C_POOL), F32),
        *[pltpu.VMEM((N_ATT_SLAB, seq_len, LANES), F32) for _ in range(AB_COUNT)],
        pltpu.VMEM((seq_len, D_MODEL), BF16),
        pltpu.VMEM((seq_len, C_ATT), BF16),
    ]
    return pl.pallas_call(
        kern,
        out_shape=out_shape,
        grid=(bsz, 2 * nt),
        in_specs=in_specs,
        out_specs=out_specs,
        scratch_shapes=scratch,
        input_output_aliases=aliases,
        compiler_params=pltpu.CompilerParams(
            dimension_semantics=("arbitrary", "arbitrary"),
            vmem_limit_bytes=VMEM_LIMIT_BYTES,
        ),
        name="prompt_layer",
    )(*operands)


def _row_to_col_tile(row):
    return jnp.broadcast_to(row, (LANES, row.shape[1])).T


def _key_multiplicity(win_len):
    t = lax.broadcasted_iota(jnp.int32, (1, win_len), 1)
    delta = win_len - t
    cnt = jnp.zeros((1, win_len), F32)
    for d in DILATIONS:
        hit = jnp.logical_and(delta % d == 0, delta <= d * WINDOW_KEYS)
        cnt = cnt + hit.astype(F32)
    return cnt


def _sample_kernel(x_ref, sc_ref, sp_ref, kt_ref, vt_ref,
                   ng_ref, win_ref, cw_ref, cb_ref, lng_ref, lnb_ref, pw_ref, psc_ref, qg_ref, kg_ref,
                   cos_ref, sin_ref, mavg_ref, wout_ref,
                   y_ref, nc_ref, np_ref, okt_ref, ovt_ref,
                   xcur_s, qr_s, kr_s, v_s, att_s, mixab_s, cg_s, *, pos):
    nsmp = x_ref.shape[0]
    win_len = kt_ref.shape[1]
    layer = pl.program_id(0)
    b = pl.program_id(1)

    @pl.when(jnp.logical_and(layer == 0, b == 0))
    def _load_input():
        xcur_s[...] = x_ref[...]

    @pl.when(b == 0)
    def _prepare():
        x = xcur_s[...]
        ms = jnp.mean(x * x, axis=-1, keepdims=True)
        h = (x * lax.rsqrt(ms + EPS) * ng_ref[...]).astype(BF16)
        proj = jnp.dot(h, win_ref[...], preferred_element_type=F32)

        u = proj[:, OFF_A_VAL:OFF_A_VAL + C_CONV] * jax.nn.sigmoid(proj[:, OFF_A_GLU:OFF_A_GLU + C_CONV])
        conv = u * cw_ref[CONV_HALO:CONV_W, :] + cb_ref[...]
        for w in range(CONV_HALO):
            conv = conv + sc_ref[:, w, :] * cw_ref[w:w + 1, :]
        nc_ref[:, 0:CONV_HALO - 1, :] = sc_ref[:, 1:CONV_HALO, :]
        nc_ref[:, CONV_HALO - 1, :] = u
        mu = jnp.mean(conv, axis=-1, keepdims=True)
        cen = conv - mu
        var = jnp.mean(cen * cen, axis=-1, keepdims=True)
        ln = cen * lax.rsqrt(var + EPS) * lng_ref[...] + lnb_ref[...]
        ya = _silu(ln) * _silu(proj[:, OFF_A_GATE:OFF_A_GATE + C_CONV])

        bval = proj[:, OFF_B_VAL:OFF_B_VAL + C_POOL]
        lane_p = lax.broadcasted_iota(jnp.int32, (nsmp, C_POOL), 1)
        pooled = jnp.zeros((nsmp, C_POOL), F32)
        acc = bval
        done = 1
        for wi, w in enumerate(POOL_WINDOWS):
            for i in range(done, w):
                acc = acc + sp_ref[:, POOL_BUF - i, :]
            done = w
            pooled = jnp.where(lane_p // POOL_GC == wi, acc / float(min(pos + 1, w)), pooled)
        np_ref[:, 0:POOL_BUF - 1, :] = sp_ref[:, 1:POOL_BUF, :]
        np_ref[:, POOL_BUF - 1, :] = bval
        dpool = (pooled - bval).astype(BF16)
        yb = (jnp.dot(dpool, pw_ref[...], preferred_element_type=F32) * psc_ref[...]
              * _silu(proj[:, OFF_B_GATE:OFF_B_GATE + C_POOL]))
        mixab_s[:, 0:C_CONV] = ya
        mixab_s[:, C_CONV:C_CONV + C_POOL] = yb
        cg_s[...] = _silu(proj[:, OFF_C_GATE:OFF_C_GATE + C_ATT])

        q = proj[:, OFF_Q:OFF_Q + C_ATT]
        k = proj[:, OFF_K:OFF_K + C_ATT]
        qn = q * lax.rsqrt(jnp.dot((q * q).astype(BF16), mavg_ref[...], preferred_element_type=F32) + EPS) * qg_ref[...]
        kn = k * lax.rsqrt(jnp.dot((k * k).astype(BF16), mavg_ref[...], preferred_element_type=F32) + EPS) * kg_ref[...]
        for s in range(N_ATT_SLAB):
            cols = slice(s * LANES, (s + 1) * LANES)
            qs, ks = qn[:, cols], kn[:, cols]
            qr_s[:, cols] = (qs * cos_ref[...] + _swap_halves(qs) * sin_ref[...]) * (HEAD_DIM ** -0.5)
            kr_s[:, cols] = ks * cos_ref[...] + _swap_halves(ks) * sin_ref[...]
        v_s[...] = proj[:, OFF_V:OFF_V + C_ATT]
        att_s[...] = jnp.zeros((nsmp, C_ATT), F32)

    rowid = lax.broadcasted_iota(jnp.int32, (nsmp, C_ATT), 0)
    mine = rowid == b

    def pick_row(ref):
        return jnp.sum(jnp.where(mine, ref[...], 0.0), axis=0, keepdims=True)

    q_col = _row_to_col_tile(pick_row(qr_s))
    k_col = _row_to_col_tile(pick_row(kr_s))
    v_col = _row_to_col_tile(pick_row(v_s))
    cnt = _key_multiplicity(win_len)
    reach = cnt > 0.0
    lane_w = lax.broadcasted_iota(jnp.int32, (HEAD_DIM, win_len), 1)
    last_lane = lane_w == win_len - 1
    n_tiles = win_len // LANES
    n_pat = float(len(DILATIONS))

    att_cols = []
    for hd in range(N_HEADS):
        hr = slice(hd * HEAD_DIM, (hd + 1) * HEAD_DIM)
        kt = kt_ref[hr, :]
        vt = vt_ref[hr, :]
        qh = q_col[hr, :]
        kh = k_col[hr, :]
        vh = v_col[hr, :]
        s_win = jnp.sum(kt * jnp.concatenate([qh] * n_tiles, axis=1), axis=0, keepdims=True)
        s_win = jnp.where(reach, s_win, NEG)
        s_new = jnp.sum(kh * qh, axis=0, keepdims=True)[:, 0:1]
        m = jnp.maximum(jnp.max(s_win, axis=-1, keepdims=True), s_new)
        p_win = cnt * jnp.exp(s_win - m)
        p_new = n_pat * jnp.exp(s_new - m)
        l = jnp.sum(p_win, axis=-1, keepdims=True) + p_new
        o = jnp.sum(vt * p_win, axis=-1, keepdims=True) + vh[:, 0:1] * p_new
        att_cols.append(o / l)
        okt_ref[hr, :] = jnp.where(last_lane, jnp.concatenate([kh] * n_tiles, axis=1), pltpu.roll(kt, win_len - 1, 1))
        ovt_ref[hr, :] = jnp.where(last_lane, jnp.concatenate([vh] * n_tiles, axis=1), pltpu.roll(vt, win_len - 1, 1))
    att_col = jnp.concatenate(att_cols, axis=0)
    att_row = jnp.broadcast_to(att_col, (C_ATT, LANES)).T[0:1, :]
    att_s[...] = jnp.where(mine, att_row, att_s[...])

    @pl.when(b == nsmp - 1)
    def _finish():
        yc = att_s[...] * cg_s[...]
        mixed = jnp.concatenate([mixab_s[...], yc], axis=-1).astype(BF16)
        y = xcur_s[...] + jnp.dot(mixed, wout_ref[...], preferred_element_type=F32)
        xcur_s[...] = y
        y_ref[...] = y


def _sample_path(pos, x, state_conv, state_pool, kt, vt, norm_g, w_in, conv_w, conv_b, ln_g, ln_b,
                 pool_wbd, pool_scale, qg, kg, cos_t, sin_t, mavg, w_out):
    nsmp = x.shape[0]
    depth, _, _, win_len = kt.shape
    assert win_len == WINDOW_KEYS * DILATIONS[-1]

    def const(shape):
        nd = len(shape)
        return pl.BlockSpec(shape, lambda l, i: (0,) * nd)

    def per_layer(shape):
        nd = len(shape)
        return pl.BlockSpec((None,) + shape, lambda l, i: (l,) + (0,) * nd)

    cache_spec = pl.BlockSpec((None, None, C_ATT, win_len), lambda l, i: (l, i, 0, 0))
    in_specs = [
        const((nsmp, D_MODEL)),
        per_layer((nsmp, CONV_HALO, C_CONV)),
        per_layer((nsmp, POOL_BUF, C_POOL)),
        cache_spec, cache_spec,
        per_layer((1, D_MODEL)), per_layer((D_MODEL, D_IN)), per_layer((CONV_W, C_CONV)), per_layer((1, C_CONV)),
        per_layer((1, C_CONV)), per_layer((1, C_CONV)), per_layer((C_POOL, C_POOL)), per_layer((1, C_POOL)),
        per_layer((1, C_ATT)), per_layer((1, C_ATT)), const((1, LANES)), const((1, LANES)),
        const((C_ATT, C_ATT)), per_layer((D_MODEL, D_MODEL)),
    ]
    out_specs = [
        const((nsmp, D_MODEL)),
        per_layer((nsmp, CONV_HALO, C_CONV)),
        per_layer((nsmp, POOL_BUF, C_POOL)),
        cache_spec, cache_spec,
    ]
    out_shape = [
        jax.ShapeDtypeStruct((nsmp, D_MODEL), F32),
        jax.ShapeDtypeStruct((depth, nsmp, CONV_HALO, C_CONV), F32),
        jax.ShapeDtypeStruct((depth, nsmp, POOL_BUF, C_POOL), F32),
        jax.ShapeDtypeStruct((depth, nsmp, C_ATT, win_len), F32),
        jax.ShapeDtypeStruct((depth, nsmp, C_ATT, win_len), F32),
    ]
    scratch = [
        pltpu.VMEM((nsmp, D_MODEL), F32),
        pltpu.VMEM((nsmp, C_ATT), F32),
        pltpu.VMEM((nsmp, C_ATT), F32),
        pltpu.VMEM((nsmp, C_ATT), F32),
        pltpu.VMEM((nsmp, C_ATT), F32),
        pltpu.VMEM((nsmp, C_CONV + C_POOL), F32),
        pltpu.VMEM((nsmp, C_ATT), F32),
    ]
    return pl.pallas_call(
        functools.partial(_sample_kernel, pos=pos),
        out_shape=out_shape,
        grid=(depth, nsmp),
        in_specs=in_specs,
        out_specs=out_specs,
        scratch_shapes=scratch,
        compiler_params=pltpu.CompilerParams(
            dimension_semantics=("arbitrary", "arbitrary"),
            vmem_limit_bytes=VMEM_LIMIT_BYTES,
        ),
        name="sample_path",
    )(x, state_conv, state_pool, kt, vt, norm_g, w_in, conv_w, conv_b, ln_g, ln_b, pool_wbd, pool_scale,
      qg, kg, cos_t, sin_t, mavg, w_out)


def _to_channel_major(a):
    depth, bsz, ntok, nh, hd = a.shape
    return jnp.transpose(a, (0, 1, 3, 4, 2)).reshape(depth, bsz, nh * hd, ntok)


def _from_channel_major(a):
    depth, bsz, _, ntok = a.shape
    return jnp.transpose(a.reshape(depth, bsz, N_HEADS, HEAD_DIM, ntok), (0, 1, 4, 2, 3))


def kernel(x_prompt, x_sample, state_conv, state_pool, cache_k_win, cache_v_win, norm_g, w_in, conv_w, conv_b,
           ln_g, ln_b, pool_w, pool_scale, q_norm_g, k_norm_g, w_out):
    depth = w_in.shape[0]
    seq_len = x_prompt.shape[1]
    nsmp = x_sample.shape[0]

    cos_p, sin_p = _rope_tables(jnp.arange(seq_len, dtype=jnp.int32))
    cos_s, sin_s = _rope_tables(jnp.full((1,), PAST_LEN, dtype=jnp.int32))
    mavg = _head_mean_matrix()
    weights = (norm_g[:, None], w_in.astype(BF16), conv_w, conv_b[:, None], ln_g[:, None], ln_b[:, None],
               _pool_block_diag(pool_w).astype(BF16), pool_scale[:, None],
               jnp.tile(q_norm_g, (1, N_HEADS))[:, None], jnp.tile(k_norm_g, (1, N_HEADS))[:, None])
    w_out_b = w_out.astype(BF16)

    xp = x_prompt
    kv = None
    conv_states, pool_states = [], []
    for layer in range(depth):
        xp, kp, vp, cst, pst = _prompt_layer(layer, xp, *weights, cos_p, sin_p, mavg, w_out_b, kv_prev=kv)
        kv = (kp, vp)
        conv_states.append(cst)
        pool_states.append(pst)

    ys, new_conv_s, new_pool_s, new_kt_s, new_vt_s = _sample_path(
        PAST_LEN, x_sample.reshape(nsmp, D_MODEL), state_conv, state_pool,
        _to_channel_major(cache_k_win), _to_channel_major(cache_v_win),
        *weights, cos_s, sin_s, mavg, w_out_b)

    return (xp, ys.reshape(nsmp, 1, D_MODEL), jnp.stack(conv_states), jnp.stack(pool_states),
            _from_channel_major(kv[0]), _from_channel_major(kv[1]),
            new_conv_s, new_pool_s, _from_channel_major(new_kt_s), _from_channel_major(new_vt_s))
```

```python
import functools

import jax
import jax.numpy as jnp
import numpy as np
from jax import lax
from jax.experimental import pallas as pl
from jax.experimental.pallas import tpu as pltpu

F32 = jnp.float32
BF16 = jnp.bfloat16

D_MODEL = 1024
C_CONV = 384
C_POOL = 256
C_ATT = 384
HEAD_DIM = 64
N_HEADS = C_ATT // HEAD_DIM
CONV_W = 31
CONV_HALO = CONV_W - 1
POOL_WINDOWS = (2, 4, 8, 16)
POOL_GC = 64
POOL_BUF = 15
DILATIONS = (1, 4, 16)
WINDOW_KEYS = 128
EPS = 1e-6
ROPE_THETA = 10000.0
D_IN = 3 * C_CONV + 2 * C_POOL + 4 * C_ATT
PAST_LEN = 16384
NEG = -1e30

OFF_A_VAL = 0
OFF_A_GLU = OFF_A_VAL + C_CONV
OFF_A_GATE = OFF_A_GLU + C_CONV
OFF_B_VAL = OFF_A_GATE + C_CONV
OFF_B_GATE = OFF_B_VAL + C_POOL
OFF_Q = OFF_B_GATE + C_POOL
OFF_K = OFF_Q + C_ATT
OFF_V = OFF_K + C_ATT
OFF_C_GATE = OFF_V + C_ATT

LANES = 128
N_CONV_SLAB = C_CONV // LANES
N_POOL_SLAB = C_POOL // LANES
N_ATT_SLAB = C_ATT // LANES
VMEM_LIMIT_BYTES = 56 * 1024 * 1024

TQ = 256
RC = 32
U_HALO = 32
B_HALO = 16
QB = WINDOW_KEYS
UNROLL_NEAR = 15
UNROLL_MID = 6
UNROLL_FAR = 8
UNROLL_REGROUP = 4

AB_Q, AB_K, AB_V, AB_O, AB_M, AB_L, AB_SPARE = range(7)
AB_COUNT = 7
AB_ATT = AB_L


def _silu(x):
    return x * jax.nn.sigmoid(x)


def _rope_tables(positions):
    half = HEAD_DIM // 2
    inv = ROPE_THETA ** (-jnp.arange(half, dtype=F32) / half)
    ang = positions.astype(F32)[:, None] * inv[None, :]
    cos = jnp.cos(ang)
    sin = jnp.sin(ang)
    cos_h = jnp.concatenate([cos, cos], axis=-1)
    sin_h = jnp.concatenate([-sin, sin], axis=-1)
    reps = LANES // HEAD_DIM
    return jnp.tile(cos_h, (1, reps)), jnp.tile(sin_h, (1, reps))


def _head_mean_matrix():
    idx = np.arange(C_ATT) // HEAD_DIM
    return jnp.asarray((idx[:, None] == idx[None, :]).astype(np.float32) / HEAD_DIM, dtype=BF16)


def _pool_block_diag(pool_w):
    out = jnp.zeros((pool_w.shape[0], C_POOL, C_POOL), pool_w.dtype)
    for g in range(len(POOL_WINDOWS)):
        out = out.at[:, g * POOL_GC:(g + 1) * POOL_GC, g * POOL_GC:(g + 1) * POOL_GC].set(pool_w[:, g])
    return out


def _swap_halves(x):
    lane = lax.broadcasted_iota(jnp.int32, x.shape, 1)
    first_half = (lane % HEAD_DIM) < (HEAD_DIM // 2)
    return jnp.where(first_half, pltpu.roll(x, LANES - HEAD_DIM // 2, 1), pltpu.roll(x, HEAD_DIM // 2, 1))


def _pool_means(loads, pos):
    lane = lax.broadcasted_iota(jnp.int32, loads(0, 0).shape, 1)
    lo = lane < POOL_GC
    posf = (pos + 1).astype(F32)
    outs = []
    for slab in range(N_POOL_SLAB):
        w_lo, w_hi = POOL_WINDOWS[2 * slab], POOL_WINDOWS[2 * slab + 1]
        cur = loads(0, slab)
        s = cur
        for i in range(1, w_lo):
            s = s + loads(i, slab)
        s_lo = s
        for i in range(w_lo, w_hi):
            s = s + loads(i, slab)
        s_hi = s
        cnt_lo = jnp.minimum(posf, float(w_lo))
        cnt_hi = jnp.minimum(posf, float(w_hi))
        pooled = jnp.where(lo, s_lo / cnt_lo, s_hi / cnt_hi)
        outs.append(pooled - cur)
    return outs


def _prompt_kernel(x_ref, ng_ref, win_ref, cw_ref, cb_ref, lng_ref, lnb_ref, pw_ref, psc_ref,
                   qg_ref, kg_ref, cos_ref, sin_ref, mavg_ref, wout_ref,
                   y_ref, ko_ref, vo_ref, cst_ref, pst_ref,
                   h_s, proj, u_buf, b_buf, sq_s, msq_s, d_s, yb_s,
                   ab0, ab1, ab2, ab3, ab4, ab5, ab6, mix, cg, *, seq_len):
    ab = (ab0, ab1, ab2, ab3, ab4, ab5, ab6)
    nt = seq_len // TQ
    t = pl.program_id(1)

    @pl.when(t < nt)
    def _phase1():
        row0 = pl.multiple_of(t * TQ, TQ)

        @pl.when(t == 0)
        def _zero_halo():
            u_buf[:, 0:U_HALO, :] = jnp.zeros((N_CONV_SLAB, U_HALO, LANES), F32)
            b_buf[:, 0:B_HALO, :] = jnp.zeros((N_POOL_SLAB, B_HALO, LANES), F32)

        def norm_chunk(i, c):
            r = pl.multiple_of(i * RC, RC)
            x = x_ref[pl.ds(r, RC), :]
            ms = jnp.mean(x * x, axis=-1, keepdims=True)
            h_s[pl.ds(r, RC), :] = (x * lax.rsqrt(ms + EPS) * ng_ref[...]).astype(BF16)
            return c

        lax.fori_loop(0, TQ // RC, norm_chunk, 0, unroll=True)
        proj[...] = jnp.dot(h_s[...], win_ref[...], preferred_element_type=F32)

        def split_chunk(i, c):
            r = pl.multiple_of(i * RC, RC)
            rows = pl.ds(r, RC)
            grow = pl.ds(pl.multiple_of(row0 + r, RC), RC)
            for s in range(N_CONV_SLAB):
                cols = slice(s * LANES, (s + 1) * LANES)
                a_val = proj[rows, OFF_A_VAL + s * LANES:OFF_A_VAL + (s + 1) * LANES]
                a_glu = proj[rows, OFF_A_GLU + s * LANES:OFF_A_GLU + (s + 1) * LANES]
                u_buf[s, pl.ds(U_HALO + r, RC), :] = a_val * jax.nn.sigmoid(a_glu)
                q = proj[rows, OFF_Q + s * LANES:OFF_Q + (s + 1) * LANES]
                k = proj[rows, OFF_K + s * LANES:OFF_K + (s + 1) * LANES]
                sq_s[rows, cols] = (q * q).astype(BF16)
                sq_s[rows, C_ATT + s * LANES:C_ATT + (s + 1) * LANES] = (k * k).astype(BF16)
                ab[AB_V][s, grow, :] = proj[rows, OFF_V + s * LANES:OFF_V + (s + 1) * LANES]
                cg[grow, cols] = _silu(proj[rows, OFF_C_GATE + s * LANES:OFF_C_GATE + (s + 1) * LANES]).astype(BF16)
            for s in range(N_POOL_SLAB):
                b_buf[s, pl.ds(B_HALO + r, RC), :] = proj[rows, OFF_B_VAL + s * LANES:OFF_B_VAL + (s + 1) * LANES]
            pos = row0 + r + lax.broadcasted_iota(jnp.int32, (RC, 1), 0)
            dl = _pool_means(lambda sh, s: b_buf[s, pl.ds(r + B_HALO - sh, RC), :], pos)
            for s in range(N_POOL_SLAB):
                d_s[rows, s * LANES:(s + 1) * LANES] = dl[s].astype(BF16)
            return c

        lax.fori_loop(0, TQ // RC, split_chunk, 0, unroll=True)
        msq_s[:, 0:C_ATT] = jnp.dot(sq_s[:, 0:C_ATT], mavg_ref[...], preferred_element_type=F32)
        msq_s[:, C_ATT:2 * C_ATT] = jnp.dot(sq_s[:, C_ATT:2 * C_ATT], mavg_ref[...], preferred_element_type=F32)
        yb_s[...] = jnp.dot(d_s[...], pw_ref[...], preferred_element_type=F32)

        def mixer_chunk(i, c):
            r = pl.multiple_of(i * RC, RC)
            rows = pl.ds(r, RC)
            grow = pl.ds(pl.multiple_of(row0 + r, RC), RC)
            conv = []
            for s in range(N_CONV_SLAB):
                cols = slice(s * LANES, (s + 1) * LANES)
                acc = jnp.zeros((RC, LANES), F32) + cb_ref[:, cols]
                for w in range(CONV_W):
                    acc = acc + u_buf[s, pl.ds(r + (U_HALO - CONV_HALO) + w, RC), :] * cw_ref[w:w + 1, cols]
                conv.append(acc)
            mu = jnp.sum(conv[0] + conv[1] + conv[2], axis=-1, keepdims=True) * (1.0 / C_CONV)
            cen = [cv - mu for cv in conv]
            var = jnp.sum(cen[0] * cen[0] + cen[1] * cen[1] + cen[2] * cen[2], axis=-1, keepdims=True) * (1.0 / C_CONV)
            rstd = lax.rsqrt(var + EPS)
            for s in range(N_CONV_SLAB):
                cols = slice(s * LANES, (s + 1) * LANES)
                ln = cen[s] * rstd * lng_ref[:, cols] + lnb_ref[:, cols]
                gate = proj[rows, OFF_A_GATE + s * LANES:OFF_A_GATE + (s + 1) * LANES]
                mix[grow, cols] = (_silu(ln) * _silu(gate)).astype(BF16)
            for s in range(N_POOL_SLAB):
                cols = slice(s * LANES, (s + 1) * LANES)
                gate = proj[rows, OFF_B_GATE + s * LANES:OFF_B_GATE + (s + 1) * LANES]
                yb = yb_s[rows, cols] * psc_ref[:, cols] * _silu(gate)
                mix[grow, C_CONV + s * LANES:C_CONV + (s + 1) * LANES] = yb.astype(BF16)
            cos = cos_ref[grow, :]
            sin = sin_ref[grow, :]
            for s in range(N_ATT_SLAB):
                cols = slice(s * LANES, (s + 1) * LANES)
                q = proj[rows, OFF_Q + s * LANES:OFF_Q + (s + 1) * LANES]
                qn = q * lax.rsqrt(msq_s[rows, cols] + EPS) * qg_ref[:, cols]
                qr = qn * cos + _swap_halves(qn) * sin
                ab[AB_Q][s, grow, :] = qr * (HEAD_DIM ** -0.5)
                k = proj[rows, OFF_K + s * LANES:OFF_K + (s + 1) * LANES]
                kn = k * lax.rsqrt(msq_s[rows, C_ATT + s * LANES:C_ATT + (s + 1) * LANES] + EPS) * kg_ref[:, cols]
                ab[AB_K][s, grow, :] = kn * cos + _swap_halves(kn) * sin
            return c

        lax.fori_loop(0, TQ // RC, mixer_chunk, 0, unroll=True)

        for s in range(N_ATT_SLAB):
            ko_ref[s * LANES:(s + 1) * LANES, :] = ab[AB_K][s, pl.ds(row0, TQ), :].T
            vo_ref[s * LANES:(s + 1) * LANES, :] = ab[AB_V][s, pl.ds(row0, TQ), :].T

        @pl.when(t == nt - 1)
        def _write_state():
            for s in range(N_CONV_SLAB):
                cst_ref[:, s * LANES:(s + 1) * LANES] = u_buf[s, TQ + U_HALO - CONV_HALO:TQ + U_HALO, :]
            for s in range(N_POOL_SLAB):
                pst_ref[:, s * LANES:(s + 1) * LANES] = b_buf[s, TQ + B_HALO - POOL_BUF:TQ + B_HALO, :]

        u_buf[:, 0:U_HALO, :] = u_buf[:, TQ:TQ + U_HALO, :]
        b_buf[:, 0:B_HALO, :] = b_buf[:, TQ:TQ + B_HALO, :]

    @pl.when(t == nt - 1)
    def _attention():
        d_mid, d_far = DILATIONS[1], DILATIONS[2]
        ratio = d_far // d_mid
        stream_len = seq_len // d_mid
        lane = lax.broadcasted_iota(jnp.int32, (QB, LANES), 1)
        rowi = lax.broadcasted_iota(jnp.int32, (QB, LANES), 0)
        lo = lane < HEAD_DIM
        cur_ok = lane <= rowi
        prev_ok = lane >= rowi
        cur_ok2 = jnp.concatenate([cur_ok, cur_ok], axis=0)
        prev_cur_ok2 = jnp.concatenate([jnp.concatenate([prev_ok, prev_ok], axis=0), cur_ok2], axis=1)

        def attend(q, keys, vals, mask):
            qa = jnp.where(lo, q, 0.0).astype(BF16)
            qb = jnp.where(lo, 0.0, q).astype(BF16)
            q2 = jnp.concatenate([qa, qb], axis=0)
            sc = lax.dot_general(q2, keys.astype(BF16), (((1,), (1,)), ((), ())), preferred_element_type=F32)
            sc = jnp.where(mask, sc, NEG)
            m = jnp.max(sc, axis=-1, keepdims=True)
            p = jnp.exp(sc - m).astype(BF16)
            v1 = jnp.concatenate([vals.astype(BF16), jnp.ones(vals.shape, BF16)], axis=1)
            ol = jnp.dot(p, v1, preferred_element_type=F32)
            o_u = jnp.where(lo, ol[0:QB, 0:LANES], ol[QB:2 * QB, 0:LANES])
            l_u = jnp.where(lo, ol[0:QB, LANES:2 * LANES], ol[QB:2 * QB, LANES:2 * LANES])
            m_u = jnp.where(lo, m[0:QB], m[QB:2 * QB])
            return o_u, m_u, l_u

        def block(bufs, rows, krows, mask, acc=None):
            b_q, b_k, b_v = bufs
            outs = []
            for s in range(N_ATT_SLAB):
                o_u, m_u, l_u = attend(b_q[s, rows, :], b_k[s, krows, :], b_v[s, krows, :], mask)
                if acc is not None:
                    a_o, a_m, a_l = acc
                    m_old = a_m[s, rows, :]
                    m_new = jnp.maximum(m_old, m_u)
                    w_old = jnp.exp(m_old - m_new)
                    w_u = jnp.exp(m_u - m_new)
                    o_u = a_o[s, rows, :] * w_old + o_u * w_u
                    l_u = a_l[s, rows, :] * w_old + l_u * w_u
                    m_u = m_new
                outs.append((o_u, m_u, l_u))
            return outs

        def store(acc, rows, outs):
            a_o, a_m, a_l = acc
            for s, (o_u, m_u, l_u) in enumerate(outs):
                a_o[s, rows, :] = o_u
                a_m[s, rows, :] = m_u
                a_l[s, rows, :] = l_u

        qkv = (ab[AB_Q], ab[AB_K], ab[AB_V])
        acc = (ab[AB_O], ab[AB_M], ab[AB_L])
        store(acc, pl.ds(0, QB), block(qkv, pl.ds(0, QB), pl.ds(0, QB), cur_ok2))

        def near_unit(u, c):
            start = pl.multiple_of(u * QB, QB)
            rows = pl.ds(start, QB)
            store(acc, rows, block(qkv, rows, pl.ds(start - QB, 2 * QB), prev_cur_ok2))
            return c

        lax.fori_loop(1, seq_len // QB, near_unit, 0, unroll=UNROLL_NEAR)

        order = (AB_SPARE, AB_Q, AB_K, AB_V, AB_O, AB_M, AB_L)
        for dst, src in zip(order[:-1], order[1:]):
            def regroup(i, c, dst=ab[dst], src=ab[src]):
                phase = i // (stream_len // QB)
                blk = i % (stream_len // QB)
                tiles = [src[s, pl.ds(phase + d_mid * QB * blk, QB, stride=d_mid), :] for s in range(N_ATT_SLAB)]
                for s in range(N_ATT_SLAB):
                    dst[s, pl.ds(pl.multiple_of(i * QB, QB), QB), :] = tiles[s]
                return c

            lax.fori_loop(0, seq_len // QB, regroup, 0, unroll=UNROLL_REGROUP)
        qkv = tuple(ab[i] for i in order[0:3])
        acc = tuple(ab[i] for i in order[3:6])
        att = ab[order[6]]

        def mid_first(ph, c):
            rows = pl.ds(pl.multiple_of(ph * stream_len, QB), QB)
            store(acc, rows, block(qkv, rows, rows, cur_ok2, acc))
            return c

        lax.fori_loop(0, d_mid, mid_first, 0, unroll=2)
        later_blocks = stream_len // QB - 1

        def mid_unit(i, c):
            start = pl.multiple_of((i // later_blocks) * stream_len + (i % later_blocks + 1) * QB, QB)
            rows = pl.ds(start, QB)
            store(acc, rows, block(qkv, rows, pl.ds(start - QB, 2 * QB), prev_cur_ok2, acc))
            return c

        lax.fori_loop(0, d_mid * later_blocks, mid_unit, 0, unroll=UNROLL_MID)

        def far_unit(r, c):
            rows = pl.ds((r % d_mid) * stream_len + r // d_mid, QB, stride=ratio)
            outs = block(qkv, rows, rows, cur_ok2, acc)
            for s, (o_n, _, l_n) in enumerate(outs):
                att[s, pl.ds(r, QB, stride=d_far), :] = o_n / l_n
            return c

        lax.fori_loop(0, d_far, far_unit, 0, unroll=UNROLL_FAR)

    @pl.when(t >= nt)
    def _phase2():
        row0 = pl.multiple_of((t - nt) * TQ, TQ)

        def att_chunk(i, c):
            r = pl.multiple_of(i * RC, RC)
            grow = pl.ds(pl.multiple_of(row0 + r, RC), RC)
            for s in range(N_ATT_SLAB):
                cols = slice(s * LANES, (s + 1) * LANES)
                yc = ab[AB_ATT][s, grow, :] * cg[grow, cols].astype(F32)
                mix[grow, C_CONV + C_POOL + s * LANES:C_CONV + C_POOL + (s + 1) * LANES] = yc.astype(BF16)
            return c

        lax.fori_loop(0, TQ // RC, att_chunk, 0, unroll=True)
        y_ref[...] = x_ref[...] + jnp.dot(mix[pl.ds(row0, TQ), :], wout_ref[...], preferred_element_type=F32)


def _prompt_kernel_with_alias(*refs, n_in, seq_len):
    _prompt_kernel(*refs[:n_in], *refs[n_in + 2:], seq_len=seq_len)


def _prompt_layer(layer, x, norm_g, w_in, conv_w, conv_b, ln_g, ln_b, pool_wbd, pool_scale, qg, kg, cos_t, sin_t,
                  mavg, w_out, kv_prev=None):
    bsz, seq_len, _ = x.shape
    depth = w_in.shape[0]
    assert seq_len % TQ == 0 and seq_len == QB * DILATIONS[-1] and DILATIONS[0] == 1
    nt = seq_len // TQ

    def const(shape):
        nd = len(shape)
        return pl.BlockSpec(shape, lambda b, t: (0,) * nd)

    def resident(shape):
        nd = len(shape)
        return pl.BlockSpec(shape, lambda b, t: (0,) * nd, pipeline_mode=pl.Buffered(1))

    def per_layer(shape, **kw):
        nd = len(shape)
        return pl.BlockSpec((None,) + shape, lambda b, t: (layer,) + (0,) * nd, **kw)

    in_specs = [
        pl.BlockSpec((None, TQ, D_MODEL), lambda b, t: (b, t % nt, 0)),
        per_layer((1, D_MODEL)),
        per_layer((D_MODEL, D_IN), pipeline_mode=pl.Buffered(1)),
        per_layer((CONV_W, C_CONV)),
        per_layer((1, C_CONV)),
        per_layer((1, C_CONV)),
        per_layer((1, C_CONV)),
        per_layer((C_POOL, C_POOL)),
        per_layer((1, C_POOL)),
        per_layer((1, C_ATT)),
        per_layer((1, C_ATT)),
        resident((seq_len, LANES)),
        resident((seq_len, LANES)),
        const((C_ATT, C_ATT)),
        per_layer((D_MODEL, D_MODEL), pipeline_mode=pl.Buffered(1)),
    ]
    operands = [x, norm_g, w_in, conv_w, conv_b, ln_g, ln_b, pool_wbd, pool_scale, qg, kg, cos_t, sin_t, mavg, w_out]
    n_in = len(operands)
    kv_spec = pl.BlockSpec((None, None, C_ATT, TQ), lambda b, t: (layer, b, 0, jnp.minimum(t, nt - 1)))
    out_specs = [
        pl.BlockSpec((None, TQ, D_MODEL), lambda b, t: (b, jnp.maximum(t - nt, 0), 0)),
        kv_spec,
        kv_spec,
        pl.BlockSpec((None, CONV_HALO, C_CONV), lambda b, t: (b, 0, 0)),
        pl.BlockSpec((None, POOL_BUF, C_POOL), lambda b, t: (b, 0, 0)),
    ]
    out_shape = [
        jax.ShapeDtypeStruct((bsz, seq_len, D_MODEL), F32),
        jax.ShapeDtypeStruct((depth, bsz, C_ATT, seq_len), F32),
        jax.ShapeDtypeStruct((depth, bsz, C_ATT, seq_len), F32),
        jax.ShapeDtypeStruct((bsz, CONV_HALO, C_CONV), F32),
        jax.ShapeDtypeStruct((bsz, POOL_BUF, C_POOL), F32),
    ]
    if kv_prev is None:
        kern = functools.partial(_prompt_kernel, seq_len=seq_len)
        aliases = {}
    else:
        kern = functools.partial(_prompt_kernel_with_alias, n_in=n_in, seq_len=seq_len)
        in_specs += [pl.BlockSpec(memory_space=pl.ANY)] * 2
        operands += list(kv_prev)
        aliases = {n_in: 1, n_in + 1: 2}
    scratch = [
        pltpu.VMEM((TQ, D_MODEL), BF16),
        pltpu.VMEM((TQ, D_IN), F32),
        pltpu.VMEM((N_CONV_SLAB, TQ + U_HALO, LANES), F32),
        pltpu.VMEM((N_POOL_SLAB, TQ + B_HALO, LANES), F32),
        pltpu.VMEM((TQ, 2 * C_ATT), BF16),
        pltpu.VMEM((TQ, 2 * C_ATT), F32),
        pltpu.VMEM((TQ, C_POOL), BF16),
        pltpu.VMEM((TQ, C_POOL), F32),
        *[pltpu.VMEM((N_ATT_SLAB, seq_len, LANES), F32) for _ in range(AB_COUNT)],
        pltpu.VMEM((seq_len, D_MODEL), BF16),
        pltpu.VMEM((seq_len, C_ATT), BF16),
    ]
    return pl.pallas_call(
        kern,
        out_shape=out_shape,
        grid=(bsz, 2 * nt),
        in_specs=in_specs,
        out_specs=out_specs,
        scratch_shapes=scratch,
        input_output_aliases=aliases,
        compiler_params=pltpu.CompilerParams(
            dimension_semantics=("arbitrary", "arbitrary"),
            vmem_limit_bytes=VMEM_LIMIT_BYTES,
        ),
        name="prompt_layer",
    )(*operands)


def _row_to_col_tile(row):
    return jnp.broadcast_to(row, (LANES, row.shape[1])).T


def _key_multiplicity(win_len):
    t = lax.broadcasted_iota(jnp.int32, (1, win_len), 1)
    delta = win_len - t
    cnt = jnp.zeros((1, win_len), F32)
    for d in DILATIONS:
        hit = jnp.logical_and(delta % d == 0, delta <= d * WINDOW_KEYS)
        cnt = cnt + hit.astype(F32)
    return cnt


def _sample_kernel(x_ref, sc_ref, sp_ref, kt_ref, vt_ref,
                   ng_ref, win_ref, cw_ref, cb_ref, lng_ref, lnb_ref, pw_ref, psc_ref, qg_ref, kg_ref,
                   cos_ref, sin_ref, mavg_ref, wout_ref,
                   y_ref, nc_ref, np_ref, okt_ref, ovt_ref,
                   xcur_s, qr_s, kr_s, v_s, att_s, mixab_s, cg_s, *, pos):
    nsmp = x_ref.shape[0]
    win_len = kt_ref.shape[1]
    layer = pl.program_id(0)
    b = pl.program_id(1)

    @pl.when(jnp.logical_and(layer == 0, b == 0))
    def _load_input():
        xcur_s[...] = x_ref[...]

    @pl.when(b == 0)
    def _prepare():
        x = xcur_s[...]
        ms = jnp.mean(x * x, axis=-1, keepdims=True)
        h = (x * lax.rsqrt(ms + EPS) * ng_ref[...]).astype(BF16)
        proj = jnp.dot(h, win_ref[...], preferred_element_type=F32)

        u = proj[:, OFF_A_VAL:OFF_A_VAL + C_CONV] * jax.nn.sigmoid(proj[:, OFF_A_GLU:OFF_A_GLU + C_CONV])
        conv = u * cw_ref[CONV_HALO:CONV_W, :] + cb_ref[...]
        for w in range(CONV_HALO):
            conv = conv + sc_ref[:, w, :] * cw_ref[w:w + 1, :]
        nc_ref[:, 0:CONV_HALO - 1, :] = sc_ref[:, 1:CONV_HALO, :]
        nc_ref[:, CONV_HALO - 1, :] = u
        mu = jnp.mean(conv, axis=-1, keepdims=True)
        cen = conv - mu
        var = jnp.mean(cen * cen, axis=-1, keepdims=True)
        ln = cen * lax.rsqrt(var + EPS) * lng_ref[...] + lnb_ref[...]
        ya = _silu(ln) * _silu(proj[:, OFF_A_GATE:OFF_A_GATE + C_CONV])

        bval = proj[:, OFF_B_VAL:OFF_B_VAL + C_POOL]
        lane_p = lax.broadcasted_iota(jnp.int32, (nsmp, C_POOL), 1)
        pooled = jnp.zeros((nsmp, C_POOL), F32)
        acc = bval
        done = 1
        for wi, w in enumerate(POOL_WINDOWS):
            for i in range(done, w):
                acc = acc + sp_ref[:, POOL_BUF - i, :]
            done = w
            pooled = jnp.where(lane_p // POOL_GC == wi, acc / float(min(pos + 1, w)), pooled)
        np_ref[:, 0:POOL_BUF - 1, :] = sp_ref[:, 1:POOL_BUF, :]
        np_ref[:, POOL_BUF - 1, :] = bval
        dpool = (pooled - bval).astype(BF16)
        yb = (jnp.dot(dpool, pw_ref[...], preferred_element_type=F32) * psc_ref[...]
              * _silu(proj[:, OFF_B_GATE:OFF_B_GATE + C_POOL]))
        mixab_s[:, 0:C_CONV] = ya
        mixab_s[:, C_CONV:C_CONV + C_POOL] = yb
        cg_s[...] = _silu(proj[:, OFF_C_GATE:OFF_C_GATE + C_ATT])

        q = proj[:, OFF_Q:OFF_Q + C_ATT]
        k = proj[:, OFF_K:OFF_K + C_ATT]
        qn = q * lax.rsqrt(jnp.dot((q * q).astype(BF16), mavg_ref[...], preferred_element_type=F32) + EPS) * qg_ref[...]
        kn = k * lax.rsqrt(jnp.dot((k * k).astype(BF16), mavg_ref[...], preferred_element_type=F32) + EPS) * kg_ref[...]
        for s in range(N_ATT_SLAB):
            cols = slice(s * LANES, (s + 1) * LANES)
            qs, ks = qn[:, cols], kn[:, cols]
            qr_s[:, cols] = (qs * cos_ref[...] + _swap_halves(qs) * sin_ref[...]) * (HEAD_DIM ** -0.5)
            kr_s[:, cols] = ks * cos_ref[...] + _swap_halves(ks) * sin_ref[...]
        v_s[...] = proj[:, OFF_V:OFF_V + C_ATT]
        att_s[...] = jnp.zeros((nsmp, C_ATT), F32)

    rowid = lax.broadcasted_iota(jnp.int32, (nsmp, C_ATT), 0)
    mine = rowid == b

    def pick_row(ref):
        return jnp.sum(jnp.where(mine, ref[...], 0.0), axis=0, keepdims=True)

    q_col = _row_to_col_tile(pick_row(qr_s))
    k_col = _row_to_col_tile(pick_row(kr_s))
    v_col = _row_to_col_tile(pick_row(v_s))
    cnt = _key_multiplicity(win_len)
    reach = cnt > 0.0
    lane_w = lax.broadcasted_iota(jnp.int32, (HEAD_DIM, win_len), 1)
    last_lane = lane_w == win_len - 1
    n_tiles = win_len // LANES
    n_pat = float(len(DILATIONS))

    att_cols = []
    for hd in range(N_HEADS):
        hr = slice(hd * HEAD_DIM, (hd + 1) * HEAD_DIM)
        kt = kt_ref[hr, :]
        vt = vt_ref[hr, :]
        qh = q_col[hr, :]
        kh = k_col[hr, :]
        vh = v_col[hr, :]
        s_win = jnp.sum(kt * jnp.concatenate([qh] * n_tiles, axis=1), axis=0, keepdims=True)
        s_win = jnp.where(reach, s_win, NEG)
        s_new = jnp.sum(kh * qh, axis=0, keepdims=True)[:, 0:1]
        m = jnp.maximum(jnp.max(s_win, axis=-1, keepdims=True), s_new)
        p_win = cnt * jnp.exp(s_win - m)
        p_new = n_pat * jnp.exp(s_new - m)
        l = jnp.sum(p_win, axis=-1, keepdims=True) + p_new
        o = jnp.sum(vt * p_win, axis=-1, keepdims=True) + vh[:, 0:1] * p_new
        att_cols.append(o / l)
        okt_ref[hr, :] = jnp.where(last_lane, jnp.concatenate([kh] * n_tiles, axis=1), pltpu.roll(kt, win_len - 1, 1))
        ovt_ref[hr, :] = jnp.where(last_lane, jnp.concatenate([vh] * n_tiles, axis=1), pltpu.roll(vt, win_len - 1, 1))
    att_col = jnp.concatenate(att_cols, axis=0)
    att_row = jnp.broadcast_to(att_col, (C_ATT, LANES)).T[0:1, :]
    att_s[...] = jnp.where(mine, att_row, att_s[...])

    @pl.when(b == nsmp - 1)
    def _finish():
        yc = att_s[...] * cg_s[...]
        mixed = jnp.concatenate([mixab_s[...], yc], axis=-1).astype(BF16)
        y = xcur_s[...] + jnp.dot(mixed, wout_ref[...], preferred_element_type=F32)
        xcur_s[...] = y
        y_ref[...] = y


def _sample_path(pos, x, state_conv, state_pool, kt, vt, norm_g, w_in, conv_w, conv_b, ln_g, ln_b,
                 pool_wbd, pool_scale, qg, kg, cos_t, sin_t, mavg, w_out):
    nsmp = x.shape[0]
    depth, _, _, win_len = kt.shape
    assert win_len == WINDOW_KEYS * DILATIONS[-1]

    def const(shape):
        nd = len(shape)
        return pl.BlockSpec(shape, lambda l, i: (0,) * nd)

    def per_layer(shape):
        nd = len(shape)
        return pl.BlockSpec((None,) + shape, lambda l, i: (l,) + (0,) * nd)

    cache_spec = pl.BlockSpec((None, None, C_ATT, win_len), lambda l, i: (l, i, 0, 0))
    in_specs = [
        const((nsmp, D_MODEL)),
        per_layer((nsmp, CONV_HALO, C_CONV)),
        per_layer((nsmp, POOL_BUF, C_POOL)),
        cache_spec, cache_spec,
        per_layer((1, D_MODEL)), per_layer((D_MODEL, D_IN)), per_layer((CONV_W, C_CONV)), per_layer((1, C_CONV)),
        per_layer((1, C_CONV)), per_layer((1, C_CONV)), per_layer((C_POOL, C_POOL)), per_layer((1, C_POOL)),
        per_layer((1, C_ATT)), per_layer((1, C_ATT)), const((1, LANES)), const((1, LANES)),
        const((C_ATT, C_ATT)), per_layer((D_MODEL, D_MODEL)),
    ]
    out_specs = [
        const((nsmp, D_MODEL)),
        per_layer((nsmp, CONV_HALO, C_CONV)),
        per_layer((nsmp, POOL_BUF, C_POOL)),
        cache_spec, cache_spec,
    ]
    out_shape = [
        jax.ShapeDtypeStruct((nsmp, D_MODEL), F32),
        jax.ShapeDtypeStruct((depth, nsmp, CONV_HALO, C_CONV), F32),
        jax.ShapeDtypeStruct((depth, nsmp, POOL_BUF, C_POOL), F32),
        jax.ShapeDtypeStruct((depth, nsmp, C_ATT, win_len), F32),
        jax.ShapeDtypeStruct((depth, nsmp, C_ATT, win_len), F32),
    ]
    scratch = [
        pltpu.VMEM((nsmp, D_MODEL), F32),
        pltpu.VMEM((nsmp, C_ATT), F32),
        pltpu.VMEM((nsmp, C_ATT), F32),
        pltpu.VMEM((nsmp, C_ATT), F32),
        pltpu.VMEM((nsmp, C_ATT), F32),
        pltpu.VMEM((nsmp, C_CONV + C_POOL), F32),
        pltpu.VMEM((nsmp, C_ATT), F32),
    ]
    return pl.pallas_call(
        functools.partial(_sample_kernel, pos=pos),
        out_shape=out_shape,
        grid=(depth, nsmp),
        in_specs=in_specs,
        out_specs=out_specs,
        scratch_shapes=scratch,
        compiler_params=pltpu.CompilerParams(
            dimension_semantics=("arbitrary", "arbitrary"),
            vmem_limit_bytes=VMEM_LIMIT_BYTES,
        ),
        name="sample_path",
    )(x, state_conv, state_pool, kt, vt, norm_g, w_in, conv_w, conv_b, ln_g, ln_b, pool_wbd, pool_scale,
      qg, kg, cos_t, sin_t, mavg, w_out)


def _to_channel_major(a):
    depth, bsz, ntok, nh, hd = a.shape
    return jnp.transpose(a, (0, 1, 3, 4, 2)).reshape(depth, bsz, nh * hd, ntok)


def _from_channel_major(a):
    depth, bsz, _, ntok = a.shape
    return jnp.transpose(a.reshape(depth, bsz, N_HEADS, HEAD_DIM, ntok), (0, 1, 4, 2, 3))


def kernel(x_prompt, x_sample, state_conv, state_pool, cache_k_win, cache_v_win, norm_g, w_in, conv_w, conv_b,
           ln_g, ln_b, pool_w, pool_scale, q_norm_g, k_norm_g, w_out):
    depth = w_in.shape[0]
    seq_len = x_prompt.shape[1]
    nsmp = x_sample.shape[0]

    cos_p, sin_p = _rope_tables(jnp.arange(seq_len, dtype=jnp.int32))
    cos_s, sin_s = _rope_tables(jnp.full((1,), PAST_LEN, dtype=jnp.int32))
    mavg = _head_mean_matrix()
    weights = (norm_g[:, None], w_in.astype(BF16), conv_w, conv_b[:, None], ln_g[:, None], ln_b[:, None],
               _pool_block_diag(pool_w).astype(BF16), pool_scale[:, None],
               jnp.tile(q_norm_g, (1, N_HEADS))[:, None], jnp.tile(k_norm_g, (1, N_HEADS))[:, None])
    w_out_b = w_out.astype(BF16)

    xp = x_prompt
    kv = None
    conv_states, pool_states = [], []
    for layer in range(depth):
        xp, kp, vp, cst, pst = _prompt_layer(layer, xp, *weights, cos_p, sin_p, mavg, w_out_b, kv_prev=kv)
        kv = (kp, vp)
        conv_states.append(cst)
        pool_states.append(pst)

    ys, new_conv_s, new_pool_s, new_kt_s, new_vt_s = _sample_path(
        PAST_LEN, x_sample.reshape(nsmp, D_MODEL), state_conv, state_pool,
        _to_channel_major(cache_k_win), _to_channel_major(cache_v_win),
        *weights, cos_s, sin_s, mavg, w_out_b)

    return (xp, ys.reshape(nsmp, 1, D_MODEL), jnp.stack(conv_states), jnp.stack(pool_states),
            _from_channel_major(kv[0]), _from_channel_major(kv[1]),
            new_conv_s, new_pool_s, _from_channel_major(new_kt_s), _from_channel_major(new_vt_s))
```

```python
import functools

import jax
import jax.numpy as jnp
import numpy as np
from jax import lax
from jax.experimental import pallas as pl
from jax.experimental.pallas import tpu as pltpu

F32 = jnp.float32
BF16 = jnp.bfloat16

D_MODEL = 1024
C_CONV = 384
C_POOL = 256
C_ATT = 384
HEAD_DIM = 64
N_HEADS = C_ATT // HEAD_DIM
CONV_W = 31
CONV_HALO = CONV_W - 1
POOL_WINDOWS = (2, 4, 8, 16)
POOL_GC = 64
POOL_BUF = 15
DILATIONS = (1, 4, 16)
WINDOW_KEYS = 128
EPS = 1e-6
ROPE_THETA = 10000.0
D_IN = 3 * C_CONV + 2 * C_POOL + 4 * C_ATT
PAST_LEN = 16384
NEG = -1e30

OFF_A_VAL = 0
OFF_A_GLU = OFF_A_VAL + C_CONV
OFF_A_GATE = OFF_A_GLU + C_CONV
OFF_B_VAL = OFF_A_GATE + C_CONV
OFF_B_GATE = OFF_B_VAL + C_POOL
OFF_Q = OFF_B_GATE + C_POOL
OFF_K = OFF_Q + C_ATT
OFF_V = OFF_K + C_ATT
OFF_C_GATE = OFF_V + C_ATT

LANES = 128
N_CONV_SLAB = C_CONV // LANES
N_POOL_SLAB = C_POOL // LANES
N_ATT_SLAB = C_ATT // LANES
VMEM_LIMIT_BYTES = 60 * 1024 * 1024

TQ = 256
RC = 32
U_HALO = 32
B_HALO = 16
QB = WINDOW_KEYS
UNROLL_NEAR = 15
UNROLL_MID = 6
UNROLL_FAR = 8
UNROLL_REGROUP = 4

AB_Q, AB_K, AB_V, AB_O, AB_M, AB_L, AB_SPARE = range(7)
AB_COUNT = 7
AB_ATT = AB_L


def _silu(x):
    return x * jax.nn.sigmoid(x)


def _rope_tables(positions):
    half = HEAD_DIM // 2
    inv = ROPE_THETA ** (-jnp.arange(half, dtype=F32) / half)
    ang = positions.astype(F32)[:, None] * inv[None, :]
    cos = jnp.cos(ang)
    sin = jnp.sin(ang)
    cos_h = jnp.concatenate([cos, cos], axis=-1)
    sin_h = jnp.concatenate([-sin, sin], axis=-1)
    reps = LANES // HEAD_DIM
    return jnp.tile(cos_h, (1, reps)), jnp.tile(sin_h, (1, reps))


def _head_mean_matrix():
    idx = np.arange(C_ATT) // HEAD_DIM
    return jnp.asarray((idx[:, None] == idx[None, :]).astype(np.float32) / HEAD_DIM, dtype=BF16)


def _pool_block_diag(pool_w):
    out = jnp.zeros((pool_w.shape[0], C_POOL, C_POOL), pool_w.dtype)
    for g in range(len(POOL_WINDOWS)):
        out = out.at[:, g * POOL_GC:(g + 1) * POOL_GC, g * POOL_GC:(g + 1) * POOL_GC].set(pool_w[:, g])
    return out


def _swap_halves(x):
    lane = lax.broadcasted_iota(jnp.int32, x.shape, 1)
    first_half = (lane % HEAD_DIM) < (HEAD_DIM // 2)
    return jnp.where(first_half, pltpu.roll(x, LANES - HEAD_DIM // 2, 1), pltpu.roll(x, HEAD_DIM // 2, 1))


def _pool_means(loads, pos):
    lane = lax.broadcasted_iota(jnp.int32, loads(0, 0).shape, 1)
    lo = lane < POOL_GC
    posf = (pos + 1).astype(F32)
    outs = []
    for slab in range(N_POOL_SLAB):
        w_lo, w_hi = POOL_WINDOWS[2 * slab], POOL_WINDOWS[2 * slab + 1]
        cur = loads(0, slab)
        s = cur
        for i in range(1, w_lo):
            s = s + loads(i, slab)
        s_lo = s
        for i in range(w_lo, w_hi):
            s = s + loads(i, slab)
        s_hi = s
        cnt_lo = jnp.minimum(posf, float(w_lo))
        cnt_hi = jnp.minimum(posf, float(w_hi))
        pooled = jnp.where(lo, s_lo / cnt_lo, s_hi / cnt_hi)
        outs.append(pooled - cur)
    return outs


N_LAYER_INPUTS = 22


def _layer_kernel(x_ref, ng_ref, win_ref, cw_ref, cb_ref, lng_ref, lnb_ref, pw_ref, psc_ref,
                  qg_ref, kg_ref, cos_ref, sin_ref, mavg_ref, wout_ref,
                  xs_ref, sc_ref, sp_ref, kt_ref, vt_ref, coss_ref, sins_ref,
                  y_ref, ko_ref, vo_ref, cst_ref, pst_ref,
                  ys_ref, ncs_ref, nps_ref, okt_ref, ovt_ref,
                  h_s, proj, u_buf, b_buf, sq_s, msq_s, d_s, yb_s,
                  ab0, ab1, ab2, ab3, ab4, ab5, ab6, mix,
                  qr_s, kr_s, v_s, att_s, mixab_s, cgs_s, sm_s, sl_s, so_s, kcar_s, vcar_s,
                  *, seq_len, win_len, pos):
    ab = (ab0, ab1, ab2, ab3, ab4, ab5, ab6)
    nt = seq_len // TQ
    b = pl.program_id(0)
    t = pl.program_id(1)
    n_batch = pl.num_programs(0)
    blocks_per_sample = win_len // kt_ref.shape[1]

    def sample_window():
        step = b * (2 * nt) + t
        part = step % blocks_per_sample
        _sample_window(part == 0, part == blocks_per_sample - 1, step // blocks_per_sample,
                       (blocks_per_sample - 1 - part) * kt_ref.shape[1],
                       kt_ref, vt_ref, okt_ref, ovt_ref, qr_s, kr_s, v_s, att_s, sm_s, sl_s, so_s, kcar_s, vcar_s,
                       win_len=win_len)

    @pl.when(t < nt)
    def _phase1():
        row0 = pl.multiple_of(t * TQ, TQ)

        _sample_prepare(jnp.logical_and(b == 0, t == 0), xs_ref, sc_ref, sp_ref,
                        ng_ref, win_ref, cw_ref, cb_ref, lng_ref, lnb_ref, pw_ref, psc_ref, qg_ref, kg_ref,
                        coss_ref, sins_ref, mavg_ref, ncs_ref, nps_ref,
                        qr_s, kr_s, v_s, att_s, mixab_s, cgs_s, pos=pos)

        @pl.when(t == 0)
        def _zero_halo():
            u_buf[:, 0:U_HALO, :] = jnp.zeros((N_CONV_SLAB, U_HALO, LANES), F32)
            b_buf[:, 0:B_HALO, :] = jnp.zeros((N_POOL_SLAB, B_HALO, LANES), F32)

        sample_window()

        def norm_chunk(i, c):
            r = pl.multiple_of(i * RC, RC)
            x = x_ref[pl.ds(r, RC), :]
            ms = jnp.mean(x * x, axis=-1, keepdims=True)
            h_s[pl.ds(r, RC), :] = (x * lax.rsqrt(ms + EPS) * ng_ref[...]).astype(BF16)
            return c

        lax.fori_loop(0, TQ // RC, norm_chunk, 0, unroll=True)
        proj[...] = jnp.dot(h_s[...], win_ref[...], preferred_element_type=F32)

        def split_chunk(i, c):
            r = pl.multiple_of(i * RC, RC)
            rows = pl.ds(r, RC)
            grow = pl.ds(pl.multiple_of(row0 + r, RC), RC)
            for s in range(N_CONV_SLAB):
                cols = slice(s * LANES, (s + 1) * LANES)
                a_val = proj[rows, OFF_A_VAL + s * LANES:OFF_A_VAL + (s + 1) * LANES]
                a_glu = proj[rows, OFF_A_GLU + s * LANES:OFF_A_GLU + (s + 1) * LANES]
                u_buf[s, pl.ds(U_HALO + r, RC), :] = a_val * jax.nn.sigmoid(a_glu)
                q = proj[rows, OFF_Q + s * LANES:OFF_Q + (s + 1) * LANES]
                k = proj[rows, OFF_K + s * LANES:OFF_K + (s + 1) * LANES]
                sq_s[rows, cols] = (q * q).astype(BF16)
                sq_s[rows, C_ATT + s * LANES:C_ATT + (s + 1) * LANES] = (k * k).astype(BF16)
                ab[AB_V][s, grow, :] = proj[rows, OFF_V + s * LANES:OFF_V + (s + 1) * LANES]
                mix[grow, C_CONV + C_POOL + s * LANES:C_CONV + C_POOL + (s + 1) * LANES] = _silu(
                    proj[rows, OFF_C_GATE + s * LANES:OFF_C_GATE + (s + 1) * LANES]).astype(BF16)
            for s in range(N_POOL_SLAB):
                b_buf[s, pl.ds(B_HALO + r, RC), :] = proj[rows, OFF_B_VAL + s * LANES:OFF_B_VAL + (s + 1) * LANES]
            pos = row0 + r + lax.broadcasted_iota(jnp.int32, (RC, 1), 0)
            dl = _pool_means(lambda sh, s: b_buf[s, pl.ds(r + B_HALO - sh, RC), :], pos)
            for s in range(N_POOL_SLAB):
                d_s[rows, s * LANES:(s + 1) * LANES] = dl[s].astype(BF16)
            return c

        lax.fori_loop(0, TQ // RC, split_chunk, 0, unroll=True)
        msq_s[:, 0:C_ATT] = jnp.dot(sq_s[:, 0:C_ATT], mavg_ref[...], preferred_element_type=F32)
        msq_s[:, C_ATT:2 * C_ATT] = jnp.dot(sq_s[:, C_ATT:2 * C_ATT], mavg_ref[...], preferred_element_type=F32)
        yb_s[...] = jnp.dot(d_s[...], pw_ref[...], preferred_element_type=F32)

        def mixer_chunk(i, c):
            r = pl.multiple_of(i * RC, RC)
            rows = pl.ds(r, RC)
            grow = pl.ds(pl.multiple_of(row0 + r, RC), RC)
            conv = []
            for s in range(N_CONV_SLAB):
                cols = slice(s * LANES, (s + 1) * LANES)
                acc = jnp.zeros((RC, LANES), F32) + cb_ref[:, cols]
                for w in range(CONV_W):
                    acc = acc + u_buf[s, pl.ds(r + (U_HALO - CONV_HALO) + w, RC), :] * cw_ref[w:w + 1, cols]
                conv.append(acc)
            mu = jnp.sum(conv[0] + conv[1] + conv[2], axis=-1, keepdims=True) * (1.0 / C_CONV)
            cen = [cv - mu for cv in conv]
            var = jnp.sum(cen[0] * cen[0] + cen[1] * cen[1] + cen[2] * cen[2], axis=-1, keepdims=True) * (1.0 / C_CONV)
            rstd = lax.rsqrt(var + EPS)
            for s in range(N_CONV_SLAB):
                cols = slice(s * LANES, (s + 1) * LANES)
                ln = cen[s] * rstd * lng_ref[:, cols] + lnb_ref[:, cols]
                gate = proj[rows, OFF_A_GATE + s * LANES:OFF_A_GATE + (s + 1) * LANES]
                mix[grow, cols] = (_silu(ln) * _silu(gate)).astype(BF16)
            for s in range(N_POOL_SLAB):
                cols = slice(s * LANES, (s + 1) * LANES)
                gate = proj[rows, OFF_B_GATE + s * LANES:OFF_B_GATE + (s + 1) * LANES]
                yb = yb_s[rows, cols] * psc_ref[:, cols] * _silu(gate)
                mix[grow, C_CONV + s * LANES:C_CONV + (s + 1) * LANES] = yb.astype(BF16)
            cos = cos_ref[rows, :]
            sin = sin_ref[rows, :]
            for s in range(N_ATT_SLAB):
                cols = slice(s * LANES, (s + 1) * LANES)
                q = proj[rows, OFF_Q + s * LANES:OFF_Q + (s + 1) * LANES]
                qn = q * lax.rsqrt(msq_s[rows, cols] + EPS) * qg_ref[:, cols]
                qr = qn * cos + _swap_halves(qn) * sin
                ab[AB_Q][s, grow, :] = qr * (HEAD_DIM ** -0.5)
                k = proj[rows, OFF_K + s * LANES:OFF_K + (s + 1) * LANES]
                kn = k * lax.rsqrt(msq_s[rows, C_ATT + s * LANES:C_ATT + (s + 1) * LANES] + EPS) * kg_ref[:, cols]
                ab[AB_K][s, grow, :] = kn * cos + _swap_halves(kn) * sin
            return c

        lax.fori_loop(0, TQ // RC, mixer_chunk, 0, unroll=True)

        for s in range(N_ATT_SLAB):
            ko_ref[s * LANES:(s + 1) * LANES, :] = ab[AB_K][s, pl.ds(row0, TQ), :].T
            vo_ref[s * LANES:(s + 1) * LANES, :] = ab[AB_V][s, pl.ds(row0, TQ), :].T

        @pl.when(t == nt - 1)
        def _write_state():
            for s in range(N_CONV_SLAB):
                cst_ref[:, s * LANES:(s + 1) * LANES] = u_buf[s, TQ + U_HALO - CONV_HALO:TQ + U_HALO, :]
            for s in range(N_POOL_SLAB):
                pst_ref[:, s * LANES:(s + 1) * LANES] = b_buf[s, TQ + B_HALO - POOL_BUF:TQ + B_HALO, :]

        u_buf[:, 0:U_HALO, :] = u_buf[:, TQ:TQ + U_HALO, :]
        b_buf[:, 0:B_HALO, :] = b_buf[:, TQ:TQ + B_HALO, :]

    @pl.when(t == nt - 1)
    def _attention():
        d_mid, d_far = DILATIONS[1], DILATIONS[2]
        ratio = d_far // d_mid
        stream_len = seq_len // d_mid
        lane = lax.broadcasted_iota(jnp.int32, (QB, LANES), 1)
        rowi = lax.broadcasted_iota(jnp.int32, (QB, LANES), 0)
        lo = lane < HEAD_DIM
        cur_ok = lane <= rowi
        prev_ok = lane >= rowi
        cur_ok2 = jnp.concatenate([cur_ok, cur_ok], axis=0)
        prev_cur_ok2 = jnp.concatenate([jnp.concatenate([prev_ok, prev_ok], axis=0), cur_ok2], axis=1)

        def attend(q, keys, vals, mask):
            qa = jnp.where(lo, q, 0.0).astype(BF16)
            qb = jnp.where(lo, 0.0, q).astype(BF16)
            q2 = jnp.concatenate([qa, qb], axis=0)
            sc = lax.dot_general(q2, keys.astype(BF16), (((1,), (1,)), ((), ())), preferred_element_type=F32)
            sc = jnp.where(mask, sc, NEG)
            m = jnp.max(sc, axis=-1, keepdims=True)
            p = jnp.exp(sc - m).astype(BF16)
            v1 = jnp.concatenate([vals.astype(BF16), jnp.ones(vals.shape, BF16)], axis=1)
            ol = jnp.dot(p, v1, preferred_element_type=F32)
            o_u = jnp.where(lo, ol[0:QB, 0:LANES], ol[QB:2 * QB, 0:LANES])
            l_u = jnp.where(lo, ol[0:QB, LANES:2 * LANES], ol[QB:2 * QB, LANES:2 * LANES])
            m_u = jnp.where(lo, m[0:QB], m[QB:2 * QB])
            return o_u, m_u, l_u

        def block(bufs, rows, krows, mask, acc=None):
            b_q, b_k, b_v = bufs
            outs = []
            for s in range(N_ATT_SLAB):
                o_u, m_u, l_u = attend(b_q[s, rows, :], b_k[s, krows, :], b_v[s, krows, :], mask)
                if acc is not None:
                    a_o, a_m, a_l = acc
                    m_old = a_m[s, rows, :]
                    m_new = jnp.maximum(m_old, m_u)
                    w_old = jnp.exp(m_old - m_new)
                    w_u = jnp.exp(m_u - m_new)
                    o_u = a_o[s, rows, :] * w_old + o_u * w_u
                    l_u = a_l[s, rows, :] * w_old + l_u * w_u
                    m_u = m_new
                outs.append((o_u, m_u, l_u))
            return outs

        def store(acc, rows, outs):
            a_o, a_m, a_l = acc
            for s, (o_u, m_u, l_u) in enumerate(outs):
                a_o[s, rows, :] = o_u
                a_m[s, rows, :] = m_u
                a_l[s, rows, :] = l_u

        qkv = (ab[AB_Q], ab[AB_K], ab[AB_V])
        acc = (ab[AB_O], ab[AB_M], ab[AB_L])
        store(acc, pl.ds(0, QB), block(qkv, pl.ds(0, QB), pl.ds(0, QB), cur_ok2))

        def near_unit(u, c):
            start = pl.multiple_of(u * QB, QB)
            rows = pl.ds(start, QB)
            store(acc, rows, block(qkv, rows, pl.ds(start - QB, 2 * QB), prev_cur_ok2))
            return c

        lax.fori_loop(1, seq_len // QB, near_unit, 0, unroll=UNROLL_NEAR)

        order = (AB_SPARE, AB_Q, AB_K, AB_V, AB_O, AB_M, AB_L)
        for dst, src in zip(order[:-1], order[1:]):
            def regroup(i, c, dst=ab[dst], src=ab[src]):
                phase = i // (stream_len // QB)
                blk = i % (stream_len // QB)
                tiles = [src[s, pl.ds(phase + d_mid * QB * blk, QB, stride=d_mid), :] for s in range(N_ATT_SLAB)]
                for s in range(N_ATT_SLAB):
                    dst[s, pl.ds(pl.multiple_of(i * QB, QB), QB), :] = tiles[s]
                return c

            lax.fori_loop(0, seq_len // QB, regroup, 0, unroll=UNROLL_REGROUP)
        qkv = tuple(ab[i] for i in order[0:3])
        acc = tuple(ab[i] for i in order[3:6])
        att = ab[order[6]]

        def mid_first(ph, c):
            rows = pl.ds(pl.multiple_of(ph * stream_len, QB), QB)
            store(acc, rows, block(qkv, rows, rows, cur_ok2, acc))
            return c

        lax.fori_loop(0, d_mid, mid_first, 0, unroll=2)
        later_blocks = stream_len // QB - 1

        def mid_unit(i, c):
            start = pl.multiple_of((i // later_blocks) * stream_len + (i % later_blocks + 1) * QB, QB)
            rows = pl.ds(start, QB)
            store(acc, rows, block(qkv, rows, pl.ds(start - QB, 2 * QB), prev_cur_ok2, acc))
            return c

        lax.fori_loop(0, d_mid * later_blocks, mid_unit, 0, unroll=UNROLL_MID)

        def far_unit(r, c):
            rows = pl.ds((r % d_mid) * stream_len + r // d_mid, QB, stride=ratio)
            outs = block(qkv, rows, rows, cur_ok2, acc)
            for s, (o_n, _, l_n) in enumerate(outs):
                att[s, pl.ds(r, QB, stride=d_far), :] = o_n / l_n
            return c

        lax.fori_loop(0, d_far, far_unit, 0, unroll=UNROLL_FAR)

    @pl.when(t >= nt)
    def _phase2():
        row0 = pl.multiple_of((t - nt) * TQ, TQ)

        def att_chunk(i, c):
            r = pl.multiple_of(i * RC, RC)
            grow = pl.ds(pl.multiple_of(row0 + r, RC), RC)
            for s in range(N_ATT_SLAB):
                cols = slice(s * LANES, (s + 1) * LANES)
                mcols = slice(C_CONV + C_POOL + s * LANES, C_CONV + C_POOL + (s + 1) * LANES)
                yc = ab[AB_ATT][s, grow, :] * mix[grow, mcols].astype(F32)
                mix[grow, mcols] = yc.astype(BF16)
            return c

        sample_window()
        lax.fori_loop(0, TQ // RC, att_chunk, 0, unroll=True)
        y_ref[...] = x_ref[...] + jnp.dot(mix[pl.ds(row0, TQ), :], wout_ref[...], preferred_element_type=F32)
        _sample_finish(jnp.logical_and(b == n_batch - 1, t == 2 * nt - 1), xs_ref, wout_ref, ys_ref,
                       att_s, mixab_s, cgs_s)


def _layer_kernel_with_alias(*refs, n_alias, **kw):
    _layer_kernel(*refs[:N_LAYER_INPUTS], *refs[N_LAYER_INPUTS + n_alias:], **kw)


def _trunk_layer(layer, pos, x, xs, state_conv, state_pool, kt, vt,
                 norm_g, w_in, conv_w, conv_b, ln_g, ln_b, pool_wbd, pool_scale, qg, kg,
                 cos_t, sin_t, cos_s, sin_s, mavg, w_out, kv_prev=None):
    bsz, seq_len, _ = x.shape
    nsmp = xs.shape[0]
    depth, _, _, win_len = kt.shape
    assert seq_len % TQ == 0 and seq_len == QB * DILATIONS[-1] and DILATIONS[0] == 1
    assert win_len == WINDOW_KEYS * DILATIONS[-1]
    nt = seq_len // TQ
    blocks_per_sample, rem = divmod(bsz * 2 * nt, nsmp)
    assert rem == 0 and blocks_per_sample >= 1 and win_len % (blocks_per_sample * LANES) == 0
    win_blk = win_len // blocks_per_sample

    def sample_block(b, t):
        step = b * (2 * nt) + t
        return (layer, step // blocks_per_sample, 0, blocks_per_sample - 1 - step % blocks_per_sample)

    def const(shape):
        nd = len(shape)
        return pl.BlockSpec(shape, lambda b, t: (0,) * nd)

    def resident(shape):
        nd = len(shape)
        return pl.BlockSpec(shape, lambda b, t: (0,) * nd, pipeline_mode=pl.Buffered(1))

    def per_layer(shape, **kw):
        nd = len(shape)
        return pl.BlockSpec((None,) + shape, lambda b, t: (layer,) + (0,) * nd, **kw)

    in_specs = [
        pl.BlockSpec((None, TQ, D_MODEL), lambda b, t: (b, t % nt, 0)),
        per_layer((1, D_MODEL)),
        per_layer((D_MODEL, D_IN), pipeline_mode=pl.Buffered(1)),
        per_layer((CONV_W, C_CONV)),
        per_layer((1, C_CONV)),
        per_layer((1, C_CONV)),
        per_layer((1, C_CONV)),
        per_layer((C_POOL, C_POOL)),
        per_layer((1, C_POOL)),
        per_layer((1, C_ATT)),
        per_layer((1, C_ATT)),
        pl.BlockSpec((TQ, LANES), lambda b, t: (jnp.minimum(t, nt - 1), 0)),
        pl.BlockSpec((TQ, LANES), lambda b, t: (jnp.minimum(t, nt - 1), 0)),
        const((C_ATT, C_ATT)),
        per_layer((D_MODEL, D_MODEL), pipeline_mode=pl.Buffered(1)),
        const((nsmp, D_MODEL)),
        per_layer((nsmp, CONV_HALO, C_CONV), pipeline_mode=pl.Buffered(1)),
        per_layer((nsmp, POOL_BUF, C_POOL), pipeline_mode=pl.Buffered(1)),
        pl.BlockSpec((None, None, C_ATT, win_blk), sample_block),
        pl.BlockSpec((None, None, C_ATT, win_blk), sample_block),
        const((1, LANES)),
        const((1, LANES)),
    ]
    operands = [x, norm_g, w_in, conv_w, conv_b, ln_g, ln_b, pool_wbd, pool_scale, qg, kg, cos_t, sin_t, mavg, w_out,
                xs, state_conv, state_pool, kt, vt, cos_s, sin_s]
    assert len(operands) == N_LAYER_INPUTS
    kv_spec = pl.BlockSpec((None, None, C_ATT, TQ), lambda b, t: (layer, b, 0, jnp.minimum(t, nt - 1)))
    out_specs = [
        pl.BlockSpec((None, TQ, D_MODEL), lambda b, t: (b, jnp.maximum(t - nt, 0), 0)),
        kv_spec,
        kv_spec,
        pl.BlockSpec((None, CONV_HALO, C_CONV), lambda b, t: (b, 0, 0)),
        pl.BlockSpec((None, POOL_BUF, C_POOL), lambda b, t: (b, 0, 0)),
        const((nsmp, D_MODEL)),
        pl.BlockSpec((nsmp, CONV_HALO, C_CONV), lambda b, t: (0, 0, 0), pipeline_mode=pl.Buffered(1)),
        pl.BlockSpec((nsmp, POOL_BUF, C_POOL), lambda b, t: (0, 0, 0), pipeline_mode=pl.Buffered(1)),
        pl.BlockSpec((None, None, C_ATT, win_blk), sample_block),
        pl.BlockSpec((None, None, C_ATT, win_blk), sample_block),
    ]
    out_shape = [
        jax.ShapeDtypeStruct((bsz, seq_len, D_MODEL), F32),
        jax.ShapeDtypeStruct((depth, bsz, C_ATT, seq_len), F32),
        jax.ShapeDtypeStruct((depth, bsz, C_ATT, seq_len), F32),
        jax.ShapeDtypeStruct((bsz, CONV_HALO, C_CONV), F32),
        jax.ShapeDtypeStruct((bsz, POOL_BUF, C_POOL), F32),
        jax.ShapeDtypeStruct((nsmp, D_MODEL), F32),
        jax.ShapeDtypeStruct((nsmp, CONV_HALO, C_CONV), F32),
        jax.ShapeDtypeStruct((nsmp, POOL_BUF, C_POOL), F32),
        jax.ShapeDtypeStruct((depth, nsmp, C_ATT, win_len), F32),
        jax.ShapeDtypeStruct((depth, nsmp, C_ATT, win_len), F32),
    ]
    static = dict(seq_len=seq_len, win_len=win_len, pos=pos)
    if kv_prev is None:
        kern = functools.partial(_layer_kernel, **static)
        aliases = {}
    else:
        kern = functools.partial(_layer_kernel_with_alias, n_alias=len(kv_prev), **static)
        in_specs += [pl.BlockSpec(memory_space=pl.ANY)] * len(kv_prev)
        operands += list(kv_prev)
        aliases = {N_LAYER_INPUTS + i: o for i, o in enumerate((1, 2, 8, 9))}
    scratch = [
        pltpu.VMEM((TQ, D_MODEL), BF16),
        pltpu.VMEM((TQ, D_IN), F32),
        pltpu.VMEM((N_CONV_SLAB, TQ + U_HALO, LANES), F32),
        pltpu.VMEM((N_POOL_SLAB, TQ + B_HALO, LANES), F32),
        pltpu.VMEM((TQ, 2 * C_ATT), BF16),
        pltpu.VMEM((TQ, 2 * C_ATT), F32),
        pltpu.VMEM((TQ, C_POOL), BF16),
        pltpu.VMEM((TQ, C_POOL), F32),
        *[pltpu.VMEM((N_ATT_SLAB, seq_len, LANES), F32) for _ in range(AB_COUNT)],
        pltpu.VMEM((seq_len, D_MODEL), BF16),
        pltpu.VMEM((nsmp, C_ATT), F32),
        pltpu.VMEM((nsmp, C_ATT), F32),
        pltpu.VMEM((nsmp, C_ATT), F32),
        pltpu.VMEM((nsmp, C_ATT), F32),
        pltpu.VMEM((nsmp, C_CONV + C_POOL), F32),
        pltpu.VMEM((nsmp, C_ATT), F32),
        pltpu.VMEM((N_HEADS, LANES), F32),
        pltpu.VMEM((N_HEADS, LANES), F32),
        pltpu.VMEM((C_ATT, LANES), F32),
        pltpu.VMEM((C_ATT, LANES), F32),
        pltpu.VMEM((C_ATT, LANES), F32),
    ]
    return pl.pallas_call(
        kern,
        out_shape=out_shape,
        grid=(bsz, 2 * nt),
        in_specs=in_specs,
        out_specs=out_specs,
        scratch_shapes=scratch,
        input_output_aliases=aliases,
        compiler_params=pltpu.CompilerParams(
            dimension_semantics=("arbitrary", "arbitrary"),
            vmem_limit_bytes=VMEM_LIMIT_BYTES,
        ),
        name="trunk_layer",
    )(*operands)


def _row_to_col_tile(row):
    return jnp.broadcast_to(row, (LANES, row.shape[1])).T


def _key_multiplicity(lane0, blk, win_len):
    t = lane0 + lax.broadcasted_iota(jnp.int32, (1, blk), 1)
    delta = win_len - t
    cnt = jnp.zeros((1, blk), F32)
    for d in DILATIONS:
        hit = jnp.logical_and(delta % d == 0, delta <= d * WINDOW_KEYS)
        cnt = cnt + hit.astype(F32)
    return cnt


def _sample_prepare(cond, x_ref, sc_ref, sp_ref,
                    ng_ref, win_ref, cw_ref, cb_ref, lng_ref, lnb_ref, pw_ref, psc_ref, qg_ref, kg_ref,
                    cos_ref, sin_ref, mavg_ref, nc_ref, np_ref,
                    qr_s, kr_s, v_s, att_s, mixab_s, cg_s, *, pos):
    nsmp = x_ref.shape[0]

    @pl.when(cond)
    def _prepare():
        x = x_ref[...]
        ms = jnp.mean(x * x, axis=-1, keepdims=True)
        h = (x * lax.rsqrt(ms + EPS) * ng_ref[...]).astype(BF16)
        proj = jnp.dot(h, win_ref[...], preferred_element_type=F32)

        u = proj[:, OFF_A_VAL:OFF_A_VAL + C_CONV] * jax.nn.sigmoid(proj[:, OFF_A_GLU:OFF_A_GLU + C_CONV])
        conv = u * cw_ref[CONV_HALO:CONV_W, :] + cb_ref[...]
        for w in range(CONV_HALO):
            conv = conv + sc_ref[:, w, :] * cw_ref[w:w + 1, :]
        nc_ref[:, 0:CONV_HALO - 1, :] = sc_ref[:, 1:CONV_HALO, :]
        nc_ref[:, CONV_HALO - 1, :] = u
        mu = jnp.mean(conv, axis=-1, keepdims=True)
        cen = conv - mu
        var = jnp.mean(cen * cen, axis=-1, keepdims=True)
        ln = cen * lax.rsqrt(var + EPS) * lng_ref[...] + lnb_ref[...]
        ya = _silu(ln) * _silu(proj[:, OFF_A_GATE:OFF_A_GATE + C_CONV])

        bval = proj[:, OFF_B_VAL:OFF_B_VAL + C_POOL]
        lane_p = lax.broadcasted_iota(jnp.int32, (nsmp, C_POOL), 1)
        pooled = jnp.zeros((nsmp, C_POOL), F32)
        acc = bval
        done = 1
        for wi, w in enumerate(POOL_WINDOWS):
            for i in range(done, w):
                acc = acc + sp_ref[:, POOL_BUF - i, :]
            done = w
            pooled = jnp.where(lane_p // POOL_GC == wi, acc / float(min(pos + 1, w)), pooled)
        np_ref[:, 0:POOL_BUF - 1, :] = sp_ref[:, 1:POOL_BUF, :]
        np_ref[:, POOL_BUF - 1, :] = bval
        dpool = (pooled - bval).astype(BF16)
        yb = (jnp.dot(dpool, pw_ref[...], preferred_element_type=F32) * psc_ref[...]
              * _silu(proj[:, OFF_B_GATE:OFF_B_GATE + C_POOL]))
        mixab_s[:, 0:C_CONV] = ya
        mixab_s[:, C_CONV:C_CONV + C_POOL] = yb
        cg_s[...] = _silu(proj[:, OFF_C_GATE:OFF_C_GATE + C_ATT])

        q = proj[:, OFF_Q:OFF_Q + C_ATT]
        k = proj[:, OFF_K:OFF_K + C_ATT]
        qn = q * lax.rsqrt(jnp.dot((q * q).astype(BF16), mavg_ref[...], preferred_element_type=F32) + EPS) * qg_ref[...]
        kn = k * lax.rsqrt(jnp.dot((k * k).astype(BF16), mavg_ref[...], preferred_element_type=F32) + EPS) * kg_ref[...]
        for s in range(N_ATT_SLAB):
            cols = slice(s * LANES, (s + 1) * LANES)
            qs, ks = qn[:, cols], kn[:, cols]
            qr_s[:, cols] = (qs * cos_ref[...] + _swap_halves(qs) * sin_ref[...]) * (HEAD_DIM ** -0.5)
            kr_s[:, cols] = ks * cos_ref[...] + _swap_halves(ks) * sin_ref[...]
        v_s[...] = proj[:, OFF_V:OFF_V + C_ATT]
        att_s[...] = jnp.zeros((nsmp, C_ATT), F32)


def _sample_window(first, last, smp, lane0, kt_ref, vt_ref, okt_ref, ovt_ref,
                   qr_s, kr_s, v_s, att_s, sm_s, sl_s, so_s, kcar_s, vcar_s, *, win_len):
    nsmp = qr_s.shape[0]
    blk = kt_ref.shape[1]
    n_tiles = blk // LANES
    n_pat = float(len(DILATIONS))
    mine = lax.broadcasted_iota(jnp.int32, (nsmp, C_ATT), 0) == smp

    def col_tile(ref):
        return _row_to_col_tile(jnp.sum(jnp.where(mine, ref[...], 0.0), axis=0, keepdims=True))

    q_col = col_tile(qr_s)
    k_col = col_tile(kr_s)
    v_col = col_tile(v_s)
    cnt = _key_multiplicity(lane0, blk, win_len)
    reach = cnt > 0.0
    last_lane = lax.broadcasted_iota(jnp.int32, (HEAD_DIM, blk), 1) == blk - 1

    att_cols = []
    for hd in range(N_HEADS):
        hr = slice(hd * HEAD_DIM, (hd + 1) * HEAD_DIM)
        kt = kt_ref[hr, :]
        vt = vt_ref[hr, :]
        qh = q_col[hr, :]
        s_new = jnp.sum(k_col[hr, :] * qh, axis=0, keepdims=True)
        m_old = jnp.where(first, s_new, sm_s[hd:hd + 1, :])
        l_old = jnp.where(first, n_pat, sl_s[hd:hd + 1, :])
        o_old = jnp.where(first, v_col[hr, :] * n_pat, so_s[hr, :])
        k_next = jnp.where(first, k_col[hr, :], kcar_s[hr, :])
        v_next = jnp.where(first, v_col[hr, :], vcar_s[hr, :])

        s_win = jnp.sum(kt * jnp.concatenate([qh] * n_tiles, axis=1), axis=0, keepdims=True)
        s_win = jnp.where(reach, s_win, NEG)
        m_new = jnp.maximum(m_old, jnp.max(s_win, axis=-1, keepdims=True))
        w_old = jnp.exp(m_old - m_new)
        p_win = cnt * jnp.exp(s_win - m_new[:, 0:1])
        l_new = l_old * w_old + jnp.sum(p_win, axis=-1, keepdims=True)
        o_new = o_old * w_old + jnp.sum(vt * p_win, axis=-1, keepdims=True)
        sm_s[hd:hd + 1, :] = m_new
        sl_s[hd:hd + 1, :] = l_new
        so_s[hr, :] = o_new
        att_cols.append(o_new / l_new)
        okt_ref[hr, :] = jnp.where(last_lane, jnp.concatenate([k_next] * n_tiles, axis=1), pltpu.roll(kt, blk - 1, 1))
        ovt_ref[hr, :] = jnp.where(last_lane, jnp.concatenate([v_next] * n_tiles, axis=1), pltpu.roll(vt, blk - 1, 1))
        kcar_s[hr, :] = jnp.broadcast_to(kt[:, 0:1], (HEAD_DIM, LANES))
        vcar_s[hr, :] = jnp.broadcast_to(vt[:, 0:1], (HEAD_DIM, LANES))
    att_row = jnp.concatenate(att_cols, axis=0).T[0:1, :]
    att_s[...] = jnp.where(jnp.logical_and(mine, last), att_row, att_s[...])


def _sample_finish(cond, x_ref, wout_ref, y_ref, att_s, mixab_s, cg_s):
    @pl.when(cond)
    def _finish():
        yc = att_s[...] * cg_s[...]
        mixed = jnp.concatenate([mixab_s[...], yc], axis=-1).astype(BF16)
        y_ref[...] = x_ref[...] + jnp.dot(mixed, wout_ref[...], preferred_element_type=F32)


def _to_channel_major(a):
    depth, bsz, ntok, nh, hd = a.shape
    return jnp.transpose(a, (0, 1, 3, 4, 2)).reshape(depth, bsz, nh * hd, ntok)


def _from_channel_major(a):
    depth, bsz, _, ntok = a.shape
    return jnp.transpose(a.reshape(depth, bsz, N_HEADS, HEAD_DIM, ntok), (0, 1, 4, 2, 3))


def kernel(x_prompt, x_sample, state_conv, state_pool, cache_k_win, cache_v_win, norm_g, w_in, conv_w, conv_b,
           ln_g, ln_b, pool_w, pool_scale, q_norm_g, k_norm_g, w_out):
    depth = w_in.shape[0]
    seq_len = x_prompt.shape[1]
    nsmp = x_sample.shape[0]

    cos_p, sin_p = _rope_tables(jnp.arange(seq_len, dtype=jnp.int32))
    cos_s, sin_s = _rope_tables(jnp.full((1,), PAST_LEN, dtype=jnp.int32))
    mavg = _head_mean_matrix()
    weights = (norm_g[:, None], w_in.astype(BF16), conv_w, conv_b[:, None], ln_g[:, None], ln_b[:, None],
               _pool_block_diag(pool_w).astype(BF16), pool_scale[:, None],
               jnp.tile(q_norm_g, (1, N_HEADS))[:, None], jnp.tile(k_norm_g, (1, N_HEADS))[:, None])
    w_out_b = w_out.astype(BF16)

    kt = _to_channel_major(cache_k_win)
    vt = _to_channel_major(cache_v_win)
    xp = x_prompt
    xs = x_sample.reshape(nsmp, D_MODEL)
    kv = None
    conv_p, pool_p, conv_s, pool_s = [], [], [], []
    for layer in range(depth):
        xp, kp, vp, cst, pst, xs, ncs, nps, ks, vs = _trunk_layer(
            layer, PAST_LEN, xp, xs, state_conv, state_pool, kt, vt, *weights,
            cos_p, sin_p, cos_s, sin_s, mavg, w_out_b, kv_prev=kv)
        kv = (kp, vp, ks, vs)
        conv_p.append(cst)
        pool_p.append(pst)
        conv_s.append(ncs)
        pool_s.append(nps)

    return (xp, xs.reshape(nsmp, 1, D_MODEL), jnp.stack(conv_p), jnp.stack(pool_p),
            _from_channel_major(kv[0]), _from_channel_major(kv[1]),
            jnp.stack(conv_s), jnp.stack(pool_s), _from_channel_major(kv[2]), _from_channel_major(kv[3]))
```

```python
import functools

import jax
import jax.numpy as jnp
import numpy as np
from jax import lax
from jax.experimental import pallas as pl
from jax.experimental.pallas import tpu as pltpu

F32 = jnp.float32
BF16 = jnp.bfloat16

D_MODEL = 1024
C_CONV = 384
C_POOL = 256
C_ATT = 384
HEAD_DIM = 64
N_HEADS = C_ATT // HEAD_DIM
CONV_W = 31
CONV_HALO = CONV_W - 1
POOL_WINDOWS = (2, 4, 8, 16)
POOL_GC = 64
POOL_BUF = 15
DILATIONS = (1, 4, 16)
WINDOW_KEYS = 128
EPS = 1e-6
ROPE_THETA = 10000.0
D_IN = 3 * C_CONV + 2 * C_POOL + 4 * C_ATT
PAST_LEN = 16384
NEG = -1e30

OFF_A_VAL = 0
OFF_A_GLU = OFF_A_VAL + C_CONV
OFF_A_GATE = OFF_A_GLU + C_CONV
OFF_B_VAL = OFF_A_GATE + C_CONV
OFF_B_GATE = OFF_B_VAL + C_POOL
OFF_Q = OFF_B_GATE + C_POOL
OFF_K = OFF_Q + C_ATT
OFF_V = OFF_K + C_ATT
OFF_C_GATE = OFF_V + C_ATT

LANES = 128
N_CONV_SLAB = C_CONV // LANES
N_POOL_SLAB = C_POOL // LANES
N_ATT_SLAB = C_ATT // LANES
VMEM_LIMIT_BYTES = 60 * 1024 * 1024

TQ = 256
RC = 32
U_HALO = 32
B_HALO = 16
QB = WINDOW_KEYS
UNROLL_NEAR = 15
UNROLL_MID = 6
UNROLL_FAR = 8
UNROLL_REGROUP = 4

AB_Q, AB_K, AB_V, AB_O, AB_M, AB_L, AB_SPARE = range(7)
AB_COUNT = 7
AB_ATT = AB_L


def _sigmoid(x):
    return 0.5 * jnp.tanh(0.5 * x) + 0.5


def _silu(x):
    h = 0.5 * x
    return h * jnp.tanh(h) + h


def _rope_tables(positions):
    half = HEAD_DIM // 2
    inv = ROPE_THETA ** (-jnp.arange(half, dtype=F32) / half)
    ang = positions.astype(F32)[:, None] * inv[None, :]
    cos = jnp.cos(ang)
    sin = jnp.sin(ang)
    cos_h = jnp.concatenate([cos, cos], axis=-1)
    sin_h = jnp.concatenate([-sin, sin], axis=-1)
    reps = LANES // HEAD_DIM
    return jnp.tile(cos_h, (1, reps)), jnp.tile(sin_h, (1, reps))


def _head_mean_matrix():
    idx = np.arange(C_ATT) // HEAD_DIM
    return jnp.asarray((idx[:, None] == idx[None, :]).astype(np.float32) / HEAD_DIM, dtype=BF16)


def _pool_block_diag(pool_w):
    out = jnp.zeros((pool_w.shape[0], C_POOL, C_POOL), pool_w.dtype)
    for g in range(len(POOL_WINDOWS)):
        out = out.at[:, g * POOL_GC:(g + 1) * POOL_GC, g * POOL_GC:(g + 1) * POOL_GC].set(pool_w[:, g])
    return out


def _swap_halves(x):
    lane = lax.broadcasted_iota(jnp.int32, x.shape, 1)
    first_half = (lane % HEAD_DIM) < (HEAD_DIM // 2)
    return jnp.where(first_half, pltpu.roll(x, LANES - HEAD_DIM // 2, 1), pltpu.roll(x, HEAD_DIM // 2, 1))


def _pool_means(loads, pos):
    lane = lax.broadcasted_iota(jnp.int32, loads(0, 0).shape, 1)
    lo = lane < POOL_GC
    posf = (pos + 1).astype(F32)
    outs = []
    for slab in range(N_POOL_SLAB):
        w_lo, w_hi = POOL_WINDOWS[2 * slab], POOL_WINDOWS[2 * slab + 1]
        cur = loads(0, slab)
        s = cur
        for i in range(1, w_lo):
            s = s + loads(i, slab)
        s_lo = s
        for i in range(w_lo, w_hi):
            s = s + loads(i, slab)
        s_hi = s
        cnt_lo = jnp.minimum(posf, float(w_lo))
        cnt_hi = jnp.minimum(posf, float(w_hi))
        pooled = jnp.where(lo, s_lo / cnt_lo, s_hi / cnt_hi)
        outs.append(pooled - cur)
    return outs


N_LAYER_INPUTS = 22


def _layer_kernel(x_ref, ng_ref, win_ref, cw_ref, cb_ref, lng_ref, lnb_ref, pw_ref, psc_ref,
                  qg_ref, kg_ref, cos_ref, sin_ref, mavg_ref, wout_ref,
                  xs_ref, sc_ref, sp_ref, kt_ref, vt_ref, coss_ref, sins_ref,
                  y_ref, ko_ref, vo_ref, cst_ref, pst_ref,
                  ys_ref, ncs_ref, nps_ref, okt_ref, ovt_ref,
                  h_s, proj, u_buf, b_buf, sq_s, msq_s, d_s, yb_s,
                  ab0, ab1, ab2, ab3, ab4, ab5, ab6, mix,
                  qr_s, kr_s, v_s, att_s, mixab_s, cgs_s, qcol_s, sm_s, sl_s, so_s, kcar_s, vcar_s,
                  *, seq_len, win_len, pos):
    ab = (ab0, ab1, ab2, ab3, ab4, ab5, ab6)
    nt = seq_len // TQ
    b = pl.program_id(0)
    t = pl.program_id(1)
    n_batch = pl.num_programs(0)
    blocks_per_sample = win_len // kt_ref.shape[1]

    step = b * (2 * nt) + t
    part = step % blocks_per_sample
    smp = step // blocks_per_sample

    def sample_start():
        _sample_start(part == 0, smp, qr_s, kr_s, v_s, qcol_s, sm_s, sl_s, so_s, kcar_s, vcar_s)

    def sample_window():
        _sample_window((blocks_per_sample - 1 - part) * kt_ref.shape[1], kt_ref, vt_ref, okt_ref, ovt_ref,
                       qcol_s, sm_s, sl_s, so_s, kcar_s, vcar_s, win_len=win_len)

    def sample_done():
        _sample_done(part == blocks_per_sample - 1, smp, att_s, sl_s, so_s)

    @pl.when(t < nt)
    def _phase1():
        row0 = pl.multiple_of(t * TQ, TQ)

        _sample_prepare(jnp.logical_and(b == 0, t == 0), xs_ref, sc_ref, sp_ref,
                        ng_ref, win_ref, cw_ref, cb_ref, lng_ref, lnb_ref, pw_ref, psc_ref, qg_ref, kg_ref,
                        coss_ref, sins_ref, mavg_ref, ncs_ref, nps_ref,
                        qr_s, kr_s, v_s, att_s, mixab_s, cgs_s, pos=pos)

        @pl.when(t == 0)
        def _zero_halo():
            u_buf[:, 0:U_HALO, :] = jnp.zeros((N_CONV_SLAB, U_HALO, LANES), F32)
            b_buf[:, 0:B_HALO, :] = jnp.zeros((N_POOL_SLAB, B_HALO, LANES), F32)

        sample_start()
        sample_window()

        def norm_chunk(i, c):
            r = pl.multiple_of(i * RC, RC)
            x = x_ref[pl.ds(r, RC), :]
            ms = jnp.mean(x * x, axis=-1, keepdims=True)
            h_s[pl.ds(r, RC), :] = (x * lax.rsqrt(ms + EPS) * ng_ref[...]).astype(BF16)
            return c

        lax.fori_loop(0, TQ // RC, norm_chunk, 0, unroll=True)
        proj[...] = jnp.dot(h_s[...], win_ref[...], preferred_element_type=F32)

        def split_chunk(i, c):
            r = pl.multiple_of(i * RC, RC)
            rows = pl.ds(r, RC)
            grow = pl.ds(pl.multiple_of(row0 + r, RC), RC)
            for s in range(N_CONV_SLAB):
                cols = slice(s * LANES, (s + 1) * LANES)
                a_val = proj[rows, OFF_A_VAL + s * LANES:OFF_A_VAL + (s + 1) * LANES]
                a_glu = proj[rows, OFF_A_GLU + s * LANES:OFF_A_GLU + (s + 1) * LANES]
                u_buf[s, pl.ds(U_HALO + r, RC), :] = a_val * _sigmoid(a_glu)
                q = proj[rows, OFF_Q + s * LANES:OFF_Q + (s + 1) * LANES]
                k = proj[rows, OFF_K + s * LANES:OFF_K + (s + 1) * LANES]
                sq_s[rows, cols] = (q * q).astype(BF16)
                sq_s[rows, C_ATT + s * LANES:C_ATT + (s + 1) * LANES] = (k * k).astype(BF16)
                ab[AB_V][s, grow, :] = proj[rows, OFF_V + s * LANES:OFF_V + (s + 1) * LANES]
                mix[grow, C_CONV + C_POOL + s * LANES:C_CONV + C_POOL + (s + 1) * LANES] = _silu(
                    proj[rows, OFF_C_GATE + s * LANES:OFF_C_GATE + (s + 1) * LANES]).astype(BF16)
            for s in range(N_POOL_SLAB):
                b_buf[s, pl.ds(B_HALO + r, RC), :] = proj[rows, OFF_B_VAL + s * LANES:OFF_B_VAL + (s + 1) * LANES]
            pos = row0 + r + lax.broadcasted_iota(jnp.int32, (RC, 1), 0)
            dl = _pool_means(lambda sh, s: b_buf[s, pl.ds(r + B_HALO - sh, RC), :], pos)
            for s in range(N_POOL_SLAB):
                d_s[rows, s * LANES:(s + 1) * LANES] = dl[s].astype(BF16)
            return c

        lax.fori_loop(0, TQ // RC, split_chunk, 0, unroll=True)
        msq_s[:, 0:C_ATT] = jnp.dot(sq_s[:, 0:C_ATT], mavg_ref[...], preferred_element_type=F32)
        msq_s[:, C_ATT:2 * C_ATT] = jnp.dot(sq_s[:, C_ATT:2 * C_ATT], mavg_ref[...], preferred_element_type=F32)
        yb_s[...] = jnp.dot(d_s[...], pw_ref[...], preferred_element_type=F32)

        def mixer_chunk(i, c):
            r = pl.multiple_of(i * RC, RC)
            rows = pl.ds(r, RC)
            grow = pl.ds(pl.multiple_of(row0 + r, RC), RC)
            conv = []
            for s in range(N_CONV_SLAB):
                cols = slice(s * LANES, (s + 1) * LANES)
                acc = jnp.zeros((RC, LANES), F32) + cb_ref[:, cols]
                for w in range(CONV_W):
                    acc = acc + u_buf[s, pl.ds(r + (U_HALO - CONV_HALO) + w, RC), :] * cw_ref[w:w + 1, cols]
                conv.append(acc)
            mu = jnp.sum(conv[0] + conv[1] + conv[2], axis=-1, keepdims=True) * (1.0 / C_CONV)
            cen = [cv - mu for cv in conv]
            var = jnp.sum(cen[0] * cen[0] + cen[1] * cen[1] + cen[2] * cen[2], axis=-1, keepdims=True) * (1.0 / C_CONV)
            rstd = lax.rsqrt(var + EPS)
            for s in range(N_CONV_SLAB):
                cols = slice(s * LANES, (s + 1) * LANES)
                ln = cen[s] * rstd * lng_ref[:, cols] + lnb_ref[:, cols]
                gate = proj[rows, OFF_A_GATE + s * LANES:OFF_A_GATE + (s + 1) * LANES]
                mix[grow, cols] = (_silu(ln) * _silu(gate)).astype(BF16)
            for s in range(N_POOL_SLAB):
                cols = slice(s * LANES, (s + 1) * LANES)
                gate = proj[rows, OFF_B_GATE + s * LANES:OFF_B_GATE + (s + 1) * LANES]
                yb = yb_s[rows, cols] * psc_ref[:, cols] * _silu(gate)
                mix[grow, C_CONV + s * LANES:C_CONV + (s + 1) * LANES] = yb.astype(BF16)
            cos = cos_ref[rows, :]
            sin = sin_ref[rows, :]
            for s in range(N_ATT_SLAB):
                cols = slice(s * LANES, (s + 1) * LANES)
                q = proj[rows, OFF_Q + s * LANES:OFF_Q + (s + 1) * LANES]
                qn = q * lax.rsqrt(msq_s[rows, cols] + EPS) * qg_ref[:, cols]
                qr = qn * cos + _swap_halves(qn) * sin
                ab[AB_Q][s, grow, :] = qr * (HEAD_DIM ** -0.5)
                k = proj[rows, OFF_K + s * LANES:OFF_K + (s + 1) * LANES]
                kn = k * lax.rsqrt(msq_s[rows, C_ATT + s * LANES:C_ATT + (s + 1) * LANES] + EPS) * kg_ref[:, cols]
                ab[AB_K][s, grow, :] = kn * cos + _swap_halves(kn) * sin
            return c

        lax.fori_loop(0, TQ // RC, mixer_chunk, 0, unroll=True)

        for s in range(N_ATT_SLAB):
            ko_ref[s * LANES:(s + 1) * LANES, :] = ab[AB_K][s, pl.ds(row0, TQ), :].T
            vo_ref[s * LANES:(s + 1) * LANES, :] = ab[AB_V][s, pl.ds(row0, TQ), :].T

        @pl.when(t == nt - 1)
        def _write_state():
            for s in range(N_CONV_SLAB):
                cst_ref[:, s * LANES:(s + 1) * LANES] = u_buf[s, TQ + U_HALO - CONV_HALO:TQ + U_HALO, :]
            for s in range(N_POOL_SLAB):
                pst_ref[:, s * LANES:(s + 1) * LANES] = b_buf[s, TQ + B_HALO - POOL_BUF:TQ + B_HALO, :]

        u_buf[:, 0:U_HALO, :] = u_buf[:, TQ:TQ + U_HALO, :]
        b_buf[:, 0:B_HALO, :] = b_buf[:, TQ:TQ + B_HALO, :]
        sample_done()

    @pl.when(t == nt - 1)
    def _attention():
        d_mid, d_far = DILATIONS[1], DILATIONS[2]
        ratio = d_far // d_mid
        stream_len = seq_len // d_mid
        lane = lax.broadcasted_iota(jnp.int32, (QB, LANES), 1)
        rowi = lax.broadcasted_iota(jnp.int32, (QB, LANES), 0)
        lo = lane < HEAD_DIM
        cur_ok = lane <= rowi
        prev_ok = lane >= rowi
        cur_ok2 = jnp.concatenate([cur_ok, cur_ok], axis=0)
        prev_cur_ok2 = jnp.concatenate([jnp.concatenate([prev_ok, prev_ok], axis=0), cur_ok2], axis=1)

        def attend(q, keys, vals, mask):
            qa = jnp.where(lo, q, 0.0).astype(BF16)
            qb = jnp.where(lo, 0.0, q).astype(BF16)
            q2 = jnp.concatenate([qa, qb], axis=0)
            sc = lax.dot_general(q2, keys.astype(BF16), (((1,), (1,)), ((), ())), preferred_element_type=F32)
            sc = jnp.where(mask, sc, NEG)
            m = jnp.max(sc, axis=-1, keepdims=True)
            p = jnp.exp(sc - m).astype(BF16)
            v1 = jnp.concatenate([vals.astype(BF16), jnp.ones(vals.shape, BF16)], axis=1)
            ol = jnp.dot(p, v1, preferred_element_type=F32)
            o_u = jnp.where(lo, ol[0:QB, 0:LANES], ol[QB:2 * QB, 0:LANES])
            l_u = jnp.where(lo, ol[0:QB, LANES:2 * LANES], ol[QB:2 * QB, LANES:2 * LANES])
            m_u = jnp.where(lo, m[0:QB], m[QB:2 * QB])
            return o_u, m_u, l_u

        def block(bufs, rows, krows, mask, acc=None):
            b_q, b_k, b_v = bufs
            outs = []
            for s in range(N_ATT_SLAB):
                o_u, m_u, l_u = attend(b_q[s, rows, :], b_k[s, krows, :], b_v[s, krows, :], mask)
                if acc is not None:
                    a_o, a_m, a_l = acc
                    m_old = a_m[s, rows, :]
                    m_new = jnp.maximum(m_old, m_u)
                    w_old = jnp.exp(m_old - m_new)
                    w_u = jnp.exp(m_u - m_new)
                    o_u = a_o[s, rows, :] * w_old + o_u * w_u
                    l_u = a_l[s, rows, :] * w_old + l_u * w_u
                    m_u = m_new
                outs.append((o_u, m_u, l_u))
            return outs

        def store(acc, rows, outs):
            a_o, a_m, a_l = acc
            for s, (o_u, m_u, l_u) in enumerate(outs):
                a_o[s, rows, :] = o_u
                a_m[s, rows, :] = m_u
                a_l[s, rows, :] = l_u

        qkv = (ab[AB_Q], ab[AB_K], ab[AB_V])
        acc = (ab[AB_O], ab[AB_M], ab[AB_L])
        store(acc, pl.ds(0, QB), block(qkv, pl.ds(0, QB), pl.ds(0, QB), cur_ok2))

        def near_unit(u, c):
            start = pl.multiple_of(u * QB, QB)
            rows = pl.ds(start, QB)
            store(acc, rows, block(qkv, rows, pl.ds(start - QB, 2 * QB), prev_cur_ok2))
            return c

        lax.fori_loop(1, seq_len // QB, near_unit, 0, unroll=UNROLL_NEAR)

        order = (AB_SPARE, AB_Q, AB_K, AB_V, AB_O, AB_M, AB_L)
        for dst, src in zip(order[:-1], order[1:]):
            def regroup(i, c, dst=ab[dst], src=ab[src]):
                phase = i // (stream_len // QB)
                blk = i % (stream_len // QB)
                tiles = [src[s, pl.ds(phase + d_mid * QB * blk, QB, stride=d_mid), :] for s in range(N_ATT_SLAB)]
                for s in range(N_ATT_SLAB):
                    dst[s, pl.ds(pl.multiple_of(i * QB, QB), QB), :] = tiles[s]
                return c

            lax.fori_loop(0, seq_len // QB, regroup, 0, unroll=UNROLL_REGROUP)
        qkv = tuple(ab[i] for i in order[0:3])
        acc = tuple(ab[i] for i in order[3:6])
        att = ab[order[6]]

        def mid_first(ph, c):
            rows = pl.ds(pl.multiple_of(ph * stream_len, QB), QB)
            store(acc, rows, block(qkv, rows, rows, cur_ok2, acc))
            return c

        lax.fori_loop(0, d_mid, mid_first, 0, unroll=2)
        later_blocks = stream_len // QB - 1

        def mid_unit(i, c):
            start = pl.multiple_of((i // later_blocks) * stream_len + (i % later_blocks + 1) * QB, QB)
            rows = pl.ds(start, QB)
            store(acc, rows, block(qkv, rows, pl.ds(start - QB, 2 * QB), prev_cur_ok2, acc))
            return c

        lax.fori_loop(0, d_mid * later_blocks, mid_unit, 0, unroll=UNROLL_MID)

        def far_unit(r, c):
            rows = pl.ds((r % d_mid) * stream_len + r // d_mid, QB, stride=ratio)
            outs = block(qkv, rows, rows, cur_ok2, acc)
            for s, (o_n, _, l_n) in enumerate(outs):
                att[s, pl.ds(r, QB, stride=d_far), :] = o_n / l_n
            return c

        lax.fori_loop(0, d_far, far_unit, 0, unroll=UNROLL_FAR)

    @pl.when(t >= nt)
    def _phase2():
        row0 = pl.multiple_of((t - nt) * TQ, TQ)

        def att_chunk(i, c):
            r = pl.multiple_of(i * RC, RC)
            grow = pl.ds(pl.multiple_of(row0 + r, RC), RC)
            for s in range(N_ATT_SLAB):
                cols = slice(s * LANES, (s + 1) * LANES)
                mcols = slice(C_CONV + C_POOL + s * LANES, C_CONV + C_POOL + (s + 1) * LANES)
                yc = ab[AB_ATT][s, grow, :] * mix[grow, mcols].astype(F32)
                mix[grow, mcols] = yc.astype(BF16)
            return c

        sample_start()
        sample_window()
        lax.fori_loop(0, TQ // RC, att_chunk, 0, unroll=True)
        y_ref[...] = x_ref[...] + jnp.dot(mix[pl.ds(row0, TQ), :], wout_ref[...], preferred_element_type=F32)
        sample_done()
        _sample_finish(jnp.logical_and(b == n_batch - 1, t == 2 * nt - 1), xs_ref, wout_ref, ys_ref,
                       att_s, mixab_s, cgs_s)


def _layer_kernel_with_alias(*refs, n_alias, **kw):
    _layer_kernel(*refs[:N_LAYER_INPUTS], *refs[N_LAYER_INPUTS + n_alias:], **kw)


def _trunk_layer(layer, pos, x, xs, state_conv, state_pool, kt, vt,
                 norm_g, w_in, conv_w, conv_b, ln_g, ln_b, pool_wbd, pool_scale, qg, kg,
                 cos_t, sin_t, cos_s, sin_s, mavg, w_out, kv_prev=None):
    bsz, seq_len, _ = x.shape
    nsmp = xs.shape[0]
    depth, _, _, win_len = kt.shape
    assert seq_len % TQ == 0 and seq_len == QB * DILATIONS[-1] and DILATIONS[0] == 1
    assert win_len == WINDOW_KEYS * DILATIONS[-1]
    nt = seq_len // TQ
    blocks_per_sample, rem = divmod(bsz * 2 * nt, nsmp)
    assert rem == 0 and blocks_per_sample >= 1 and win_len % (blocks_per_sample * LANES) == 0
    win_blk = win_len // blocks_per_sample

    def sample_block(b, t):
        step = b * (2 * nt) + t
        return (layer, step // blocks_per_sample, 0, blocks_per_sample - 1 - step % blocks_per_sample)

    def const(shape):
        nd = len(shape)
        return pl.BlockSpec(shape, lambda b, t: (0,) * nd)

    def resident(shape):
        nd = len(shape)
        return pl.BlockSpec(shape, lambda b, t: (0,) * nd, pipeline_mode=pl.Buffered(1))

    def per_layer(shape, **kw):
        nd = len(shape)
        return pl.BlockSpec((None,) + shape, lambda b, t: (layer,) + (0,) * nd, **kw)

    in_specs = [
        pl.BlockSpec((None, TQ, D_MODEL), lambda b, t: (b, t % nt, 0)),
        per_layer((1, D_MODEL)),
        per_layer((D_MODEL, D_IN), pipeline_mode=pl.Buffered(1)),
        per_layer((CONV_W, C_CONV)),
        per_layer((1, C_CONV)),
        per_layer((1, C_CONV)),
        per_layer((1, C_CONV)),
        per_layer((C_POOL, C_POOL)),
        per_layer((1, C_POOL)),
        per_layer((1, C_ATT)),
        per_layer((1, C_ATT)),
        pl.BlockSpec((TQ, LANES), lambda b, t: (jnp.minimum(t, nt - 1), 0)),
        pl.BlockSpec((TQ, LANES), lambda b, t: (jnp.minimum(t, nt - 1), 0)),
        const((C_ATT, C_ATT)),
        per_layer((D_MODEL, D_MODEL), pipeline_mode=pl.Buffered(1)),
        const((nsmp, D_MODEL)),
        per_layer((nsmp, CONV_HALO, C_CONV), pipeline_mode=pl.Buffered(1)),
        per_layer((nsmp, POOL_BUF, C_POOL), pipeline_mode=pl.Buffered(1)),
        pl.BlockSpec((None, None, C_ATT, win_blk), sample_block),
        pl.BlockSpec((None, None, C_ATT, win_blk), sample_block),
        const((1, LANES)),
        const((1, LANES)),
    ]
    operands = [x, norm_g, w_in, conv_w, conv_b, ln_g, ln_b, pool_wbd, pool_scale, qg, kg, cos_t, sin_t, mavg, w_out,
                xs, state_conv, state_pool, kt, vt, cos_s, sin_s]
    assert len(operands) == N_LAYER_INPUTS
    kv_spec = pl.BlockSpec((None, None, C_ATT, TQ), lambda b, t: (layer, b, 0, jnp.minimum(t, nt - 1)))
    out_specs = [
        pl.BlockSpec((None, TQ, D_MODEL), lambda b, t: (b, jnp.maximum(t - nt, 0), 0)),
        kv_spec,
        kv_spec,
        pl.BlockSpec((None, CONV_HALO, C_CONV), lambda b, t: (b, 0, 0)),
        pl.BlockSpec((None, POOL_BUF, C_POOL), lambda b, t: (b, 0, 0)),
        const((nsmp, D_MODEL)),
        pl.BlockSpec((nsmp, CONV_HALO, C_CONV), lambda b, t: (0, 0, 0), pipeline_mode=pl.Buffered(1)),
        pl.BlockSpec((nsmp, POOL_BUF, C_POOL), lambda b, t: (0, 0, 0), pipeline_mode=pl.Buffered(1)),
        pl.BlockSpec((None, None, C_ATT, win_blk), sample_block),
        pl.BlockSpec((None, None, C_ATT, win_blk), sample_block),
    ]
    out_shape = [
        jax.ShapeDtypeStruct((bsz, seq_len, D_MODEL), F32),
        jax.ShapeDtypeStruct((depth, bsz, C_ATT, seq_len), F32),
        jax.ShapeDtypeStruct((depth, bsz, C_ATT, seq_len), F32),
        jax.ShapeDtypeStruct((bsz, CONV_HALO, C_CONV), F32),
        jax.ShapeDtypeStruct((bsz, POOL_BUF, C_POOL), F32),
        jax.ShapeDtypeStruct((nsmp, D_MODEL), F32),
        jax.ShapeDtypeStruct((nsmp, CONV_HALO, C_CONV), F32),
        jax.ShapeDtypeStruct((nsmp, POOL_BUF, C_POOL), F32),
        jax.ShapeDtypeStruct((depth, nsmp, C_ATT, win_len), F32),
        jax.ShapeDtypeStruct((depth, nsmp, C_ATT, win_len), F32),
    ]
    static = dict(seq_len=seq_len, win_len=win_len, pos=pos)
    if kv_prev is None:
        kern = functools.partial(_layer_kernel, **static)
        aliases = {}
    else:
        kern = functools.partial(_layer_kernel_with_alias, n_alias=len(kv_prev), **static)
        in_specs += [pl.BlockSpec(memory_space=pl.ANY)] * len(kv_prev)
        operands += list(kv_prev)
        aliases = {N_LAYER_INPUTS + i: o for i, o in enumerate((1, 2, 8, 9))}
    scratch = [
        pltpu.VMEM((TQ, D_MODEL), BF16),
        pltpu.VMEM((TQ, D_IN), F32),
        pltpu.VMEM((N_CONV_SLAB, TQ + U_HALO, LANES), F32),
        pltpu.VMEM((N_POOL_SLAB, TQ + B_HALO, LANES), F32),
        pltpu.VMEM((TQ, 2 * C_ATT), BF16),
        pltpu.VMEM((TQ, 2 * C_ATT), F32),
        pltpu.VMEM((TQ, C_POOL), BF16),
        pltpu.VMEM((TQ, C_POOL), F32),
        *[pltpu.VMEM((N_ATT_SLAB, seq_len, LANES), F32) for _ in range(AB_COUNT)],
        pltpu.VMEM((seq_len, D_MODEL), BF16),
        pltpu.VMEM((nsmp, C_ATT), F32),
        pltpu.VMEM((nsmp, C_ATT), F32),
        pltpu.VMEM((nsmp, C_ATT), F32),
        pltpu.VMEM((nsmp, C_ATT), F32),
        pltpu.VMEM((nsmp, C_CONV + C_POOL), F32),
        pltpu.VMEM((nsmp, C_ATT), F32),
        pltpu.VMEM((C_ATT, LANES), F32),
        pltpu.VMEM((N_HEADS, LANES), F32),
        pltpu.VMEM((N_HEADS, LANES), F32),
        pltpu.VMEM((C_ATT, LANES), F32),
        pltpu.VMEM((C_ATT, LANES), F32),
        pltpu.VMEM((C_ATT, LANES), F32),
    ]
    return pl.pallas_call(
        kern,
        out_shape=out_shape,
        grid=(bsz, 2 * nt),
        in_specs=in_specs,
        out_specs=out_specs,
        scratch_shapes=scratch,
        input_output_aliases=aliases,
        compiler_params=pltpu.CompilerParams(
            dimension_semantics=("arbitrary", "arbitrary"),
            vmem_limit_bytes=VMEM_LIMIT_BYTES,
        ),
        name="trunk_layer",
    )(*operands)


def _row_to_col_tile(row):
    return jnp.broadcast_to(row, (LANES, row.shape[1])).T


def _key_multiplicity(lane0, blk, win_len):
    t = lane0 + lax.broadcasted_iota(jnp.int32, (1, blk), 1)
    delta = win_len - t
    cnt = jnp.zeros((1, blk), F32)
    for d in DILATIONS:
        hit = jnp.logical_and(delta % d == 0, delta <= d * WINDOW_KEYS)
        cnt = cnt + hit.astype(F32)
    return cnt


def _sample_prepare(cond, x_ref, sc_ref, sp_ref,
                    ng_ref, win_ref, cw_ref, cb_ref, lng_ref, lnb_ref, pw_ref, psc_ref, qg_ref, kg_ref,
                    cos_ref, sin_ref, mavg_ref, nc_ref, np_ref,
                    qr_s, kr_s, v_s, att_s, mixab_s, cg_s, *, pos):
    nsmp = x_ref.shape[0]

    @pl.when(cond)
    def _prepare():
        x = x_ref[...]
        ms = jnp.mean(x * x, axis=-1, keepdims=True)
        h = (x * lax.rsqrt(ms + EPS) * ng_ref[...]).astype(BF16)
        proj = jnp.dot(h, win_ref[...], preferred_element_type=F32)

        u = proj[:, OFF_A_VAL:OFF_A_VAL + C_CONV] * _sigmoid(proj[:, OFF_A_GLU:OFF_A_GLU + C_CONV])
        conv = u * cw_ref[CONV_HALO:CONV_W, :] + cb_ref[...]
        for w in range(CONV_HALO):
            conv = conv + sc_ref[:, w, :] * cw_ref[w:w + 1, :]
        nc_ref[:, 0:CONV_HALO - 1, :] = sc_ref[:, 1:CONV_HALO, :]
        nc_ref[:, CONV_HALO - 1, :] = u
        mu = jnp.mean(conv, axis=-1, keepdims=True)
        cen = conv - mu
        var = jnp.mean(cen * cen, axis=-1, keepdims=True)
        ln = cen * lax.rsqrt(var + EPS) * lng_ref[...] + lnb_ref[...]
        ya = _silu(ln) * _silu(proj[:, OFF_A_GATE:OFF_A_GATE + C_CONV])

        bval = proj[:, OFF_B_VAL:OFF_B_VAL + C_POOL]
        lane_p = lax.broadcasted_iota(jnp.int32, (nsmp, C_POOL), 1)
        pooled = jnp.zeros((nsmp, C_POOL), F32)
        acc = bval
        done = 1
        for wi, w in enumerate(POOL_WINDOWS):
            for i in range(done, w):
                acc = acc + sp_ref[:, POOL_BUF - i, :]
            done = w
            pooled = jnp.where(lane_p // POOL_GC == wi, acc / float(min(pos + 1, w)), pooled)
        np_ref[:, 0:POOL_BUF - 1, :] = sp_ref[:, 1:POOL_BUF, :]
        np_ref[:, POOL_BUF - 1, :] = bval
        dpool = (pooled - bval).astype(BF16)
        yb = (jnp.dot(dpool, pw_ref[...], preferred_element_type=F32) * psc_ref[...]
              * _silu(proj[:, OFF_B_GATE:OFF_B_GATE + C_POOL]))
        mixab_s[:, 0:C_CONV] = ya
        mixab_s[:, C_CONV:C_CONV + C_POOL] = yb
        cg_s[...] = _silu(proj[:, OFF_C_GATE:OFF_C_GATE + C_ATT])

        q = proj[:, OFF_Q:OFF_Q + C_ATT]
        k = proj[:, OFF_K:OFF_K + C_ATT]
        qn = q * lax.rsqrt(jnp.dot((q * q).astype(BF16), mavg_ref[...], preferred_element_type=F32) + EPS) * qg_ref[...]
        kn = k * lax.rsqrt(jnp.dot((k * k).astype(BF16), mavg_ref[...], preferred_element_type=F32) + EPS) * kg_ref[...]
        for s in range(N_ATT_SLAB):
            cols = slice(s * LANES, (s + 1) * LANES)
            qs, ks = qn[:, cols], kn[:, cols]
            qr_s[:, cols] = (qs * cos_ref[...] + _swap_halves(qs) * sin_ref[...]) * (HEAD_DIM ** -0.5)
            kr_s[:, cols] = ks * cos_ref[...] + _swap_halves(ks) * sin_ref[...]
        v_s[...] = proj[:, OFF_V:OFF_V + C_ATT]
        att_s[...] = jnp.zeros((nsmp, C_ATT), F32)


def _sample_start(cond, smp, qr_s, kr_s, v_s, qcol_s, sm_s, sl_s, so_s, kcar_s, vcar_s):
    nsmp = qr_s.shape[0]
    n_pat = float(len(DILATIONS))

    @pl.when(cond)
    def _start():
        mine = lax.broadcasted_iota(jnp.int32, (nsmp, C_ATT), 0) == smp

        def col_tile(ref):
            return _row_to_col_tile(jnp.sum(jnp.where(mine, ref[...], 0.0), axis=0, keepdims=True))

        q_col = col_tile(qr_s)
        k_col = col_tile(kr_s)
        v_col = col_tile(v_s)
        qcol_s[...] = q_col
        kcar_s[...] = k_col
        vcar_s[...] = v_col
        so_s[...] = v_col * n_pat
        for hd in range(N_HEADS):
            hr = slice(hd * HEAD_DIM, (hd + 1) * HEAD_DIM)
            sm_s[hd:hd + 1, :] = jnp.sum(k_col[hr, :] * q_col[hr, :], axis=0, keepdims=True)
        sl_s[...] = jnp.full(sl_s.shape, n_pat, F32)


def _sample_window(lane0, kt_ref, vt_ref, okt_ref, ovt_ref, qcol_s, sm_s, sl_s, so_s, kcar_s, vcar_s, *, win_len):
    blk = kt_ref.shape[1]
    n_tiles = blk // LANES
    cnt = _key_multiplicity(lane0, blk, win_len)
    reach = cnt > 0.0
    last_lane = lax.broadcasted_iota(jnp.int32, (HEAD_DIM, blk), 1) == blk - 1

    for hd in range(N_HEADS):
        hr = slice(hd * HEAD_DIM, (hd + 1) * HEAD_DIM)
        kt = kt_ref[hr, :]
        vt = vt_ref[hr, :]
        s_win = jnp.sum(kt * jnp.concatenate([qcol_s[hr, :]] * n_tiles, axis=1), axis=0, keepdims=True)
        s_win = jnp.where(reach, s_win, NEG)
        m_old = sm_s[hd:hd + 1, :]
        m_new = jnp.maximum(m_old, jnp.max(s_win, axis=-1, keepdims=True))
        w_old = jnp.exp(m_old - m_new)
        p_win = cnt * jnp.exp(s_win - m_new[:, 0:1])
        sm_s[hd:hd + 1, :] = m_new
        sl_s[hd:hd + 1, :] = sl_s[hd:hd + 1, :] * w_old + jnp.sum(p_win, axis=-1, keepdims=True)
        so_s[hr, :] = so_s[hr, :] * w_old + jnp.sum(vt * p_win, axis=-1, keepdims=True)
        k_next = jnp.concatenate([kcar_s[hr, :]] * n_tiles, axis=1)
        v_next = jnp.concatenate([vcar_s[hr, :]] * n_tiles, axis=1)
        okt_ref[hr, :] = jnp.where(last_lane, k_next, pltpu.roll(kt, blk - 1, 1))
        ovt_ref[hr, :] = jnp.where(last_lane, v_next, pltpu.roll(vt, blk - 1, 1))
        kcar_s[hr, :] = jnp.broadcast_to(kt[:, 0:1], (HEAD_DIM, LANES))
        vcar_s[hr, :] = jnp.broadcast_to(vt[:, 0:1], (HEAD_DIM, LANES))


def _sample_done(cond, smp, att_s, sl_s, so_s):
    nsmp = att_s.shape[0]

    @pl.when(cond)
    def _done():
        cols = [so_s[hd * HEAD_DIM:(hd + 1) * HEAD_DIM, :] / sl_s[hd:hd + 1, :] for hd in range(N_HEADS)]
        att_row = jnp.concatenate(cols, axis=0).T[0:1, :]
        mine = lax.broadcasted_iota(jnp.int32, (nsmp, C_ATT), 0) == smp
        att_s[...] = jnp.where(mine, att_row, att_s[...])


def _sample_finish(cond, x_ref, wout_ref, y_ref, att_s, mixab_s, cg_s):
    @pl.when(cond)
    def _finish():
        yc = att_s[...] * cg_s[...]
        mixed = jnp.concatenate([mixab_s[...], yc], axis=-1).astype(BF16)
        y_ref[...] = x_ref[...] + jnp.dot(mixed, wout_ref[...], preferred_element_type=F32)


def _to_channel_major(a):
    depth, bsz, ntok, nh, hd = a.shape
    return jnp.transpose(a, (0, 1, 3, 4, 2)).reshape(depth, bsz, nh * hd, ntok)


def _from_channel_major(a):
    depth, bsz, _, ntok = a.shape
    return jnp.transpose(a.reshape(depth, bsz, N_HEADS, HEAD_DIM, ntok), (0, 1, 4, 2, 3))


def kernel(x_prompt, x_sample, state_conv, state_pool, cache_k_win, cache_v_win, norm_g, w_in, conv_w, conv_b,
           ln_g, ln_b, pool_w, pool_scale, q_norm_g, k_norm_g, w_out):
    depth = w_in.shape[0]
    seq_len = x_prompt.shape[1]
    nsmp = x_sample.shape[0]

    cos_p, sin_p = _rope_tables(jnp.arange(seq_len, dtype=jnp.int32))
    cos_s, sin_s = _rope_tables(jnp.full((1,), PAST_LEN, dtype=jnp.int32))
    mavg = _head_mean_matrix()
    weights = (norm_g[:, None], w_in.astype(BF16), conv_w, conv_b[:, None], ln_g[:, None], ln_b[:, None],
               _pool_block_diag(pool_w).astype(BF16), pool_scale[:, None],
               jnp.tile(q_norm_g, (1, N_HEADS))[:, None], jnp.tile(k_norm_g, (1, N_HEADS))[:, None])
    w_out_b = w_out.astype(BF16)

    kt = _to_channel_major(cache_k_win)
    vt = _to_channel_major(cache_v_win)
    xp = x_prompt
    xs = x_sample.reshape(nsmp, D_MODEL)
    kv = None
    conv_p, pool_p, conv_s, pool_s = [], [], [], []
    for layer in range(depth):
        xp, kp, vp, cst, pst, xs, ncs, nps, ks, vs = _trunk_layer(
            layer, PAST_LEN, xp, xs, state_conv, state_pool, kt, vt, *weights,
            cos_p, sin_p, cos_s, sin_s, mavg, w_out_b, kv_prev=kv)
        kv = (kp, vp, ks, vs)
        conv_p.append(cst)
        pool_p.append(pst)
        conv_s.append(ncs)
        pool_s.append(nps)

    return (xp, xs.reshape(nsmp, 1, D_MODEL), jnp.stack(conv_p), jnp.stack(pool_p),
            _from_channel_major(kv[0]), _from_channel_major(kv[1]),
            jnp.stack(conv_s), jnp.stack(pool_s), _from_channel_major(kv[2]), _from_channel_major(kv[3]))
```

```python
import functools

import jax
import jax.numpy as jnp
import numpy as np
from jax import lax
from jax.experimental import pallas as pl
from jax.experimental.pallas import tpu as pltpu

F32 = jnp.float32
BF16 = jnp.bfloat16

D_MODEL = 1024
C_CONV = 384
C_POOL = 256
C_ATT = 384
HEAD_DIM = 64
N_HEADS = C_ATT // HEAD_DIM
CONV_W = 31
CONV_HALO = CONV_W - 1
POOL_WINDOWS = (2, 4, 8, 16)
POOL_GC = 64
POOL_BUF = 15
DILATIONS = (1, 4, 16)
WINDOW_KEYS = 128
EPS = 1e-6
ROPE_THETA = 10000.0
D_IN = 3 * C_CONV + 2 * C_POOL + 4 * C_ATT
PAST_LEN = 16384
NEG = -1e30

OFF_A_VAL = 0
OFF_A_GLU = OFF_A_VAL + C_CONV
OFF_A_GATE = OFF_A_GLU + C_CONV
OFF_B_VAL = OFF_A_GATE + C_CONV
OFF_B_GATE = OFF_B_VAL + C_POOL
OFF_Q = OFF_B_GATE + C_POOL
OFF_K = OFF_Q + C_ATT
OFF_V = OFF_K + C_ATT
OFF_C_GATE = OFF_V + C_ATT

LANES = 128
N_CONV_SLAB = C_CONV // LANES
N_POOL_SLAB = C_POOL // LANES
N_ATT_SLAB = C_ATT // LANES
VMEM_LIMIT_BYTES = 60 * 1024 * 1024

TQ = 256
RC = 32
U_HALO = 32
B_HALO = 16
QB = WINDOW_KEYS
UNROLL_NEAR = 15
UNROLL_MID = 6
UNROLL_FAR = 8

AB_Q, AB_K, AB_V, AB_O, AB_LSE = range(5)
AB_COUNT = 5


def _sigmoid(x):
    return 0.5 * jnp.tanh(0.5 * x) + 0.5


def _silu(x):
    h = 0.5 * x
    return h * jnp.tanh(h) + h


def _rope_tables(positions):
    half = HEAD_DIM // 2
    inv = ROPE_THETA ** (-jnp.arange(half, dtype=F32) / half)
    ang = positions.astype(F32)[:, None] * inv[None, :]
    cos = jnp.cos(ang)
    sin = jnp.sin(ang)
    cos_h = jnp.concatenate([cos, cos], axis=-1)
    sin_h = jnp.concatenate([-sin, sin], axis=-1)
    reps = LANES // HEAD_DIM
    return jnp.tile(cos_h, (1, reps)), jnp.tile(sin_h, (1, reps))


def _head_mean_matrix():
    idx = np.arange(C_ATT) // HEAD_DIM
    return jnp.asarray((idx[:, None] == idx[None, :]).astype(np.float32) / HEAD_DIM, dtype=BF16)


def _pool_block_diag(pool_w):
    out = jnp.zeros((pool_w.shape[0], C_POOL, C_POOL), pool_w.dtype)
    for g in range(len(POOL_WINDOWS)):
        out = out.at[:, g * POOL_GC:(g + 1) * POOL_GC, g * POOL_GC:(g + 1) * POOL_GC].set(pool_w[:, g])
    return out


def _swap_halves(x):
    lane = lax.broadcasted_iota(jnp.int32, x.shape, 1)
    first_half = (lane % HEAD_DIM) < (HEAD_DIM // 2)
    return jnp.where(first_half, pltpu.roll(x, LANES - HEAD_DIM // 2, 1), pltpu.roll(x, HEAD_DIM // 2, 1))


def _pool_means(loads, pos):
    lane = lax.broadcasted_iota(jnp.int32, loads(0, 0).shape, 1)
    lo = lane < POOL_GC
    posf = (pos + 1).astype(F32)
    outs = []
    for slab in range(N_POOL_SLAB):
        w_lo, w_hi = POOL_WINDOWS[2 * slab], POOL_WINDOWS[2 * slab + 1]
        cur = loads(0, slab)
        s = cur
        for i in range(1, w_lo):
            s = s + loads(i, slab)
        s_lo = s
        for i in range(w_lo, w_hi):
            s = s + loads(i, slab)
        s_hi = s
        cnt_lo = jnp.minimum(posf, float(w_lo))
        cnt_hi = jnp.minimum(posf, float(w_hi))
        pooled = jnp.where(lo, s_lo / cnt_lo, s_hi / cnt_hi)
        outs.append(pooled - cur)
    return outs


N_LAYER_INPUTS = 22


def _layer_kernel(x_ref, ng_ref, win_ref, cw_ref, cb_ref, lng_ref, lnb_ref, pw_ref, psc_ref,
                  qg_ref, kg_ref, cos_ref, sin_ref, mavg_ref, wout_ref,
                  xs_ref, sc_ref, sp_ref, kt_ref, vt_ref, coss_ref, sins_ref,
                  y_ref, ko_ref, vo_ref, cst_ref, pst_ref,
                  ys_ref, ncs_ref, nps_ref, okt_ref, ovt_ref,
                  h_s, proj, u_buf, b_buf, sq_s, msq_s, d_s, yb_s,
                  st_q, st_k, st_v, ab0, ab1, ab2, ab3, ab4, mix,
                  qr_s, kr_s, v_s, att_s, mixab_s, cgs_s, qcol_s, sm_s, sl_s, so_s, kcar_s, vcar_s,
                  *, seq_len, win_len, pos):
    ab = (ab0, ab1, ab2, ab3, ab4)
    nt = seq_len // TQ
    b = pl.program_id(0)
    t = pl.program_id(1)
    n_batch = pl.num_programs(0)
    blocks_per_sample = win_len // kt_ref.shape[1]

    step = b * nt + jnp.minimum(t, nt - 1)
    part = step % blocks_per_sample
    smp = step // blocks_per_sample

    def sample_start():
        _sample_start(part == 0, smp, qr_s, kr_s, v_s, qcol_s, sm_s, sl_s, so_s, kcar_s, vcar_s)

    def sample_window():
        _sample_window((blocks_per_sample - 1 - part) * kt_ref.shape[1], kt_ref, vt_ref, okt_ref, ovt_ref,
                       qcol_s, sm_s, sl_s, so_s, kcar_s, vcar_s, win_len=win_len)

    def sample_done():
        _sample_done(part == blocks_per_sample - 1, smp, att_s, sl_s, so_s)

    @pl.when(t < nt)
    def _phase1():
        row0 = pl.multiple_of(t * TQ, TQ)

        _sample_prepare(jnp.logical_and(b == 0, t == 0), xs_ref, sc_ref, sp_ref,
                        ng_ref, win_ref, cw_ref, cb_ref, lng_ref, lnb_ref, pw_ref, psc_ref, qg_ref, kg_ref,
                        coss_ref, sins_ref, mavg_ref, ncs_ref, nps_ref,
                        qr_s, kr_s, v_s, att_s, mixab_s, cgs_s, pos=pos)

        @pl.when(t == 0)
        def _zero_halo():
            u_buf[:, 0:U_HALO, :] = jnp.zeros((N_CONV_SLAB, U_HALO, LANES), F32)
            b_buf[:, 0:B_HALO, :] = jnp.zeros((N_POOL_SLAB, B_HALO, LANES), F32)

        sample_start()
        sample_window()

        def norm_chunk(i, c):
            r = pl.multiple_of(i * RC, RC)
            x = x_ref[pl.ds(r, RC), :]
            ms = jnp.mean(x * x, axis=-1, keepdims=True)
            h_s[pl.ds(r, RC), :] = (x * lax.rsqrt(ms + EPS) * ng_ref[...]).astype(BF16)
            return c

        lax.fori_loop(0, TQ // RC, norm_chunk, 0, unroll=True)
        proj[...] = jnp.dot(h_s[...], win_ref[...], preferred_element_type=F32)

        def split_chunk(i, c):
            r = pl.multiple_of(i * RC, RC)
            rows = pl.ds(r, RC)
            grow = pl.ds(pl.multiple_of(row0 + r, RC), RC)
            for s in range(N_CONV_SLAB):
                cols = slice(s * LANES, (s + 1) * LANES)
                a_val = proj[rows, OFF_A_VAL + s * LANES:OFF_A_VAL + (s + 1) * LANES]
                a_glu = proj[rows, OFF_A_GLU + s * LANES:OFF_A_GLU + (s + 1) * LANES]
                u_buf[s, pl.ds(U_HALO + r, RC), :] = a_val * _sigmoid(a_glu)
                q = proj[rows, OFF_Q + s * LANES:OFF_Q + (s + 1) * LANES]
                k = proj[rows, OFF_K + s * LANES:OFF_K + (s + 1) * LANES]
                sq_s[rows, cols] = (q * q).astype(BF16)
                sq_s[rows, C_ATT + s * LANES:C_ATT + (s + 1) * LANES] = (k * k).astype(BF16)
                st_v[s, rows, :] = proj[rows, OFF_V + s * LANES:OFF_V + (s + 1) * LANES]
                mix[grow, C_CONV + C_POOL + s * LANES:C_CONV + C_POOL + (s + 1) * LANES] = _silu(
                    proj[rows, OFF_C_GATE + s * LANES:OFF_C_GATE + (s + 1) * LANES]).astype(BF16)
            for s in range(N_POOL_SLAB):
                b_buf[s, pl.ds(B_HALO + r, RC), :] = proj[rows, OFF_B_VAL + s * LANES:OFF_B_VAL + (s + 1) * LANES]
            pos = row0 + r + lax.broadcasted_iota(jnp.int32, (RC, 1), 0)
            dl = _pool_means(lambda sh, s: b_buf[s, pl.ds(r + B_HALO - sh, RC), :], pos)
            for s in range(N_POOL_SLAB):
                d_s[rows, s * LANES:(s + 1) * LANES] = dl[s].astype(BF16)
            return c

        lax.fori_loop(0, TQ // RC, split_chunk, 0, unroll=True)
        msq_s[:, 0:C_ATT] = jnp.dot(sq_s[:, 0:C_ATT], mavg_ref[...], preferred_element_type=F32)
        msq_s[:, C_ATT:2 * C_ATT] = jnp.dot(sq_s[:, C_ATT:2 * C_ATT], mavg_ref[...], preferred_element_type=F32)
        yb_s[...] = jnp.dot(d_s[...], pw_ref[...], preferred_element_type=F32)

        def mixer_chunk(i, c):
            r = pl.multiple_of(i * RC, RC)
            rows = pl.ds(r, RC)
            grow = pl.ds(pl.multiple_of(row0 + r, RC), RC)
            conv = []
            for s in range(N_CONV_SLAB):
                cols = slice(s * LANES, (s + 1) * LANES)
                acc = jnp.zeros((RC, LANES), F32) + cb_ref[:, cols]
                for w in range(CONV_W):
                    acc = acc + u_buf[s, pl.ds(r + (U_HALO - CONV_HALO) + w, RC), :] * cw_ref[w:w + 1, cols]
                conv.append(acc)
            mu = jnp.sum(conv[0] + conv[1] + conv[2], axis=-1, keepdims=True) * (1.0 / C_CONV)
            cen = [cv - mu for cv in conv]
            var = jnp.sum(cen[0] * cen[0] + cen[1] * cen[1] + cen[2] * cen[2], axis=-1, keepdims=True) * (1.0 / C_CONV)
            rstd = lax.rsqrt(var + EPS)
            for s in range(N_CONV_SLAB):
                cols = slice(s * LANES, (s + 1) * LANES)
                ln = cen[s] * rstd * lng_ref[:, cols] + lnb_ref[:, cols]
                gate = proj[rows, OFF_A_GATE + s * LANES:OFF_A_GATE + (s + 1) * LANES]
                mix[grow, cols] = (_silu(ln) * _silu(gate)).astype(BF16)
            for s in range(N_POOL_SLAB):
                cols = slice(s * LANES, (s + 1) * LANES)
                gate = proj[rows, OFF_B_GATE + s * LANES:OFF_B_GATE + (s + 1) * LANES]
                yb = yb_s[rows, cols] * psc_ref[:, cols] * _silu(gate)
                mix[grow, C_CONV + s * LANES:C_CONV + (s + 1) * LANES] = yb.astype(BF16)
            cos = cos_ref[rows, :]
            sin = sin_ref[rows, :]
            for s in range(N_ATT_SLAB):
                cols = slice(s * LANES, (s + 1) * LANES)
                q = proj[rows, OFF_Q + s * LANES:OFF_Q + (s + 1) * LANES]
                qn = q * lax.rsqrt(msq_s[rows, cols] + EPS) * qg_ref[:, cols]
                qr = qn * cos + _swap_halves(qn) * sin
                st_q[s, rows, :] = qr * (HEAD_DIM ** -0.5)
                k = proj[rows, OFF_K + s * LANES:OFF_K + (s + 1) * LANES]
                kn = k * lax.rsqrt(msq_s[rows, C_ATT + s * LANES:C_ATT + (s + 1) * LANES] + EPS) * kg_ref[:, cols]
                st_k[s, rows, :] = kn * cos + _swap_halves(kn) * sin
            return c

        lax.fori_loop(0, TQ // RC, mixer_chunk, 0, unroll=True)

        d_mid = DILATIONS[1]
        per_stream = TQ // d_mid
        for s in range(N_ATT_SLAB):
            ko_ref[s * LANES:(s + 1) * LANES, :] = st_k[s].T
            vo_ref[s * LANES:(s + 1) * LANES, :] = st_v[s].T
            for st, dst in ((st_q, AB_Q), (st_k, AB_K), (st_v, AB_V)):
                for r in range(d_mid):
                    dst_rows = pl.ds(pl.multiple_of(r * (seq_len // d_mid) + t * per_stream, per_stream), per_stream)
                    ab[dst][s, dst_rows, :] = st[s, pl.ds(r, per_stream, stride=d_mid), :]

        @pl.when(t == nt - 1)
        def _write_state():
            for s in range(N_CONV_SLAB):
                cst_ref[:, s * LANES:(s + 1) * LANES] = u_buf[s, TQ + U_HALO - CONV_HALO:TQ + U_HALO, :]
            for s in range(N_POOL_SLAB):
                pst_ref[:, s * LANES:(s + 1) * LANES] = b_buf[s, TQ + B_HALO - POOL_BUF:TQ + B_HALO, :]

        u_buf[:, 0:U_HALO, :] = u_buf[:, TQ:TQ + U_HALO, :]
        b_buf[:, 0:B_HALO, :] = b_buf[:, TQ:TQ + B_HALO, :]
        sample_done()

    @pl.when(t == nt - 1)
    def _attention():
        d_mid, d_far = DILATIONS[1], DILATIONS[2]
        ratio = d_far // d_mid
        stream_len = seq_len // d_mid
        piece = QB // d_mid
        lane = lax.broadcasted_iota(jnp.int32, (QB, LANES), 1)
        rowi = lax.broadcasted_iota(jnp.int32, (QB, LANES), 0)
        lo = lane < HEAD_DIM

        def masks(key_pos, query_pos):
            cur = key_pos <= query_pos
            prev = key_pos >= query_pos
            cur2 = jnp.concatenate([cur, cur], axis=0)
            return cur2, jnp.concatenate([jnp.concatenate([prev, prev], axis=0), cur2], axis=1)

        cur_ok2, prev_cur_ok2 = masks(lane, rowi)
        ncur_ok2, nprev_cur_ok2 = masks(d_mid * (lane % piece) + lane // piece, d_mid * (rowi % piece) + rowi // piece)

        def attend(q, keys, vals, mask):
            qa = jnp.where(lo, q, 0.0).astype(BF16)
            qb = jnp.where(lo, 0.0, q).astype(BF16)
            q2 = jnp.concatenate([qa, qb], axis=0)
            sc = lax.dot_general(q2, keys.astype(BF16), (((1,), (1,)), ((), ())), preferred_element_type=F32)
            sc = jnp.where(mask, sc, NEG)
            m = jnp.max(sc, axis=-1, keepdims=True)
            p = jnp.exp(sc - m).astype(BF16)
            v1 = jnp.concatenate([vals.astype(BF16), jnp.ones(vals.shape, BF16)], axis=1)
            ol = jnp.dot(p, v1, preferred_element_type=F32)
            o_u = jnp.where(lo, ol[0:QB, 0:LANES], ol[QB:2 * QB, 0:LANES])
            l_u = jnp.where(lo, ol[0:QB, LANES:2 * LANES], ol[QB:2 * QB, LANES:2 * LANES])
            m_u = jnp.where(lo, m[0:QB], m[QB:2 * QB])
            return o_u, m_u, l_u

        def load(ref, s, pieces):
            tiles = [ref[s, p, :] for p in pieces]
            return tiles[0] if len(tiles) == 1 else jnp.concatenate(tiles, axis=0)

        def save(ref, s, pieces, val):
            n = val.shape[0] // len(pieces)
            for j, p in enumerate(pieces):
                ref[s, p, :] = val[j * n:(j + 1) * n]

        def block(rows, krows, mask, merge):
            outs = []
            for s in range(N_ATT_SLAB):
                o_u, m_u, l_u = attend(load(ab[AB_Q], s, rows), load(ab[AB_K], s, krows), load(ab[AB_V], s, krows),
                                       mask)
                o_n = o_u / l_u
                lse_n = m_u + jnp.log(l_u)
                if merge:
                    lse_old = load(ab[AB_LSE], s, rows)
                    m_new = jnp.maximum(lse_old, lse_n)
                    w_old = jnp.exp(lse_old - m_new)
                    w_u = jnp.exp(lse_n - m_new)
                    den = w_old + w_u
                    o_n = (load(ab[AB_O], s, rows) * w_old + o_n * w_u) / den
                    lse_n = m_new + jnp.log(den)
                outs.append((o_n, lse_n))
            return outs

        def store(rows, outs, with_lse=True):
            for s, (o_n, lse_n) in enumerate(outs):
                save(ab[AB_O], s, rows, o_n)
                if with_lse:
                    save(ab[AB_LSE], s, rows, lse_n)

        def token_block(base):
            return [pl.ds(r * stream_len + base, piece) for r in range(d_mid)]

        store(token_block(0), block(token_block(0), token_block(0), ncur_ok2, merge=False))

        def near_unit(i, c):
            base = pl.multiple_of(i * piece, piece)
            rows = token_block(base)
            store(rows, block(rows, token_block(base - piece) + rows, nprev_cur_ok2, merge=False))
            return c

        lax.fori_loop(1, seq_len // QB, near_unit, 0, unroll=UNROLL_NEAR)

        def mid_first(ph, c):
            rows = [pl.ds(pl.multiple_of(ph * stream_len, QB), QB)]
            store(rows, block(rows, rows, cur_ok2, merge=True))
            return c

        lax.fori_loop(0, d_mid, mid_first, 0, unroll=2)
        later_blocks = stream_len // QB - 1

        def mid_unit(i, c):
            start = pl.multiple_of((i // later_blocks) * stream_len + (i % later_blocks + 1) * QB, QB)
            rows = [pl.ds(start, QB)]
            store(rows, block(rows, [pl.ds(start - QB, 2 * QB)], prev_cur_ok2, merge=True))
            return c

        lax.fori_loop(0, d_mid * later_blocks, mid_unit, 0, unroll=UNROLL_MID)

        def far_unit(r, c):
            rows = [pl.ds((r % d_mid) * stream_len + r // d_mid, QB, stride=ratio)]
            store(rows, block(rows, rows, cur_ok2, merge=True), with_lse=False)
            return c

        lax.fori_loop(0, d_far, far_unit, 0, unroll=UNROLL_FAR)

    @pl.when(t >= nt)
    def _phase2():
        row0 = pl.multiple_of((t - nt) * TQ, TQ)

        d_mid = DILATIONS[1]
        per_stream = TQ // d_mid
        for s in range(N_ATT_SLAB):
            for r in range(d_mid):
                src_rows = pl.ds(pl.multiple_of(r * (seq_len // d_mid) + (t - nt) * per_stream, per_stream), per_stream)
                st_q[s, pl.ds(r, per_stream, stride=d_mid), :] = ab[AB_O][s, src_rows, :]

        def att_chunk(i, c):
            r = pl.multiple_of(i * RC, RC)
            grow = pl.ds(pl.multiple_of(row0 + r, RC), RC)
            for s in range(N_ATT_SLAB):
                mcols = slice(C_CONV + C_POOL + s * LANES, C_CONV + C_POOL + (s + 1) * LANES)
                yc = st_q[s, pl.ds(r, RC), :] * mix[grow, mcols].astype(F32)
                mix[grow, mcols] = yc.astype(BF16)
            return c

        lax.fori_loop(0, TQ // RC, att_chunk, 0, unroll=True)
        y_ref[...] = x_ref[...] + jnp.dot(mix[pl.ds(row0, TQ), :], wout_ref[...], preferred_element_type=F32)
        _sample_finish(jnp.logical_and(b == n_batch - 1, t == 2 * nt - 1), xs_ref, wout_ref, ys_ref,
                       att_s, mixab_s, cgs_s)


def _layer_kernel_with_alias(*refs, n_alias, **kw):
    _layer_kernel(*refs[:N_LAYER_INPUTS], *refs[N_LAYER_INPUTS + n_alias:], **kw)


def _trunk_layer(layer, pos, x, xs, state_conv, state_pool, kt, vt,
                 norm_g, w_in, conv_w, conv_b, ln_g, ln_b, pool_wbd, pool_scale, qg, kg,
                 cos_t, sin_t, cos_s, sin_s, mavg, w_out, kv_prev=None):
    bsz, seq_len, _ = x.shape
    nsmp = xs.shape[0]
    depth, _, _, win_len = kt.shape
    assert seq_len % TQ == 0 and seq_len == QB * DILATIONS[-1] and DILATIONS[0] == 1
    assert win_len == WINDOW_KEYS * DILATIONS[-1]
    nt = seq_len // TQ
    blocks_per_sample, rem = divmod(bsz * nt, nsmp)
    assert rem == 0 and blocks_per_sample >= 1 and win_len % (blocks_per_sample * LANES) == 0
    win_blk = win_len // blocks_per_sample

    def sample_block(b, t):
        step = b * nt + jnp.minimum(t, nt - 1)
        return (layer, step // blocks_per_sample, 0, blocks_per_sample - 1 - step % blocks_per_sample)

    def const(shape):
        nd = len(shape)
        return pl.BlockSpec(shape, lambda b, t: (0,) * nd)

    def resident(shape):
        nd = len(shape)
        return pl.BlockSpec(shape, lambda b, t: (0,) * nd, pipeline_mode=pl.Buffered(1))

    def per_layer(shape, **kw):
        nd = len(shape)
        return pl.BlockSpec((None,) + shape, lambda b, t: (layer,) + (0,) * nd, **kw)

    in_specs = [
        pl.BlockSpec((None, TQ, D_MODEL), lambda b, t: (b, t % nt, 0)),
        per_layer((1, D_MODEL)),
        per_layer((D_MODEL, D_IN), pipeline_mode=pl.Buffered(1)),
        per_layer((CONV_W, C_CONV)),
        per_layer((1, C_CONV)),
        per_layer((1, C_CONV)),
        per_layer((1, C_CONV)),
        per_layer((C_POOL, C_POOL)),
        per_layer((1, C_POOL)),
        per_layer((1, C_ATT)),
        per_layer((1, C_ATT)),
        pl.BlockSpec((TQ, LANES), lambda b, t: (jnp.minimum(t, nt - 1), 0)),
        pl.BlockSpec((TQ, LANES), lambda b, t: (jnp.minimum(t, nt - 1), 0)),
        const((C_ATT, C_ATT)),
        per_layer((D_MODEL, D_MODEL), pipeline_mode=pl.Buffered(1)),
        const((nsmp, D_MODEL)),
        per_layer((nsmp, CONV_HALO, C_CONV), pipeline_mode=pl.Buffered(1)),
        per_layer((nsmp, POOL_BUF, C_POOL), pipeline_mode=pl.Buffered(1)),
        pl.BlockSpec((None, None, C_ATT, win_blk), sample_block),
        pl.BlockSpec((None, None, C_ATT, win_blk), sample_block),
        const((1, LANES)),
        const((1, LANES)),
    ]
    operands = [x, norm_g, w_in, conv_w, conv_b, ln_g, ln_b, pool_wbd, pool_scale, qg, kg, cos_t, sin_t, mavg, w_out,
                xs, state_conv, state_pool, kt, vt, cos_s, sin_s]
    assert len(operands) == N_LAYER_INPUTS
    kv_spec = pl.BlockSpec((None, None, C_ATT, TQ), lambda b, t: (layer, b, 0, jnp.minimum(t, nt - 1)))
    out_specs = [
        pl.BlockSpec((None, TQ, D_MODEL), lambda b, t: (b, jnp.maximum(t - nt, 0), 0)),
        kv_spec,
        kv_spec,
        pl.BlockSpec((None, CONV_HALO, C_CONV), lambda b, t: (b, 0, 0)),
        pl.BlockSpec((None, POOL_BUF, C_POOL), lambda b, t: (b, 0, 0)),
        const((nsmp, D_MODEL)),
        pl.BlockSpec((nsmp, CONV_HALO, C_CONV), lambda b, t: (0, 0, 0), pipeline_mode=pl.Buffered(1)),
        pl.BlockSpec((nsmp, POOL_BUF, C_POOL), lambda b, t: (0, 0, 0), pipeline_mode=pl.Buffered(1)),
        pl.BlockSpec((None, None, C_ATT, win_blk), sample_block),
        pl.BlockSpec((None, None, C_ATT, win_blk), sample_block),
    ]
    out_shape = [
        jax.ShapeDtypeStruct((bsz, seq_len, D_MODEL), F32),
        jax.ShapeDtypeStruct((depth, bsz, C_ATT, seq_len), F32),
        jax.ShapeDtypeStruct((depth, bsz, C_ATT, seq_len), F32),
        jax.ShapeDtypeStruct((bsz, CONV_HALO, C_CONV), F32),
        jax.ShapeDtypeStruct((bsz, POOL_BUF, C_POOL), F32),
        jax.ShapeDtypeStruct((nsmp, D_MODEL), F32),
        jax.ShapeDtypeStruct((nsmp, CONV_HALO, C_CONV), F32),
        jax.ShapeDtypeStruct((nsmp, POOL_BUF, C_POOL), F32),
        jax.ShapeDtypeStruct((depth, nsmp, C_ATT, win_len), F32),
        jax.ShapeDtypeStruct((depth, nsmp, C_ATT, win_len), F32),
    ]
    static = dict(seq_len=seq_len, win_len=win_len, pos=pos)
    if kv_prev is None:
        kern = functools.partial(_layer_kernel, **static)
        aliases = {}
    else:
        kern = functools.partial(_layer_kernel_with_alias, n_alias=len(kv_prev), **static)
        in_specs += [pl.BlockSpec(memory_space=pl.ANY)] * len(kv_prev)
        operands += list(kv_prev)
        aliases = {N_LAYER_INPUTS + i: o for i, o in enumerate((1, 2, 8, 9))}
    scratch = [
        pltpu.VMEM((TQ, D_MODEL), BF16),
        pltpu.VMEM((TQ, D_IN), F32),
        pltpu.VMEM((N_CONV_SLAB, TQ + U_HALO, LANES), F32),
        pltpu.VMEM((N_POOL_SLAB, TQ + B_HALO, LANES), F32),
        pltpu.VMEM((TQ, 2 * C_ATT), BF16),
        pltpu.VMEM((TQ, 2 * C_ATT), F32),
        pltpu.VMEM((TQ, C_POOL), BF16),
        pltpu.VMEM((TQ, C_POOL), F32),
        *[pltpu.VMEM((N_ATT_SLAB, TQ, LANES), F32) for _ in range(3)],
        *[pltpu.VMEM((N_ATT_SLAB, seq_len, LANES), F32) for _ in range(AB_COUNT)],
        pltpu.VMEM((seq_len, D_MODEL), BF16),
        pltpu.VMEM((nsmp, C_ATT), F32),
        pltpu.VMEM((nsmp, C_ATT), F32),
        pltpu.VMEM((nsmp, C_ATT), F32),
        pltpu.VMEM((nsmp, C_ATT), F32),
        pltpu.VMEM((nsmp, C_CONV + C_POOL), F32),
        pltpu.VMEM((nsmp, C_ATT), F32),
        pltpu.VMEM((C_ATT, LANES), F32),
        pltpu.VMEM((N_HEADS, LANES), F32),
        pltpu.VMEM((N_HEADS, LANES), F32),
        pltpu.VMEM((C_ATT, LANES), F32),
        pltpu.VMEM((C_ATT, LANES), F32),
        pltpu.VMEM((C_ATT, LANES), F32),
    ]
    return pl.pallas_call(
        kern,
        out_shape=out_shape,
        grid=(bsz, 2 * nt),
        in_specs=in_specs,
        out_specs=out_specs,
        scratch_shapes=scratch,
        input_output_aliases=aliases,
        compiler_params=pltpu.CompilerParams(
            dimension_semantics=("arbitrary", "arbitrary"),
            vmem_limit_bytes=VMEM_LIMIT_BYTES,
        ),
        name="trunk_layer",
    )(*operands)


def _row_to_col_tile(row):
    return jnp.broadcast_to(row, (LANES, row.shape[1])).T


def _key_multiplicity(lane0, blk, win_len):
    t = lane0 + lax.broadcasted_iota(jnp.int32, (1, blk), 1)
    delta = win_len - t
    cnt = jnp.zeros((1, blk), F32)
    for d in DILATIONS:
        hit = jnp.logical_and(delta % d == 0, delta <= d * WINDOW_KEYS)
        cnt = cnt + hit.astype(F32)
    return cnt


def _sample_prepare(cond, x_ref, sc_ref, sp_ref,
                    ng_ref, win_ref, cw_ref, cb_ref, lng_ref, lnb_ref, pw_ref, psc_ref, qg_ref, kg_ref,
                    cos_ref, sin_ref, mavg_ref, nc_ref, np_ref,
                    qr_s, kr_s, v_s, att_s, mixab_s, cg_s, *, pos):
    nsmp = x_ref.shape[0]

    @pl.when(cond)
    def _prepare():
        x = x_ref[...]
        ms = jnp.mean(x * x, axis=-1, keepdims=True)
        h = (x * lax.rsqrt(ms + EPS) * ng_ref[...]).astype(BF16)
        proj = jnp.dot(h, win_ref[...], preferred_element_type=F32)

        u = proj[:, OFF_A_VAL:OFF_A_VAL + C_CONV] * _sigmoid(proj[:, OFF_A_GLU:OFF_A_GLU + C_CONV])
        conv = u * cw_ref[CONV_HALO:CONV_W, :] + cb_ref[...]
        for w in range(CONV_HALO):
            conv = conv + sc_ref[:, w, :] * cw_ref[w:w + 1, :]
        nc_ref[:, 0:CONV_HALO - 1, :] = sc_ref[:, 1:CONV_HALO, :]
        nc_ref[:, CONV_HALO - 1, :] = u
        mu = jnp.mean(conv, axis=-1, keepdims=True)
        cen = conv - mu
        var = jnp.mean(cen * cen, axis=-1, keepdims=True)
        ln = cen * lax.rsqrt(var + EPS) * lng_ref[...] + lnb_ref[...]
        ya = _silu(ln) * _silu(proj[:, OFF_A_GATE:OFF_A_GATE + C_CONV])

        bval = proj[:, OFF_B_VAL:OFF_B_VAL + C_POOL]
        lane_p = lax.broadcasted_iota(jnp.int32, (nsmp, C_POOL), 1)
        pooled = jnp.zeros((nsmp, C_POOL), F32)
        acc = bval
        done = 1
        for wi, w in enumerate(POOL_WINDOWS):
            for i in range(done, w):
                acc = acc + sp_ref[:, POOL_BUF - i, :]
            done = w
            pooled = jnp.where(lane_p // POOL_GC == wi, acc / float(min(pos + 1, w)), pooled)
        np_ref[:, 0:POOL_BUF - 1, :] = sp_ref[:, 1:POOL_BUF, :]
        np_ref[:, POOL_BUF - 1, :] = bval
        dpool = (pooled - bval).astype(BF16)
        yb = (jnp.dot(dpool, pw_ref[...], preferred_element_type=F32) * psc_ref[...]
              * _silu(proj[:, OFF_B_GATE:OFF_B_GATE + C_POOL]))
        mixab_s[:, 0:C_CONV] = ya
        mixab_s[:, C_CONV:C_CONV + C_POOL] = yb
        cg_s[...] = _silu(proj[:, OFF_C_GATE:OFF_C_GATE + C_ATT])

        q = proj[:, OFF_Q:OFF_Q + C_ATT]
        k = proj[:, OFF_K:OFF_K + C_ATT]
        qn = q * lax.rsqrt(jnp.dot((q * q).astype(BF16), mavg_ref[...], preferred_element_type=F32) + EPS) * qg_ref[...]
        kn = k * lax.rsqrt(jnp.dot((k * k).astype(BF16), mavg_ref[...], preferred_element_type=F32) + EPS) * kg_ref[...]
        for s in range(N_ATT_SLAB):
            cols = slice(s * LANES, (s + 1) * LANES)
            qs, ks = qn[:, cols], kn[:, cols]
            qr_s[:, cols] = (qs * cos_ref[...] + _swap_halves(qs) * sin_ref[...]) * (HEAD_DIM ** -0.5)
            kr_s[:, cols] = ks * cos_ref[...] + _swap_halves(ks) * sin_ref[...]
        v_s[...] = proj[:, OFF_V:OFF_V + C_ATT]
        att_s[...] = jnp.zeros((nsmp, C_ATT), F32)


def _sample_start(cond, smp, qr_s, kr_s, v_s, qcol_s, sm_s, sl_s, so_s, kcar_s, vcar_s):
    nsmp = qr_s.shape[0]
    n_pat = float(len(DILATIONS))

    @pl.when(cond)
    def _start():
        mine = lax.broadcasted_iota(jnp.int32, (nsmp, C_ATT), 0) == smp

        def col_tile(ref):
            return _row_to_col_tile(jnp.sum(jnp.where(mine, ref[...], 0.0), axis=0, keepdims=True))

        q_col = col_tile(qr_s)
        k_col = col_tile(kr_s)
        v_col = col_tile(v_s)
        qcol_s[...] = q_col
        kcar_s[...] = k_col
        vcar_s[...] = v_col
        so_s[...] = v_col * n_pat
        for hd in range(N_HEADS):
            hr = slice(hd * HEAD_DIM, (hd + 1) * HEAD_DIM)
            sm_s[hd:hd + 1, :] = jnp.sum(k_col[hr, :] * q_col[hr, :], axis=0, keepdims=True)
        sl_s[...] = jnp.full(sl_s.shape, n_pat, F32)


def _sample_window(lane0, kt_ref, vt_ref, okt_ref, ovt_ref, qcol_s, sm_s, sl_s, so_s, kcar_s, vcar_s, *, win_len):
    blk = kt_ref.shape[1]
    n_tiles = blk // LANES
    cnt = _key_multiplicity(lane0, blk, win_len)
    reach = cnt > 0.0
    last_lane = lax.broadcasted_iota(jnp.int32, (HEAD_DIM, blk), 1) == blk - 1

    for hd in range(N_HEADS):
        hr = slice(hd * HEAD_DIM, (hd + 1) * HEAD_DIM)
        kt = kt_ref[hr, :]
        vt = vt_ref[hr, :]
        s_win = jnp.sum(kt * jnp.concatenate([qcol_s[hr, :]] * n_tiles, axis=1), axis=0, keepdims=True)
        s_win = jnp.where(reach, s_win, NEG)
        m_old = sm_s[hd:hd + 1, :]
        m_new = jnp.maximum(m_old, jnp.max(s_win, axis=-1, keepdims=True))
        w_old = jnp.exp(m_old - m_new)
        p_win = cnt * jnp.exp(s_win - m_new[:, 0:1])
        sm_s[hd:hd + 1, :] = m_new
        sl_s[hd:hd + 1, :] = sl_s[hd:hd + 1, :] * w_old + jnp.sum(p_win, axis=-1, keepdims=True)
        so_s[hr, :] = so_s[hr, :] * w_old + jnp.sum(vt * p_win, axis=-1, keepdims=True)
        k_next = jnp.concatenate([kcar_s[hr, :]] * n_tiles, axis=1)
        v_next = jnp.concatenate([vcar_s[hr, :]] * n_tiles, axis=1)
        okt_ref[hr, :] = jnp.where(last_lane, k_next, pltpu.roll(kt, blk - 1, 1))
        ovt_ref[hr, :] = jnp.where(last_lane, v_next, pltpu.roll(vt, blk - 1, 1))
        kcar_s[hr, :] = jnp.broadcast_to(kt[:, 0:1], (HEAD_DIM, LANES))
        vcar_s[hr, :] = jnp.broadcast_to(vt[:, 0:1], (HEAD_DIM, LANES))


def _sample_done(cond, smp, att_s, sl_s, so_s):
    nsmp = att_s.shape[0]

    @pl.when(cond)
    def _done():
        cols = [so_s[hd * HEAD_DIM:(hd + 1) * HEAD_DIM, :] / sl_s[hd:hd + 1, :] for hd in range(N_HEADS)]
        att_row = jnp.concatenate(cols, axis=0).T[0:1, :]
        mine = lax.broadcasted_iota(jnp.int32, (nsmp, C_ATT), 0) == smp
        att_s[...] = jnp.where(mine, att_row, att_s[...])


def _sample_finish(cond, x_ref, wout_ref, y_ref, att_s, mixab_s, cg_s):
    @pl.when(cond)
    def _finish():
        yc = att_s[...] * cg_s[...]
        mixed = jnp.concatenate([mixab_s[...], yc], axis=-1).astype(BF16)
        y_ref[...] = x_ref[...] + jnp.dot(mixed, wout_ref[...], preferred_element_type=F32)


def _to_channel_major(a):
    depth, bsz, ntok, nh, hd = a.shape
    return jnp.transpose(a, (0, 1, 3, 4, 2)).reshape(depth, bsz, nh * hd, ntok)


def _from_channel_major(a):
    depth, bsz, _, ntok = a.shape
    return jnp.transpose(a.reshape(depth, bsz, N_HEADS, HEAD_DIM, ntok), (0, 1, 4, 2, 3))


def kernel(x_prompt, x_sample, state_conv, state_pool, cache_k_win, cache_v_win, norm_g, w_in, conv_w, conv_b,
           ln_g, ln_b, pool_w, pool_scale, q_norm_g, k_norm_g, w_out):
    depth = w_in.shape[0]
    seq_len = x_prompt.shape[1]
    nsmp = x_sample.shape[0]

    cos_p, sin_p = _rope_tables(jnp.arange(seq_len, dtype=jnp.int32))
    cos_s, sin_s = _rope_tables(jnp.full((1,), PAST_LEN, dtype=jnp.int32))
    mavg = _head_mean_matrix()
    weights = (norm_g[:, None], w_in.astype(BF16), conv_w, conv_b[:, None], ln_g[:, None], ln_b[:, None],
               _pool_block_diag(pool_w).astype(BF16), pool_scale[:, None],
               jnp.tile(q_norm_g, (1, N_HEADS))[:, None], jnp.tile(k_norm_g, (1, N_HEADS))[:, None])
    w_out_b = w_out.astype(BF16)

    kt = _to_channel_major(cache_k_win)
    vt = _to_channel_major(cache_v_win)
    xp = x_prompt
    xs = x_sample.reshape(nsmp, D_MODEL)
    kv = None
    conv_p, pool_p, conv_s, pool_s = [], [], [], []
    for layer in range(depth):
        xp, kp, vp, cst, pst, xs, ncs, nps, ks, vs = _trunk_layer(
            layer, PAST_LEN, xp, xs, state_conv, state_pool, kt, vt, *weights,
            cos_p, sin_p, cos_s, sin_s, mavg, w_out_b, kv_prev=kv)
        kv = (kp, vp, ks, vs)
        conv_p.append(cst)
        pool_p.append(pst)
        conv_s.append(ncs)
        pool_s.append(nps)

    return (xp, xs.reshape(nsmp, 1, D_MODEL), jnp.stack(conv_p), jnp.stack(pool_p),
            _from_channel_major(kv[0]), _from_channel_major(kv[1]),
            jnp.stack(conv_s), jnp.stack(pool_s), _from_channel_major(kv[2]), _from_channel_major(kv[3]))
```

```python
import functools

import jax
import jax.numpy as jnp
import numpy as np
from jax import lax
from jax.experimental import pallas as pl
from jax.experimental.pallas import tpu as pltpu

F32 = jnp.float32
BF16 = jnp.bfloat16

D_MODEL = 1024
C_CONV = 384
C_POOL = 256
C_ATT = 384
HEAD_DIM = 64
N_HEADS = C_ATT // HEAD_DIM
CONV_W = 31
CONV_HALO = CONV_W - 1
POOL_WINDOWS = (2, 4, 8, 16)
POOL_GC = 64
POOL_BUF = 15
DILATIONS = (1, 4, 16)
WINDOW_KEYS = 128
EPS = 1e-6
ROPE_THETA = 10000.0
D_IN = 3 * C_CONV + 2 * C_POOL + 4 * C_ATT
PAST_LEN = 16384
NEG = -1e30

OFF_A_VAL = 0
OFF_A_GLU = OFF_A_VAL + C_CONV
OFF_A_GATE = OFF_A_GLU + C_CONV
OFF_B_VAL = OFF_A_GATE + C_CONV
OFF_B_GATE = OFF_B_VAL + C_POOL
OFF_Q = OFF_B_GATE + C_POOL
OFF_K = OFF_Q + C_ATT
OFF_V = OFF_K + C_ATT
OFF_C_GATE = OFF_V + C_ATT

LANES = 128
N_CONV_SLAB = C_CONV // LANES
N_POOL_SLAB = C_POOL // LANES
N_ATT_SLAB = C_ATT // LANES
VMEM_LIMIT_BYTES = 60 * 1024 * 1024

TQ = 256
RC = 32
U_HALO = 32
B_HALO = 16
QB = WINDOW_KEYS
UNROLL_NEAR = 15
UNROLL_MID = 6
UNROLL_FAR = 8

AB_Q, AB_K, AB_V, AB_O, AB_LSE = range(5)
AB_COUNT = 5


def _sigmoid(x):
    return 0.5 * jnp.tanh(0.5 * x) + 0.5


def _silu(x):
    h = 0.5 * x
    return h * jnp.tanh(h) + h


def _rope_tables(positions):
    half = HEAD_DIM // 2
    inv = ROPE_THETA ** (-jnp.arange(half, dtype=F32) / half)
    ang = positions.astype(F32)[:, None] * inv[None, :]
    cos = jnp.cos(ang)
    sin = jnp.sin(ang)
    cos_h = jnp.concatenate([cos, cos], axis=-1)
    sin_h = jnp.concatenate([-sin, sin], axis=-1)
    reps = LANES // HEAD_DIM
    return jnp.tile(cos_h, (1, reps)), jnp.tile(sin_h, (1, reps))


def _head_mean_matrix():
    idx = np.arange(C_ATT) // HEAD_DIM
    return jnp.asarray((idx[:, None] == idx[None, :]).astype(np.float32) / HEAD_DIM, dtype=BF16)


def _pool_block_diag(pool_w):
    out = jnp.zeros((pool_w.shape[0], C_POOL, C_POOL), pool_w.dtype)
    for g in range(len(POOL_WINDOWS)):
        out = out.at[:, g * POOL_GC:(g + 1) * POOL_GC, g * POOL_GC:(g + 1) * POOL_GC].set(pool_w[:, g])
    return out


def _swap_halves(x):
    lane = lax.broadcasted_iota(jnp.int32, x.shape, 1)
    first_half = (lane % HEAD_DIM) < (HEAD_DIM // 2)
    return jnp.where(first_half, pltpu.roll(x, LANES - HEAD_DIM // 2, 1), pltpu.roll(x, HEAD_DIM // 2, 1))


def _pool_means(loads, pos):
    lane = lax.broadcasted_iota(jnp.int32, loads(0, 0).shape, 1)
    lo = lane < POOL_GC
    posf = (pos + 1).astype(F32)
    outs = []
    for slab in range(N_POOL_SLAB):
        w_lo, w_hi = POOL_WINDOWS[2 * slab], POOL_WINDOWS[2 * slab + 1]
        cur = loads(0, slab)
        s = cur
        for i in range(1, w_lo):
            s = s + loads(i, slab)
        s_lo = s
        for i in range(w_lo, w_hi):
            s = s + loads(i, slab)
        s_hi = s
        cnt_lo = jnp.minimum(posf, float(w_lo))
        cnt_hi = jnp.minimum(posf, float(w_hi))
        pooled = jnp.where(lo, s_lo / cnt_lo, s_hi / cnt_hi)
        outs.append(pooled - cur)
    return outs


N_LAYER_INPUTS = 24


def _layer_kernel(x_ref, x2_ref, ng_ref, win_ref, cw_ref, cb_ref, lng_ref, lnb_ref, pw_ref, psc_ref,
                  qg_ref, kg_ref, cos_ref, sin_ref, mavg_ref, wout_ref,
                  xs_ref, qr_s, kr_s, v_s, mixab_s, cgs_s, kt_ref, vt_ref,
                  y_ref, ko_ref, vo_ref, cst_ref, pst_ref,
                  ys_ref, okt_ref, ovt_ref,
                  h_s, proj, u_buf, b_buf,
                  st_q, st_k, st_v, st_att, ab0, ab1, ab2, ab3, ab4, mix,
                  att_s, qcol_s, sm_s, sl_s, so_s, kcar_s, vcar_s,
                  *, seq_len, win_len):
    ab = (ab0, ab1, ab2, ab3, ab4)
    nt = seq_len // TQ
    b = pl.program_id(0)
    t = pl.program_id(1)
    n_batch = pl.num_programs(0) - 1
    blocks_per_sample = win_len // kt_ref.shape[1]
    row0 = pl.multiple_of(t * TQ, TQ)
    sq_s = h_s.at[:, 0:2 * C_ATT]
    d_s = h_s.at[:, 2 * C_ATT:2 * C_ATT + C_POOL]
    msq_s = proj.at[:, OFF_A_VAL:OFF_A_VAL + 2 * C_ATT]
    yb_s = proj.at[:, OFF_B_VAL:OFF_B_VAL + C_POOL]

    step = b * nt + t
    part = step % blocks_per_sample
    smp = step // blocks_per_sample

    def sample_start():
        _sample_start(part == 0, smp, qr_s, kr_s, v_s, qcol_s, sm_s, sl_s, so_s, kcar_s, vcar_s)

    def sample_window():
        _sample_window((blocks_per_sample - 1 - part) * kt_ref.shape[1], kt_ref, vt_ref, okt_ref, ovt_ref,
                       qcol_s, sm_s, sl_s, so_s, kcar_s, vcar_s, win_len=win_len)

    def sample_done():
        _sample_done(part == blocks_per_sample - 1, smp, att_s, sl_s, so_s)

    def phase2():
        d_mid = DILATIONS[1]
        per_stream = TQ // d_mid
        for s in range(N_ATT_SLAB):
            for r in range(d_mid):
                src_rows = pl.ds(pl.multiple_of(r * (seq_len // d_mid) + t * per_stream, per_stream), per_stream)
                st_att[s, pl.ds(r, per_stream, stride=d_mid), :] = ab[AB_O][s, src_rows, :]

        def att_chunk(i, c):
            r = pl.multiple_of(i * RC, RC)
            grow = pl.ds(pl.multiple_of(row0 + r, RC), RC)
            for s in range(N_ATT_SLAB):
                mcols = slice(C_CONV + C_POOL + s * LANES, C_CONV + C_POOL + (s + 1) * LANES)
                yc = st_att[s, pl.ds(r, RC), :] * mix[grow, mcols].astype(F32)
                mix[grow, mcols] = yc.astype(BF16)
            return c

        lax.fori_loop(0, TQ // RC, att_chunk, 0, unroll=True)
        y_ref[...] = x2_ref[...] + jnp.dot(mix[pl.ds(row0, TQ), :], wout_ref[...], preferred_element_type=F32)

    def phase1(with_phase2):
        @pl.when(t == 0)
        def _zero_halo():
            u_buf[:, 0:U_HALO, :] = jnp.zeros((N_CONV_SLAB, U_HALO, LANES), F32)
            b_buf[:, 0:B_HALO, :] = jnp.zeros((N_POOL_SLAB, B_HALO, LANES), F32)
            if not with_phase2:
                att_s[...] = jnp.zeros(att_s.shape, F32)

        sample_start()
        if with_phase2:
            phase2()
        sample_window()

        def norm_chunk(i, c):
            r = pl.multiple_of(i * RC, RC)
            x = x_ref[pl.ds(r, RC), :]
            ms = jnp.mean(x * x, axis=-1, keepdims=True)
            h_s[pl.ds(r, RC), :] = (x * lax.rsqrt(ms + EPS) * ng_ref[...]).astype(BF16)
            return c

        lax.fori_loop(0, TQ // RC, norm_chunk, 0, unroll=True)
        proj[...] = jnp.dot(h_s[...], win_ref[...], preferred_element_type=F32)

        def split_chunk(i, c):
            r = pl.multiple_of(i * RC, RC)
            rows = pl.ds(r, RC)
            grow = pl.ds(pl.multiple_of(row0 + r, RC), RC)
            for s in range(N_CONV_SLAB):
                cols = slice(s * LANES, (s + 1) * LANES)
                a_val = proj[rows, OFF_A_VAL + s * LANES:OFF_A_VAL + (s + 1) * LANES]
                a_glu = proj[rows, OFF_A_GLU + s * LANES:OFF_A_GLU + (s + 1) * LANES]
                u_buf[s, pl.ds(U_HALO + r, RC), :] = a_val * _sigmoid(a_glu)
                q = proj[rows, OFF_Q + s * LANES:OFF_Q + (s + 1) * LANES]
                k = proj[rows, OFF_K + s * LANES:OFF_K + (s + 1) * LANES]
                sq_s[rows, cols] = (q * q).astype(BF16)
                sq_s[rows, C_ATT + s * LANES:C_ATT + (s + 1) * LANES] = (k * k).astype(BF16)
                st_v[s, rows, :] = proj[rows, OFF_V + s * LANES:OFF_V + (s + 1) * LANES]
                mix[grow, C_CONV + C_POOL + s * LANES:C_CONV + C_POOL + (s + 1) * LANES] = _silu(
                    proj[rows, OFF_C_GATE + s * LANES:OFF_C_GATE + (s + 1) * LANES]).astype(BF16)
            for s in range(N_POOL_SLAB):
                b_buf[s, pl.ds(B_HALO + r, RC), :] = proj[rows, OFF_B_VAL + s * LANES:OFF_B_VAL + (s + 1) * LANES]
            pos = row0 + r + lax.broadcasted_iota(jnp.int32, (RC, 1), 0)
            dl = _pool_means(lambda sh, s: b_buf[s, pl.ds(r + B_HALO - sh, RC), :], pos)
            for s in range(N_POOL_SLAB):
                d_s[rows, s * LANES:(s + 1) * LANES] = dl[s].astype(BF16)
            return c

        lax.fori_loop(0, TQ // RC, split_chunk, 0, unroll=True)
        msq_s[:, 0:C_ATT] = jnp.dot(sq_s[:, 0:C_ATT], mavg_ref[...], preferred_element_type=F32)
        msq_s[:, C_ATT:2 * C_ATT] = jnp.dot(sq_s[:, C_ATT:2 * C_ATT], mavg_ref[...], preferred_element_type=F32)
        yb_s[...] = jnp.dot(d_s[...], pw_ref[...], preferred_element_type=F32)

        def mixer_chunk(i, c):
            r = pl.multiple_of(i * RC, RC)
            rows = pl.ds(r, RC)
            grow = pl.ds(pl.multiple_of(row0 + r, RC), RC)
            conv = []
            for s in range(N_CONV_SLAB):
                cols = slice(s * LANES, (s + 1) * LANES)
                acc = jnp.zeros((RC, LANES), F32) + cb_ref[:, cols]
                for w in range(CONV_W):
                    acc = acc + u_buf[s, pl.ds(r + (U_HALO - CONV_HALO) + w, RC), :] * cw_ref[w:w + 1, cols]
                conv.append(acc)
            mu = jnp.sum(conv[0] + conv[1] + conv[2], axis=-1, keepdims=True) * (1.0 / C_CONV)
            cen = [cv - mu for cv in conv]
            var = jnp.sum(cen[0] * cen[0] + cen[1] * cen[1] + cen[2] * cen[2], axis=-1, keepdims=True) * (1.0 / C_CONV)
            rstd = lax.rsqrt(var + EPS)
            for s in range(N_CONV_SLAB):
                cols = slice(s * LANES, (s + 1) * LANES)
                ln = cen[s] * rstd * lng_ref[:, cols] + lnb_ref[:, cols]
                gate = proj[rows, OFF_A_GATE + s * LANES:OFF_A_GATE + (s + 1) * LANES]
                mix[grow, cols] = (_silu(ln) * _silu(gate)).astype(BF16)
            for s in range(N_POOL_SLAB):
                cols = slice(s * LANES, (s + 1) * LANES)
                gate = proj[rows, OFF_B_GATE + s * LANES:OFF_B_GATE + (s + 1) * LANES]
                yb = yb_s[rows, cols] * psc_ref[:, cols] * _silu(gate)
                mix[grow, C_CONV + s * LANES:C_CONV + (s + 1) * LANES] = yb.astype(BF16)
            cos = cos_ref[rows, :]
            sin = sin_ref[rows, :]
            for s in range(N_ATT_SLAB):
                cols = slice(s * LANES, (s + 1) * LANES)
                q = proj[rows, OFF_Q + s * LANES:OFF_Q + (s + 1) * LANES]
                qn = q * lax.rsqrt(msq_s[rows, cols] + EPS) * qg_ref[:, cols]
                qr = qn * cos + _swap_halves(qn) * sin
                st_q[s, rows, :] = qr * (HEAD_DIM ** -0.5)
                k = proj[rows, OFF_K + s * LANES:OFF_K + (s + 1) * LANES]
                kn = k * lax.rsqrt(msq_s[rows, C_ATT + s * LANES:C_ATT + (s + 1) * LANES] + EPS) * kg_ref[:, cols]
                st_k[s, rows, :] = kn * cos + _swap_halves(kn) * sin
            return c

        lax.fori_loop(0, TQ // RC, mixer_chunk, 0, unroll=True)

        d_mid = DILATIONS[1]
        per_stream = TQ // d_mid
        for s in range(N_ATT_SLAB):
            ko_ref[s * LANES:(s + 1) * LANES, :] = st_k[s].T
            vo_ref[s * LANES:(s + 1) * LANES, :] = st_v[s].T
            for st, dst in ((st_q, AB_Q), (st_k, AB_K), (st_v, AB_V)):
                for r in range(d_mid):
                    dst_rows = pl.ds(pl.multiple_of(r * (seq_len // d_mid) + t * per_stream, per_stream), per_stream)
                    ab[dst][s, dst_rows, :] = st[s, pl.ds(r, per_stream, stride=d_mid), :]

        @pl.when(t == nt - 1)
        def _write_state():
            for s in range(N_CONV_SLAB):
                cst_ref[:, s * LANES:(s + 1) * LANES] = u_buf[s, TQ + U_HALO - CONV_HALO:TQ + U_HALO, :]
            for s in range(N_POOL_SLAB):
                pst_ref[:, s * LANES:(s + 1) * LANES] = b_buf[s, TQ + B_HALO - POOL_BUF:TQ + B_HALO, :]

        u_buf[:, 0:U_HALO, :] = u_buf[:, TQ:TQ + U_HALO, :]
        b_buf[:, 0:B_HALO, :] = b_buf[:, TQ:TQ + B_HALO, :]
        sample_done()

    def attention():
        d_mid, d_far = DILATIONS[1], DILATIONS[2]
        ratio = d_far // d_mid
        stream_len = seq_len // d_mid
        piece = QB // d_mid
        lane = lax.broadcasted_iota(jnp.int32, (QB, LANES), 1)
        rowi = lax.broadcasted_iota(jnp.int32, (QB, LANES), 0)
        lo = lane < HEAD_DIM

        def masks(key_pos, query_pos):
            cur = key_pos <= query_pos
            prev = key_pos >= query_pos
            cur2 = jnp.concatenate([cur, cur], axis=0)
            return cur2, jnp.concatenate([jnp.concatenate([prev, prev], axis=0), cur2], axis=1)

        cur_ok2, prev_cur_ok2 = masks(lane, rowi)
        ncur_ok2, nprev_cur_ok2 = masks(d_mid * (lane % piece) + lane // piece, d_mid * (rowi % piece) + rowi // piece)

        def attend(q, keys, vals, mask):
            qa = jnp.where(lo, q, 0.0).astype(BF16)
            qb = jnp.where(lo, 0.0, q).astype(BF16)
            q2 = jnp.concatenate([qa, qb], axis=0)
            sc = lax.dot_general(q2, keys.astype(BF16), (((1,), (1,)), ((), ())), preferred_element_type=F32)
            sc = jnp.where(mask, sc, NEG)
            m = jnp.max(sc, axis=-1, keepdims=True)
            p = jnp.exp(sc - m).astype(BF16)
            v1 = jnp.concatenate([vals.astype(BF16), jnp.ones(vals.shape, BF16)], axis=1)
            ol = jnp.dot(p, v1, preferred_element_type=F32)
            o_u = jnp.where(lo, ol[0:QB, 0:LANES], ol[QB:2 * QB, 0:LANES])
            l_u = jnp.where(lo, ol[0:QB, LANES:2 * LANES], ol[QB:2 * QB, LANES:2 * LANES])
            m_u = jnp.where(lo, m[0:QB], m[QB:2 * QB])
            return o_u, m_u, l_u

        def load(ref, s, pieces):
            tiles = [ref[s, p, :] for p in pieces]
            return tiles[0] if len(tiles) == 1 else jnp.concatenate(tiles, axis=0)

        def save(ref, s, pieces, val):
            n = val.shape[0] // len(pieces)
            for j, p in enumerate(pieces):
                ref[s, p, :] = val[j * n:(j + 1) * n]

        def block(rows, krows, mask, merge):
            outs = []
            for s in range(N_ATT_SLAB):
                o_u, m_u, l_u = attend(load(ab[AB_Q], s, rows), load(ab[AB_K], s, krows), load(ab[AB_V], s, krows),
                                       mask)
                o_n = o_u / l_u
                lse_n = m_u + jnp.log(l_u)
                if merge:
                    lse_old = load(ab[AB_LSE], s, rows)
                    m_new = jnp.maximum(lse_old, lse_n)
                    w_old = jnp.exp(lse_old - m_new)
                    w_u = jnp.exp(lse_n - m_new)
                    den = w_old + w_u
                    o_n = (load(ab[AB_O], s, rows) * w_old + o_n * w_u) / den
                    lse_n = m_new + jnp.log(den)
                outs.append((o_n, lse_n))
            return outs

        def store(rows, outs, with_lse=True):
            for s, (o_n, lse_n) in enumerate(outs):
                save(ab[AB_O], s, rows, o_n)
                if with_lse:
                    save(ab[AB_LSE], s, rows, lse_n)

        def token_block(base):
            return [pl.ds(r * stream_len + base, piece) for r in range(d_mid)]

        store(token_block(0), block(token_block(0), token_block(0), ncur_ok2, merge=False))

        def near_unit(i, c):
            base = pl.multiple_of(i * piece, piece)
            rows = token_block(base)
            store(rows, block(rows, token_block(base - piece) + rows, nprev_cur_ok2, merge=False))
            return c

        lax.fori_loop(1, seq_len // QB, near_unit, 0, unroll=UNROLL_NEAR)

        def mid_first(ph, c):
            rows = [pl.ds(pl.multiple_of(ph * stream_len, QB), QB)]
            store(rows, block(rows, rows, cur_ok2, merge=True))
            return c

        lax.fori_loop(0, d_mid, mid_first, 0, unroll=2)
        later_blocks = stream_len // QB - 1

        def mid_unit(i, c):
            start = pl.multiple_of((i // later_blocks) * stream_len + (i % later_blocks + 1) * QB, QB)
            rows = [pl.ds(start, QB)]
            store(rows, block(rows, [pl.ds(start - QB, 2 * QB)], prev_cur_ok2, merge=True))
            return c

        lax.fori_loop(0, d_mid * later_blocks, mid_unit, 0, unroll=UNROLL_MID)

        def far_unit(r, c):
            rows = [pl.ds((r % d_mid) * stream_len + r // d_mid, QB, stride=ratio)]
            store(rows, block(rows, rows, cur_ok2, merge=True), with_lse=False)
            return c

        lax.fori_loop(0, d_far, far_unit, 0, unroll=UNROLL_FAR)

    @pl.when(b == 0)
    def _first_row():
        phase1(with_phase2=False)

    @pl.when(jnp.logical_and(b > 0, b < n_batch))
    def _steady_rows():
        phase1(with_phase2=True)

    @pl.when(b == n_batch)
    def _last_row():
        phase2()
        _sample_finish(t == nt - 1, xs_ref, wout_ref, ys_ref, att_s, mixab_s, cgs_s)

    pl.when(jnp.logical_and(t == nt - 1, b < n_batch))(attention)


def _layer_kernel_with_alias(*refs, n_alias, **kw):
    _layer_kernel(*refs[:N_LAYER_INPUTS], *refs[N_LAYER_INPUTS + n_alias:], **kw)


def _trunk_layer(layer, x, xs, decode_new, kt, vt,
                 norm_g, w_in, conv_w, conv_b, ln_g, ln_b, pool_wbd, pool_scale, qg, kg,
                 cos_t, sin_t, mavg, w_out, kv_prev=None):
    bsz, seq_len, _ = x.shape
    nsmp = xs.shape[0]
    depth, _, _, win_len = kt.shape
    assert seq_len % TQ == 0 and seq_len == QB * DILATIONS[-1] and DILATIONS[0] == 1
    assert win_len == WINDOW_KEYS * DILATIONS[-1]
    nt = seq_len // TQ
    blocks_per_sample, rem = divmod(bsz * nt, nsmp)
    assert rem == 0 and blocks_per_sample >= 1 and win_len % (blocks_per_sample * LANES) == 0
    win_blk = win_len // blocks_per_sample

    def p1(b, t):
        return jnp.minimum(b, bsz - 1), jnp.where(b == bsz, nt - 1, t)

    def p2(b, t):
        return jnp.maximum(b - 1, 0), jnp.where(b == 0, 0, t)

    def sample_block(b, t):
        pb, pt = p1(b, t)
        step = pb * nt + pt
        return (layer, step // blocks_per_sample, 0, blocks_per_sample - 1 - step % blocks_per_sample)

    def const(shape):
        nd = len(shape)
        return pl.BlockSpec(shape, lambda b, t: (0,) * nd)

    def resident(shape):
        nd = len(shape)
        return pl.BlockSpec(shape, lambda b, t: (0,) * nd, pipeline_mode=pl.Buffered(1))

    def per_layer(shape, **kw):
        nd = len(shape)
        return pl.BlockSpec((None,) + shape, lambda b, t: (layer,) + (0,) * nd, **kw)

    in_specs = [
        pl.BlockSpec((None, TQ, D_MODEL), lambda b, t: (*p1(b, t), 0)),
        pl.BlockSpec((None, TQ, D_MODEL), lambda b, t: (*p2(b, t), 0)),
        per_layer((1, D_MODEL)),
        per_layer((D_MODEL, D_IN), pipeline_mode=pl.Buffered(1)),
        per_layer((CONV_W, C_CONV)),
        per_layer((1, C_CONV)),
        per_layer((1, C_CONV)),
        per_layer((1, C_CONV)),
        per_layer((C_POOL, C_POOL)),
        per_layer((1, C_POOL)),
        per_layer((1, C_ATT)),
        per_layer((1, C_ATT)),
        pl.BlockSpec((TQ, LANES), lambda b, t: (p1(b, t)[1], 0)),
        pl.BlockSpec((TQ, LANES), lambda b, t: (p1(b, t)[1], 0)),
        const((C_ATT, C_ATT)),
        per_layer((D_MODEL, D_MODEL), pipeline_mode=pl.Buffered(1)),
        const((nsmp, D_MODEL)),
        const((nsmp, C_ATT)),
        const((nsmp, C_ATT)),
        const((nsmp, C_ATT)),
        const((nsmp, C_CONV + C_POOL)),
        const((nsmp, C_ATT)),
        pl.BlockSpec((None, None, C_ATT, win_blk), sample_block),
        pl.BlockSpec((None, None, C_ATT, win_blk), sample_block),
    ]
    operands = [x, x, norm_g, w_in, conv_w, conv_b, ln_g, ln_b, pool_wbd, pool_scale, qg, kg, cos_t, sin_t, mavg,
                w_out, xs, *decode_new, kt, vt]
    assert len(operands) == N_LAYER_INPUTS
    kv_spec = pl.BlockSpec((None, None, C_ATT, TQ), lambda b, t: (layer, p1(b, t)[0], 0, p1(b, t)[1]))
    out_specs = [
        pl.BlockSpec((None, TQ, D_MODEL), lambda b, t: (*p2(b, t), 0)),
        kv_spec,
        kv_spec,
        pl.BlockSpec((None, CONV_HALO, C_CONV), lambda b, t: (p1(b, t)[0], 0, 0)),
        pl.BlockSpec((None, POOL_BUF, C_POOL), lambda b, t: (p1(b, t)[0], 0, 0)),
        const((nsmp, D_MODEL)),
        pl.BlockSpec((None, None, C_ATT, win_blk), sample_block),
        pl.BlockSpec((None, None, C_ATT, win_blk), sample_block),
    ]
    out_shape = [
        jax.ShapeDtypeStruct((bsz, seq_len, D_MODEL), F32),
        jax.ShapeDtypeStruct((depth, bsz, C_ATT, seq_len), F32),
        jax.ShapeDtypeStruct((depth, bsz, C_ATT, seq_len), F32),
        jax.ShapeDtypeStruct((bsz, CONV_HALO, C_CONV), F32),
        jax.ShapeDtypeStruct((bsz, POOL_BUF, C_POOL), F32),
        jax.ShapeDtypeStruct((nsmp, D_MODEL), F32),
        jax.ShapeDtypeStruct((depth, nsmp, C_ATT, win_len), F32),
        jax.ShapeDtypeStruct((depth, nsmp, C_ATT, win_len), F32),
    ]
    static = dict(seq_len=seq_len, win_len=win_len)
    if kv_prev is None:
        kern = functools.partial(_layer_kernel, **static)
        aliases = {}
    else:
        kern = functools.partial(_layer_kernel_with_alias, n_alias=len(kv_prev), **static)
        in_specs += [pl.BlockSpec(memory_space=pl.ANY)] * len(kv_prev)
        operands += list(kv_prev)
        aliases = {N_LAYER_INPUTS + i: o for i, o in enumerate((1, 2, 6, 7))}
    scratch = [
        pltpu.VMEM((TQ, D_MODEL), BF16),
        pltpu.VMEM((TQ, D_IN), F32),
        pltpu.VMEM((N_CONV_SLAB, TQ + U_HALO, LANES), F32),
        pltpu.VMEM((N_POOL_SLAB, TQ + B_HALO, LANES), F32),
        *[pltpu.VMEM((N_ATT_SLAB, TQ, LANES), F32) for _ in range(4)],
        *[pltpu.VMEM((N_ATT_SLAB, seq_len, LANES), F32) for _ in range(AB_COUNT)],
        pltpu.VMEM((seq_len, D_MODEL), BF16),
        pltpu.VMEM((nsmp, C_ATT), F32),
        pltpu.VMEM((C_ATT, LANES), F32),
        pltpu.VMEM((N_HEADS, LANES), F32),
        pltpu.VMEM((N_HEADS, LANES), F32),
        pltpu.VMEM((C_ATT, LANES), F32),
        pltpu.VMEM((C_ATT, LANES), F32),
        pltpu.VMEM((C_ATT, LANES), F32),
    ]
    return pl.pallas_call(
        kern,
        out_shape=out_shape,
        grid=(bsz + 1, nt),
        in_specs=in_specs,
        out_specs=out_specs,
        scratch_shapes=scratch,
        input_output_aliases=aliases,
        compiler_params=pltpu.CompilerParams(
            dimension_semantics=("arbitrary", "arbitrary"),
            vmem_limit_bytes=VMEM_LIMIT_BYTES,
        ),
        name="trunk_layer",
    )(*operands)


def _row_to_col_tile(row):
    return jnp.broadcast_to(row, (LANES, row.shape[1])).T


def _key_multiplicity(lane0, blk, win_len):
    t = lane0 + lax.broadcasted_iota(jnp.int32, (1, blk), 1)
    delta = win_len - t
    cnt = jnp.zeros((1, blk), F32)
    for d in DILATIONS:
        hit = jnp.logical_and(delta % d == 0, delta <= d * WINDOW_KEYS)
        cnt = cnt + hit.astype(F32)
    return cnt


def _decode_prepare_kernel(x_ref, sc_ref, sp_ref,
                           ng_ref, win_ref, cw_ref, cb_ref, lng_ref, lnb_ref, pw_ref, psc_ref, qg_ref, kg_ref,
                           cos_ref, sin_ref, mavg_ref,
                           nc_ref, np_ref, qr_s, kr_s, v_s, mixab_s, cg_s, *, pos):
    nsmp = x_ref.shape[0]

    x = x_ref[...]
    ms = jnp.mean(x * x, axis=-1, keepdims=True)
    h = (x * lax.rsqrt(ms + EPS) * ng_ref[...]).astype(BF16)
    proj = jnp.dot(h, win_ref[...], preferred_element_type=F32)

    u = proj[:, OFF_A_VAL:OFF_A_VAL + C_CONV] * _sigmoid(proj[:, OFF_A_GLU:OFF_A_GLU + C_CONV])
    conv = u * cw_ref[CONV_HALO:CONV_W, :] + cb_ref[...]
    for w in range(CONV_HALO):
        conv = conv + sc_ref[:, w, :] * cw_ref[w:w + 1, :]
    nc_ref[:, 0:CONV_HALO - 1, :] = sc_ref[:, 1:CONV_HALO, :]
    nc_ref[:, CONV_HALO - 1, :] = u
    mu = jnp.mean(conv, axis=-1, keepdims=True)
    cen = conv - mu
    var = jnp.mean(cen * cen, axis=-1, keepdims=True)
    ln = cen * lax.rsqrt(var + EPS) * lng_ref[...] + lnb_ref[...]
    ya = _silu(ln) * _silu(proj[:, OFF_A_GATE:OFF_A_GATE + C_CONV])

    bval = proj[:, OFF_B_VAL:OFF_B_VAL + C_POOL]
    lane_p = lax.broadcasted_iota(jnp.int32, (nsmp, C_POOL), 1)
    pooled = jnp.zeros((nsmp, C_POOL), F32)
    acc = bval
    done = 1
    for wi, w in enumerate(POOL_WINDOWS):
        for i in range(done, w):
            acc = acc + sp_ref[:, POOL_BUF - i, :]
        done = w
        pooled = jnp.where(lane_p // POOL_GC == wi, acc / float(min(pos + 1, w)), pooled)
    np_ref[:, 0:POOL_BUF - 1, :] = sp_ref[:, 1:POOL_BUF, :]
    np_ref[:, POOL_BUF - 1, :] = bval
    dpool = (pooled - bval).astype(BF16)
    yb = (jnp.dot(dpool, pw_ref[...], preferred_element_type=F32) * psc_ref[...]
          * _silu(proj[:, OFF_B_GATE:OFF_B_GATE + C_POOL]))
    mixab_s[:, 0:C_CONV] = ya
    mixab_s[:, C_CONV:C_CONV + C_POOL] = yb
    cg_s[...] = _silu(proj[:, OFF_C_GATE:OFF_C_GATE + C_ATT])

    q = proj[:, OFF_Q:OFF_Q + C_ATT]
    k = proj[:, OFF_K:OFF_K + C_ATT]
    qn = q * lax.rsqrt(jnp.dot((q * q).astype(BF16), mavg_ref[...], preferred_element_type=F32) + EPS) * qg_ref[...]
    kn = k * lax.rsqrt(jnp.dot((k * k).astype(BF16), mavg_ref[...], preferred_element_type=F32) + EPS) * kg_ref[...]
    for s in range(N_ATT_SLAB):
        cols = slice(s * LANES, (s + 1) * LANES)
        qs, ks = qn[:, cols], kn[:, cols]
        qr_s[:, cols] = (qs * cos_ref[...] + _swap_halves(qs) * sin_ref[...]) * (HEAD_DIM ** -0.5)
        kr_s[:, cols] = ks * cos_ref[...] + _swap_halves(ks) * sin_ref[...]
    v_s[...] = proj[:, OFF_V:OFF_V + C_ATT]


def _decode_prepare(layer, pos, xs, state_conv, state_pool, norm_g, w_in, conv_w, conv_b, ln_g, ln_b,
                    pool_wbd, pool_scale, qg, kg, cos_s, sin_s, mavg):
    nsmp = xs.shape[0]

    def const(shape):
        nd = len(shape)
        return pl.BlockSpec(shape, lambda i: (0,) * nd)

    def per_layer(shape):
        nd = len(shape)
        return pl.BlockSpec((None,) + shape, lambda i: (layer,) + (0,) * nd)

    in_specs = [
        const((nsmp, D_MODEL)), per_layer((nsmp, CONV_HALO, C_CONV)), per_layer((nsmp, POOL_BUF, C_POOL)),
        per_layer((1, D_MODEL)), per_layer((D_MODEL, D_IN)), per_layer((CONV_W, C_CONV)), per_layer((1, C_CONV)),
        per_layer((1, C_CONV)), per_layer((1, C_CONV)), per_layer((C_POOL, C_POOL)), per_layer((1, C_POOL)),
        per_layer((1, C_ATT)), per_layer((1, C_ATT)), const((1, LANES)), const((1, LANES)), const((C_ATT, C_ATT)),
    ]
    out_dims = [(nsmp, CONV_HALO, C_CONV), (nsmp, POOL_BUF, C_POOL), (nsmp, C_ATT), (nsmp, C_ATT), (nsmp, C_ATT),
                (nsmp, C_CONV + C_POOL), (nsmp, C_ATT)]
    return pl.pallas_call(
        functools.partial(_decode_prepare_kernel, pos=pos),
        out_shape=[jax.ShapeDtypeStruct(d, F32) for d in out_dims],
        grid=(1,),
        in_specs=in_specs,
        out_specs=[const(d) for d in out_dims],
        compiler_params=pltpu.CompilerParams(dimension_semantics=("arbitrary",)),
        name="decode_prepare",
    )(xs, state_conv, state_pool, norm_g, w_in, conv_w, conv_b, ln_g, ln_b, pool_wbd, pool_scale, qg, kg,
      cos_s, sin_s, mavg)


def _sample_start(cond, smp, qr_s, kr_s, v_s, qcol_s, sm_s, sl_s, so_s, kcar_s, vcar_s):
    nsmp = qr_s.shape[0]
    n_pat = float(len(DILATIONS))

    @pl.when(cond)
    def _start():
        mine = lax.broadcasted_iota(jnp.int32, (nsmp, C_ATT), 0) == smp

        def col_tile(ref):
            return _row_to_col_tile(jnp.sum(jnp.where(mine, ref[...], 0.0), axis=0, keepdims=True))

        q_col = col_tile(qr_s)
        k_col = col_tile(kr_s)
        v_col = col_tile(v_s)
        qcol_s[...] = q_col
        kcar_s[...] = k_col
        vcar_s[...] = v_col
        so_s[...] = v_col * n_pat
        for hd in range(N_HEADS):
            hr = slice(hd * HEAD_DIM, (hd + 1) * HEAD_DIM)
            sm_s[hd:hd + 1, :] = jnp.sum(k_col[hr, :] * q_col[hr, :], axis=0, keepdims=True)
        sl_s[...] = jnp.full(sl_s.shape, n_pat, F32)


def _sample_window(lane0, kt_ref, vt_ref, okt_ref, ovt_ref, qcol_s, sm_s, sl_s, so_s, kcar_s, vcar_s, *, win_len):
    blk = kt_ref.shape[1]
    n_tiles = blk // LANES
    cnt = _key_multiplicity(lane0, blk, win_len)
    reach = cnt > 0.0
    last_lane = lax.broadcasted_iota(jnp.int32, (HEAD_DIM, blk), 1) == blk - 1

    for hd in range(N_HEADS):
        hr = slice(hd * HEAD_DIM, (hd + 1) * HEAD_DIM)
        kt = kt_ref[hr, :]
        vt = vt_ref[hr, :]
        s_win = jnp.sum(kt * jnp.concatenate([qcol_s[hr, :]] * n_tiles, axis=1), axis=0, keepdims=True)
        s_win = jnp.where(reach, s_win, NEG)
        m_old = sm_s[hd:hd + 1, :]
        m_new = jnp.maximum(m_old, jnp.max(s_win, axis=-1, keepdims=True))
        w_old = jnp.exp(m_old - m_new)
        p_win = cnt * jnp.exp(s_win - m_new[:, 0:1])
        sm_s[hd:hd + 1, :] = m_new
        sl_s[hd:hd + 1, :] = sl_s[hd:hd + 1, :] * w_old + jnp.sum(p_win, axis=-1, keepdims=True)
        so_s[hr, :] = so_s[hr, :] * w_old + jnp.sum(vt * p_win, axis=-1, keepdims=True)
        k_next = jnp.concatenate([kcar_s[hr, :]] * n_tiles, axis=1)
        v_next = jnp.concatenate([vcar_s[hr, :]] * n_tiles, axis=1)
        okt_ref[hr, :] = jnp.where(last_lane, k_next, pltpu.roll(kt, blk - 1, 1))
        ovt_ref[hr, :] = jnp.where(last_lane, v_next, pltpu.roll(vt, blk - 1, 1))
        kcar_s[hr, :] = jnp.broadcast_to(kt[:, 0:1], (HEAD_DIM, LANES))
        vcar_s[hr, :] = jnp.broadcast_to(vt[:, 0:1], (HEAD_DIM, LANES))


def _sample_done(cond, smp, att_s, sl_s, so_s):
    nsmp = att_s.shape[0]

    @pl.when(cond)
    def _done():
        cols = [so_s[hd * HEAD_DIM:(hd + 1) * HEAD_DIM, :] / sl_s[hd:hd + 1, :] for hd in range(N_HEADS)]
        att_row = jnp.concatenate(cols, axis=0).T[0:1, :]
        mine = lax.broadcasted_iota(jnp.int32, (nsmp, C_ATT), 0) == smp
        att_s[...] = jnp.where(mine, att_row, att_s[...])


def _sample_finish(cond, x_ref, wout_ref, y_ref, att_s, mixab_s, cg_s):
    @pl.when(cond)
    def _finish():
        yc = att_s[...] * cg_s[...]
        mixed = jnp.concatenate([mixab_s[...], yc], axis=-1).astype(BF16)
        y_ref[...] = x_ref[...] + jnp.dot(mixed, wout_ref[...], preferred_element_type=F32)


def _to_channel_major(a):
    depth, bsz, ntok, nh, hd = a.shape
    return jnp.transpose(a, (0, 1, 3, 4, 2)).reshape(depth, bsz, nh * hd, ntok)


def _from_channel_major(a):
    depth, bsz, _, ntok = a.shape
    return jnp.transpose(a.reshape(depth, bsz, N_HEADS, HEAD_DIM, ntok), (0, 1, 4, 2, 3))


def kernel(x_prompt, x_sample, state_conv, state_pool, cache_k_win, cache_v_win, norm_g, w_in, conv_w, conv_b,
           ln_g, ln_b, pool_w, pool_scale, q_norm_g, k_norm_g, w_out):
    depth = w_in.shape[0]
    seq_len = x_prompt.shape[1]
    nsmp = x_sample.shape[0]

    cos_p, sin_p = _rope_tables(jnp.arange(seq_len, dtype=jnp.int32))
    cos_s, sin_s = _rope_tables(jnp.full((1,), PAST_LEN, dtype=jnp.int32))
    mavg = _head_mean_matrix()
    weights = (norm_g[:, None], w_in.astype(BF16), conv_w, conv_b[:, None], ln_g[:, None], ln_b[:, None],
               _pool_block_diag(pool_w).astype(BF16), pool_scale[:, None],
               jnp.tile(q_norm_g, (1, N_HEADS))[:, None], jnp.tile(k_norm_g, (1, N_HEADS))[:, None])
    w_out_b = w_out.astype(BF16)

    kt = _to_channel_major(cache_k_win)
    vt = _to_channel_major(cache_v_win)
    xp = x_prompt
    xs = x_sample.reshape(nsmp, D_MODEL)
    kv = None
    conv_p, pool_p, conv_s, pool_s = [], [], [], []
    for layer in range(depth):
        ncs, nps, *decode_new = _decode_prepare(layer, PAST_LEN, xs, state_conv, state_pool, *weights,
                                                cos_s, sin_s, mavg)
        xp, kp, vp, cst, pst, xs, ks, vs = _trunk_layer(
            layer, xp, xs, decode_new, kt, vt, *weights, cos_p, sin_p, mavg, w_out_b, kv_prev=kv)
        kv = (kp, vp, ks, vs)
        conv_p.append(cst)
        pool_p.append(pst)
        conv_s.append(ncs)
        pool_s.append(nps)

    return (xp, xs.reshape(nsmp, 1, D_MODEL), jnp.stack(conv_p), jnp.stack(pool_p),
            _from_channel_major(kv[0]), _from_channel_major(kv[1]),
            jnp.stack(conv_s), jnp.stack(pool_s), _from_channel_major(kv[2]), _from_channel_major(kv[3]))
```

```python
import functools

import jax
import jax.numpy as jnp
import numpy as np
from jax import lax
from jax.experimental import pallas as pl
from jax.experimental.pallas import tpu as pltpu

F32 = jnp.float32
BF16 = jnp.bfloat16

D_MODEL = 1024
C_CONV = 384
C_POOL = 256
C_ATT = 384
HEAD_DIM = 64
N_HEADS = C_ATT // HEAD_DIM
CONV_W = 31
CONV_HALO = CONV_W - 1
POOL_WINDOWS = (2, 4, 8, 16)
POOL_GC = 64
POOL_BUF = 15
DILATIONS = (1, 4, 16)
WINDOW_KEYS = 128
EPS = 1e-6
ROPE_THETA = 10000.0
D_IN = 3 * C_CONV + 2 * C_POOL + 4 * C_ATT
PAST_LEN = 16384
NEG = -1e30
LOG2_E = 1.4426950408889634

OFF_A_VAL = 0
OFF_A_GLU = OFF_A_VAL + C_CONV
OFF_A_GATE = OFF_A_GLU + C_CONV
OFF_B_VAL = OFF_A_GATE + C_CONV
OFF_B_GATE = OFF_B_VAL + C_POOL
OFF_Q = OFF_B_GATE + C_POOL
OFF_K = OFF_Q + C_ATT
OFF_V = OFF_K + C_ATT
OFF_C_GATE = OFF_V + C_ATT

LANES = 128
N_CONV_SLAB = C_CONV // LANES
N_POOL_SLAB = C_POOL // LANES
N_ATT_SLAB = C_ATT // LANES
VMEM_LIMIT_BYTES = 60 * 1024 * 1024

TQ = 256
RC = 32
CONV_GROUP = 8
U_HALO = 32
B_HALO = 16
QB = WINDOW_KEYS
UNROLL_NEAR = 15
UNROLL_MID = 6
UNROLL_FAR = 8

AB_Q, AB_K, AB_V, AB_O, AB_LSE = range(5)
AB_COUNT = 5


def _sigmoid(x):
    return 0.5 * jnp.tanh(0.5 * x) + 0.5


def _silu(x):
    h = 0.5 * x
    return h * jnp.tanh(h) + h


def _rope_tables(positions):
    half = HEAD_DIM // 2
    inv = ROPE_THETA ** (-jnp.arange(half, dtype=F32) / half)
    ang = positions.astype(F32)[:, None] * inv[None, :]
    cos = jnp.cos(ang)
    sin = jnp.sin(ang)
    cos_h = jnp.concatenate([cos, cos], axis=-1)
    sin_h = jnp.concatenate([-sin, sin], axis=-1)
    reps = LANES // HEAD_DIM
    return jnp.tile(cos_h, (1, reps)), jnp.tile(sin_h, (1, reps))


def _head_mean_matrix():
    idx = np.arange(C_ATT) // HEAD_DIM
    return jnp.asarray((idx[:, None] == idx[None, :]).astype(np.float32) / HEAD_DIM, dtype=BF16)


def _pool_block_diag(pool_w):
    out = jnp.zeros((pool_w.shape[0], C_POOL, C_POOL), pool_w.dtype)
    for g in range(len(POOL_WINDOWS)):
        out = out.at[:, g * POOL_GC:(g + 1) * POOL_GC, g * POOL_GC:(g + 1) * POOL_GC].set(pool_w[:, g])
    return out


def _swap_halves(x):
    lane = lax.broadcasted_iota(jnp.int32, x.shape, 1)
    first_half = (lane % HEAD_DIM) < (HEAD_DIM // 2)
    return jnp.where(first_half, pltpu.roll(x, LANES - HEAD_DIM // 2, 1), pltpu.roll(x, HEAD_DIM // 2, 1))


def _pool_means(loads, pos):
    lane = lax.broadcasted_iota(jnp.int32, loads(0, 0).shape, 1)
    lo = lane < POOL_GC
    posf = (pos + 1).astype(F32)
    outs = []
    for slab in range(N_POOL_SLAB):
        w_lo, w_hi = POOL_WINDOWS[2 * slab], POOL_WINDOWS[2 * slab + 1]
        cur = loads(0, slab)
        s = cur
        for i in range(1, w_lo):
            s = s + loads(i, slab)
        s_lo = s
        for i in range(w_lo, w_hi):
            s = s + loads(i, slab)
        s_hi = s
        cnt_lo = jnp.minimum(posf, float(w_lo))
        cnt_hi = jnp.minimum(posf, float(w_hi))
        pooled = jnp.where(lo, s_lo / cnt_lo, s_hi / cnt_hi)
        outs.append(pooled - cur)
    return outs


N_LAYER_INPUTS = 24


def _layer_kernel(x_ref, x2_ref, ng_ref, win_ref, cw_ref, cb_ref, lng_ref, lnb_ref, pw_ref, psc_ref,
                  qg_ref, kg_ref, cos_ref, sin_ref, mavg_ref, wout_ref,
                  xs_ref, qr_s, kr_s, v_s, mixab_s, cgs_s, kt_ref, vt_ref,
                  y_ref, ko_ref, vo_ref, cst_ref, pst_ref,
                  ys_ref, okt_ref, ovt_ref,
                  h_s, proj, u_buf, b_buf,
                  st_q, st_k, st_v, st_att, conv_s, ab0, ab1, ab2, ab3, ab4, mix,
                  att_s, qcol_s, sm_s, sl_s, so_s, kcar_s, vcar_s,
                  *, seq_len, win_len):
    ab = (ab0, ab1, ab2, ab3, ab4)
    nt = seq_len // TQ
    b = pl.program_id(0)
    t = pl.program_id(1)
    n_batch = pl.num_programs(0) - 1
    blocks_per_sample = win_len // kt_ref.shape[1]
    row0 = pl.multiple_of(t * TQ, TQ)
    sq_s = h_s.at[:, 0:2 * C_ATT]
    d_s = h_s.at[:, 2 * C_ATT:2 * C_ATT + C_POOL]
    msq_s = proj.at[:, OFF_A_VAL:OFF_A_VAL + 2 * C_ATT]
    yb_s = proj.at[:, OFF_B_VAL:OFF_B_VAL + C_POOL]

    step = b * nt + t
    part = step % blocks_per_sample
    smp = step // blocks_per_sample

    def sample_start():
        _sample_start(part == 0, smp, qr_s, kr_s, v_s, qcol_s, sm_s, sl_s, so_s, kcar_s, vcar_s)

    def sample_window():
        _sample_window((blocks_per_sample - 1 - part) * kt_ref.shape[1], kt_ref, vt_ref, okt_ref, ovt_ref,
                       qcol_s, sm_s, sl_s, so_s, kcar_s, vcar_s, win_len=win_len)

    def sample_done():
        _sample_done(part == blocks_per_sample - 1, smp, att_s, sl_s, so_s)

    def phase2():
        d_mid = DILATIONS[1]
        per_stream = TQ // d_mid
        for s in range(N_ATT_SLAB):
            for r in range(d_mid):
                src_rows = pl.ds(pl.multiple_of(r * (seq_len // d_mid) + t * per_stream, per_stream), per_stream)
                st_att[s, pl.ds(r, per_stream, stride=d_mid), :] = ab[AB_O][s, src_rows, :]

        def att_chunk(i, c):
            r = pl.multiple_of(i * RC, RC)
            grow = pl.ds(pl.multiple_of(row0 + r, RC), RC)
            for s in range(N_ATT_SLAB):
                mcols = slice(C_CONV + C_POOL + s * LANES, C_CONV + C_POOL + (s + 1) * LANES)
                yc = st_att[s, pl.ds(r, RC), :] * mix[grow, mcols].astype(F32)
                mix[grow, mcols] = yc.astype(BF16)
            return c

        lax.fori_loop(0, TQ // RC, att_chunk, 0, unroll=True)
        y_ref[...] = x2_ref[...] + jnp.dot(mix[pl.ds(row0, TQ), :], wout_ref[...], preferred_element_type=F32)

    def phase1(with_phase2):
        @pl.when(t == 0)
        def _zero_halo():
            u_buf[:, 0:U_HALO, :] = jnp.zeros((N_CONV_SLAB, U_HALO, LANES), F32)
            b_buf[:, 0:B_HALO, :] = jnp.zeros((N_POOL_SLAB, B_HALO, LANES), F32)
            if not with_phase2:
                att_s[...] = jnp.zeros(att_s.shape, F32)

        sample_start()
        if with_phase2:
            phase2()
        sample_window()

        def norm_chunk(i, c):
            r = pl.multiple_of(i * RC, RC)
            x = x_ref[pl.ds(r, RC), :]
            ms = jnp.mean(x * x, axis=-1, keepdims=True)
            h_s[pl.ds(r, RC), :] = (x * lax.rsqrt(ms + EPS) * ng_ref[...]).astype(BF16)
            return c

        lax.fori_loop(0, TQ // RC, norm_chunk, 0, unroll=True)
        proj[...] = jnp.dot(h_s[...], win_ref[...], preferred_element_type=F32)

        def split_chunk(i, c):
            r = pl.multiple_of(i * RC, RC)
            rows = pl.ds(r, RC)
            grow = pl.ds(pl.multiple_of(row0 + r, RC), RC)
            for s in range(N_CONV_SLAB):
                cols = slice(s * LANES, (s + 1) * LANES)
                a_val = proj[rows, OFF_A_VAL + s * LANES:OFF_A_VAL + (s + 1) * LANES]
                a_glu = proj[rows, OFF_A_GLU + s * LANES:OFF_A_GLU + (s + 1) * LANES]
                u_buf[s, pl.ds(U_HALO + r, RC), :] = a_val * _sigmoid(a_glu)
                q = proj[rows, OFF_Q + s * LANES:OFF_Q + (s + 1) * LANES]
                k = proj[rows, OFF_K + s * LANES:OFF_K + (s + 1) * LANES]
                sq_s[rows, cols] = (q * q).astype(BF16)
                sq_s[rows, C_ATT + s * LANES:C_ATT + (s + 1) * LANES] = (k * k).astype(BF16)
                st_v[s, rows, :] = proj[rows, OFF_V + s * LANES:OFF_V + (s + 1) * LANES]
                mix[grow, C_CONV + C_POOL + s * LANES:C_CONV + C_POOL + (s + 1) * LANES] = _silu(
                    proj[rows, OFF_C_GATE + s * LANES:OFF_C_GATE + (s + 1) * LANES]).astype(BF16)
            for s in range(N_POOL_SLAB):
                b_buf[s, pl.ds(B_HALO + r, RC), :] = proj[rows, OFF_B_VAL + s * LANES:OFF_B_VAL + (s + 1) * LANES]
            pos = row0 + r + lax.broadcasted_iota(jnp.int32, (RC, 1), 0)
            dl = _pool_means(lambda sh, s: b_buf[s, pl.ds(r + B_HALO - sh, RC), :], pos)
            for s in range(N_POOL_SLAB):
                d_s[rows, s * LANES:(s + 1) * LANES] = dl[s].astype(BF16)
            return c

        lax.fori_loop(0, TQ // RC, split_chunk, 0, unroll=True)
        msq_s[:, 0:C_ATT] = jnp.dot(sq_s[:, 0:C_ATT], mavg_ref[...], preferred_element_type=F32)
        msq_s[:, C_ATT:2 * C_ATT] = jnp.dot(sq_s[:, C_ATT:2 * C_ATT], mavg_ref[...], preferred_element_type=F32)
        yb_s[...] = jnp.dot(d_s[...], pw_ref[...], preferred_element_type=F32)

        row_groups = TQ // CONV_GROUP
        for s in range(N_CONV_SLAB):
            cols = slice(s * LANES, (s + 1) * LANES)
            acc = [jnp.broadcast_to(cb_ref[:, cols], (CONV_GROUP, LANES))] * row_groups
            for a in range(CONV_GROUP):
                taps = list(range(a, CONV_W, CONV_GROUP))
                cws = [jnp.broadcast_to(cw_ref[w:w + 1, cols], (CONV_GROUP, LANES)) for w in taps]
                for j in range(row_groups + len(taps) - 1):
                    start = (U_HALO - CONV_HALO) + a + CONV_GROUP * j
                    window = u_buf[s, start:start + CONV_GROUP, :]
                    for m in range(len(taps)):
                        if 0 <= j - m < row_groups:
                            acc[j - m] = acc[j - m] + window * cws[m]
            for g in range(row_groups):
                conv_s[s, g * CONV_GROUP:(g + 1) * CONV_GROUP, :] = acc[g]

        def mixer_chunk(i, c):
            r = pl.multiple_of(i * RC, RC)
            rows = pl.ds(r, RC)
            grow = pl.ds(pl.multiple_of(row0 + r, RC), RC)
            conv = [conv_s[s, rows, :] for s in range(N_CONV_SLAB)]
            mu = jnp.sum(conv[0] + conv[1] + conv[2], axis=-1, keepdims=True) * (1.0 / C_CONV)
            cen = [cv - mu for cv in conv]
            var = jnp.sum(cen[0] * cen[0] + cen[1] * cen[1] + cen[2] * cen[2], axis=-1, keepdims=True) * (1.0 / C_CONV)
            rstd = lax.rsqrt(var + EPS)
            for s in range(N_CONV_SLAB):
                cols = slice(s * LANES, (s + 1) * LANES)
                ln = cen[s] * rstd * lng_ref[:, cols] + lnb_ref[:, cols]
                gate = proj[rows, OFF_A_GATE + s * LANES:OFF_A_GATE + (s + 1) * LANES]
                mix[grow, cols] = (_silu(ln) * _silu(gate)).astype(BF16)
            for s in range(N_POOL_SLAB):
                cols = slice(s * LANES, (s + 1) * LANES)
                gate = proj[rows, OFF_B_GATE + s * LANES:OFF_B_GATE + (s + 1) * LANES]
                yb = yb_s[rows, cols] * psc_ref[:, cols] * _silu(gate)
                mix[grow, C_CONV + s * LANES:C_CONV + (s + 1) * LANES] = yb.astype(BF16)
            cos = cos_ref[rows, :]
            sin = sin_ref[rows, :]
            for s in range(N_ATT_SLAB):
                cols = slice(s * LANES, (s + 1) * LANES)
                q = proj[rows, OFF_Q + s * LANES:OFF_Q + (s + 1) * LANES]
                qn = q * lax.rsqrt(msq_s[rows, cols] + EPS) * qg_ref[:, cols]
                qr = qn * cos + _swap_halves(qn) * sin
                st_q[s, rows, :] = qr * (HEAD_DIM ** -0.5 * LOG2_E)
                k = proj[rows, OFF_K + s * LANES:OFF_K + (s + 1) * LANES]
                kn = k * lax.rsqrt(msq_s[rows, C_ATT + s * LANES:C_ATT + (s + 1) * LANES] + EPS) * kg_ref[:, cols]
                st_k[s, rows, :] = kn * cos + _swap_halves(kn) * sin
            return c

        lax.fori_loop(0, TQ // RC, mixer_chunk, 0, unroll=True)

        d_mid = DILATIONS[1]
        per_stream = TQ // d_mid
        for s in range(N_ATT_SLAB):
            ko_ref[s * LANES:(s + 1) * LANES, :] = st_k[s].T
            vo_ref[s * LANES:(s + 1) * LANES, :] = st_v[s].T
            for st, dst in ((st_q, AB_Q), (st_k, AB_K), (st_v, AB_V)):
                for r in range(d_mid):
                    dst_rows = pl.ds(pl.multiple_of(r * (seq_len // d_mid) + t * per_stream, per_stream), per_stream)
                    ab[dst][s, dst_rows, :] = st[s, pl.ds(r, per_stream, stride=d_mid), :]

        @pl.when(t == nt - 1)
        def _write_state():
            for s in range(N_CONV_SLAB):
                cst_ref[:, s * LANES:(s + 1) * LANES] = u_buf[s, TQ + U_HALO - CONV_HALO:TQ + U_HALO, :]
            for s in range(N_POOL_SLAB):
                pst_ref[:, s * LANES:(s + 1) * LANES] = b_buf[s, TQ + B_HALO - POOL_BUF:TQ + B_HALO, :]

        u_buf[:, 0:U_HALO, :] = u_buf[:, TQ:TQ + U_HALO, :]
        b_buf[:, 0:B_HALO, :] = b_buf[:, TQ:TQ + B_HALO, :]
        sample_done()

    def attention():
        d_mid, d_far = DILATIONS[1], DILATIONS[2]
        ratio = d_far // d_mid
        stream_len = seq_len // d_mid
        piece = QB // d_mid
        lane = lax.broadcasted_iota(jnp.int32, (QB, LANES), 1)
        rowi = lax.broadcasted_iota(jnp.int32, (QB, LANES), 0)
        lo = lane < HEAD_DIM

        def masks(key_pos, query_pos):
            cur = key_pos <= query_pos
            prev = key_pos >= query_pos
            cur2 = jnp.concatenate([cur, cur], axis=0)
            return cur2, jnp.concatenate([jnp.concatenate([prev, prev], axis=0), cur2], axis=1)

        cur_ok2, prev_cur_ok2 = masks(lane, rowi)
        ncur_ok2, nprev_cur_ok2 = masks(d_mid * (lane % piece) + lane // piece, d_mid * (rowi % piece) + rowi // piece)

        def attend(q, keys, vals, mask):
            qa = jnp.where(lo, q, 0.0).astype(BF16)
            qb = jnp.where(lo, 0.0, q).astype(BF16)
            q2 = jnp.concatenate([qa, qb], axis=0)
            sc = lax.dot_general(q2, keys.astype(BF16), (((1,), (1,)), ((), ())), preferred_element_type=F32)
            sc = jnp.where(mask, sc, NEG)
            m = jnp.max(sc, axis=-1, keepdims=True)
            p = jnp.exp2(sc - m).astype(BF16)
            v1 = jnp.concatenate([vals.astype(BF16), jnp.ones(vals.shape, BF16)], axis=1)
            ol = jnp.dot(p, v1, preferred_element_type=F32)
            o_u = jnp.where(lo, ol[0:QB, 0:LANES], ol[QB:2 * QB, 0:LANES])
            l_u = jnp.where(lo, ol[0:QB, LANES:2 * LANES], ol[QB:2 * QB, LANES:2 * LANES])
            m_u = jnp.where(lo, m[0:QB], m[QB:2 * QB])
            return o_u, m_u, l_u

        def load(ref, s, pieces):
            tiles = [ref[s, p, :] for p in pieces]
            return tiles[0] if len(tiles) == 1 else jnp.concatenate(tiles, axis=0)

        def save(ref, s, pieces, val):
            n = val.shape[0] // len(pieces)
            for j, p in enumerate(pieces):
                ref[s, p, :] = val[j * n:(j + 1) * n]

        def block(rows, krows, mask, merge, with_lse=True):
            outs = []
            for s in range(N_ATT_SLAB):
                o_u, m_u, l_u = attend(load(ab[AB_Q], s, rows), load(ab[AB_K], s, krows), load(ab[AB_V], s, krows),
                                       mask)
                if merge:
                    lse_old = load(ab[AB_LSE], s, rows)
                    m_new = jnp.maximum(lse_old, m_u)
                    w_old = jnp.exp2(lse_old - m_new)
                    w_u = jnp.exp2(m_u - m_new)
                    den = w_old + l_u * w_u
                    o_n = (load(ab[AB_O], s, rows) * w_old + o_u * w_u) * (1.0 / den)
                else:
                    m_new, den = m_u, l_u
                    o_n = o_u * (1.0 / l_u)
                outs.append((o_n, m_new + jnp.log2(den) if with_lse else None))
            return outs

        def store(rows, outs):
            for s, (o_n, lse_n) in enumerate(outs):
                save(ab[AB_O], s, rows, o_n)
                if lse_n is not None:
                    save(ab[AB_LSE], s, rows, lse_n)

        def token_block(base):
            return [pl.ds(r * stream_len + base, piece) for r in range(d_mid)]

        store(token_block(0), block(token_block(0), token_block(0), ncur_ok2, merge=False))

        def near_unit(i, c):
            base = pl.multiple_of(i * piece, piece)
            rows = token_block(base)
            store(rows, block(rows, token_block(base - piece) + rows, nprev_cur_ok2, merge=False))
            return c

        lax.fori_loop(1, seq_len // QB, near_unit, 0, unroll=UNROLL_NEAR)

        def mid_first(ph, c):
            rows = [pl.ds(pl.multiple_of(ph * stream_len, QB), QB)]
            store(rows, block(rows, rows, cur_ok2, merge=True))
            return c

        lax.fori_loop(0, d_mid, mid_first, 0, unroll=2)
        later_blocks = stream_len // QB - 1

        def mid_unit(i, c):
            start = pl.multiple_of((i // later_blocks) * stream_len + (i % later_blocks + 1) * QB, QB)
            rows = [pl.ds(start, QB)]
            store(rows, block(rows, [pl.ds(start - QB, 2 * QB)], prev_cur_ok2, merge=True))
            return c

        lax.fori_loop(0, d_mid * later_blocks, mid_unit, 0, unroll=UNROLL_MID)

        def far_unit(r, c):
            rows = [pl.ds((r % d_mid) * stream_len + r // d_mid, QB, stride=ratio)]
            store(rows, block(rows, rows, cur_ok2, merge=True, with_lse=False))
            return c

        lax.fori_loop(0, d_far, far_unit, 0, unroll=UNROLL_FAR)

    @pl.when(b == 0)
    def _first_row():
        phase1(with_phase2=False)

    @pl.when(jnp.logical_and(b > 0, b < n_batch))
    def _steady_rows():
        phase1(with_phase2=True)

    @pl.when(b == n_batch)
    def _last_row():
        phase2()
        _sample_finish(t == nt - 1, xs_ref, wout_ref, ys_ref, att_s, mixab_s, cgs_s)

    pl.when(jnp.logical_and(t == nt - 1, b < n_batch))(attention)


def _layer_kernel_with_alias(*refs, n_alias, **kw):
    _layer_kernel(*refs[:N_LAYER_INPUTS], *refs[N_LAYER_INPUTS + n_alias:], **kw)


def _trunk_layer(layer, x, xs, decode_new, kt, vt,
                 norm_g, w_in, conv_w, conv_b, ln_g, ln_b, pool_wbd, pool_scale, qg, kg,
                 cos_t, sin_t, mavg, w_out, kv_prev=None):
    bsz, seq_len, _ = x.shape
    nsmp = xs.shape[0]
    depth, _, _, win_len = kt.shape
    assert seq_len % TQ == 0 and seq_len == QB * DILATIONS[-1] and DILATIONS[0] == 1
    assert win_len == WINDOW_KEYS * DILATIONS[-1]
    nt = seq_len // TQ
    blocks_per_sample, rem = divmod(bsz * nt, nsmp)
    assert rem == 0 and blocks_per_sample >= 1 and win_len % (blocks_per_sample * LANES) == 0
    win_blk = win_len // blocks_per_sample

    def p1(b, t):
        return jnp.minimum(b, bsz - 1), jnp.where(b == bsz, nt - 1, t)

    def p2(b, t):
        return jnp.maximum(b - 1, 0), jnp.where(b == 0, 0, t)

    def sample_block(b, t):
        pb, pt = p1(b, t)
        step = pb * nt + pt
        return (layer, step // blocks_per_sample, 0, blocks_per_sample - 1 - step % blocks_per_sample)

    def const(shape):
        nd = len(shape)
        return pl.BlockSpec(shape, lambda b, t: (0,) * nd)

    def resident(shape):
        nd = len(shape)
        return pl.BlockSpec(shape, lambda b, t: (0,) * nd, pipeline_mode=pl.Buffered(1))

    def per_layer(shape, **kw):
        nd = len(shape)
        return pl.BlockSpec((None,) + shape, lambda b, t: (layer,) + (0,) * nd, **kw)

    in_specs = [
        pl.BlockSpec((None, TQ, D_MODEL), lambda b, t: (*p1(b, t), 0)),
        pl.BlockSpec((None, TQ, D_MODEL), lambda b, t: (*p2(b, t), 0)),
        per_layer((1, D_MODEL)),
        per_layer((D_MODEL, D_IN), pipeline_mode=pl.Buffered(1)),
        per_layer((CONV_W, C_CONV)),
        per_layer((1, C_CONV)),
        per_layer((1, C_CONV)),
        per_layer((1, C_CONV)),
        per_layer((C_POOL, C_POOL)),
        per_layer((1, C_POOL)),
        per_layer((1, C_ATT)),
        per_layer((1, C_ATT)),
        pl.BlockSpec((TQ, LANES), lambda b, t: (p1(b, t)[1], 0)),
        pl.BlockSpec((TQ, LANES), lambda b, t: (p1(b, t)[1], 0)),
        const((C_ATT, C_ATT)),
        per_layer((D_MODEL, D_MODEL), pipeline_mode=pl.Buffered(1)),
        const((nsmp, D_MODEL)),
        const((nsmp, C_ATT)),
        const((nsmp, C_ATT)),
        const((nsmp, C_ATT)),
        const((nsmp, C_CONV + C_POOL)),
        const((nsmp, C_ATT)),
        pl.BlockSpec((None, None, C_ATT, win_blk), sample_block),
        pl.BlockSpec((None, None, C_ATT, win_blk), sample_block),
    ]
    operands = [x, x, norm_g, w_in, conv_w, conv_b, ln_g, ln_b, pool_wbd, pool_scale, qg, kg, cos_t, sin_t, mavg,
                w_out, xs, *decode_new, kt, vt]
    assert len(operands) == N_LAYER_INPUTS
    kv_spec = pl.BlockSpec((None, None, C_ATT, TQ), lambda b, t: (layer, p1(b, t)[0], 0, p1(b, t)[1]))
    out_specs = [
        pl.BlockSpec((None, TQ, D_MODEL), lambda b, t: (*p2(b, t), 0)),
        kv_spec,
        kv_spec,
        pl.BlockSpec((None, CONV_HALO, C_CONV), lambda b, t: (p1(b, t)[0], 0, 0)),
        pl.BlockSpec((None, POOL_BUF, C_POOL), lambda b, t: (p1(b, t)[0], 0, 0)),
        const((nsmp, D_MODEL)),
        pl.BlockSpec((None, None, C_ATT, win_blk), sample_block),
        pl.BlockSpec((None, None, C_ATT, win_blk), sample_block),
    ]
    out_shape = [
        jax.ShapeDtypeStruct((bsz, seq_len, D_MODEL), F32),
        jax.ShapeDtypeStruct((depth, bsz, C_ATT, seq_len), F32),
        jax.ShapeDtypeStruct((depth, bsz, C_ATT, seq_len), F32),
        jax.ShapeDtypeStruct((bsz, CONV_HALO, C_CONV), F32),
        jax.ShapeDtypeStruct((bsz, POOL_BUF, C_POOL), F32),
        jax.ShapeDtypeStruct((nsmp, D_MODEL), F32),
        jax.ShapeDtypeStruct((depth, nsmp, C_ATT, win_len), F32),
        jax.ShapeDtypeStruct((depth, nsmp, C_ATT, win_len), F32),
    ]
    static = dict(seq_len=seq_len, win_len=win_len)
    if kv_prev is None:
        kern = functools.partial(_layer_kernel, **static)
        aliases = {}
    else:
        kern = functools.partial(_layer_kernel_with_alias, n_alias=len(kv_prev), **static)
        in_specs += [pl.BlockSpec(memory_space=pl.ANY)] * len(kv_prev)
        operands += list(kv_prev)
        aliases = {N_LAYER_INPUTS + i: o for i, o in enumerate((1, 2, 6, 7))}
    scratch = [
        pltpu.VMEM((TQ, D_MODEL), BF16),
        pltpu.VMEM((TQ, D_IN), F32),
        pltpu.VMEM((N_CONV_SLAB, TQ + U_HALO, LANES), F32),
        pltpu.VMEM((N_POOL_SLAB, TQ + B_HALO, LANES), F32),
        *[pltpu.VMEM((N_ATT_SLAB, TQ, LANES), F32) for _ in range(4)],
        pltpu.VMEM((N_CONV_SLAB, TQ, LANES), F32),
        *[pltpu.VMEM((N_ATT_SLAB, seq_len, LANES), F32) for _ in range(AB_COUNT)],
        pltpu.VMEM((seq_len, D_MODEL), BF16),
        pltpu.VMEM((nsmp, C_ATT), F32),
        pltpu.VMEM((C_ATT, LANES), F32),
        pltpu.VMEM((N_HEADS, LANES), F32),
        pltpu.VMEM((N_HEADS, LANES), F32),
        pltpu.VMEM((C_ATT, LANES), F32),
        pltpu.VMEM((C_ATT, LANES), F32),
        pltpu.VMEM((C_ATT, LANES), F32),
    ]
    return pl.pallas_call(
        kern,
        out_shape=out_shape,
        grid=(bsz + 1, nt),
        in_specs=in_specs,
        out_specs=out_specs,
        scratch_shapes=scratch,
        input_output_aliases=aliases,
        compiler_params=pltpu.CompilerParams(
            dimension_semantics=("arbitrary", "arbitrary"),
            vmem_limit_bytes=VMEM_LIMIT_BYTES,
        ),
        name="trunk_layer",
    )(*operands)


def _row_to_col_tile(row):
    return jnp.broadcast_to(row, (LANES, row.shape[1])).T


def _key_multiplicity(lane0, blk, win_len):
    t = lane0 + lax.broadcasted_iota(jnp.int32, (1, blk), 1)
    delta = win_len - t
    cnt = jnp.zeros((1, blk), F32)
    for d in DILATIONS:
        hit = jnp.logical_and(delta % d == 0, delta <= d * WINDOW_KEYS)
        cnt = cnt + hit.astype(F32)
    return cnt


def _decode_prepare_kernel(x_ref, sc_ref, sp_ref,
                           ng_ref, win_ref, cw_ref, cb_ref, lng_ref, lnb_ref, pw_ref, psc_ref, qg_ref, kg_ref,
                           cos_ref, sin_ref, mavg_ref,
                           nc_ref, np_ref, qr_s, kr_s, v_s, mixab_s, cg_s, *, pos):
    nsmp = x_ref.shape[0]

    x = x_ref[...]
    ms = jnp.mean(x * x, axis=-1, keepdims=True)
    h = (x * lax.rsqrt(ms + EPS) * ng_ref[...]).astype(BF16)
    proj = jnp.dot(h, win_ref[...], preferred_element_type=F32)

    u = proj[:, OFF_A_VAL:OFF_A_VAL + C_CONV] * _sigmoid(proj[:, OFF_A_GLU:OFF_A_GLU + C_CONV])
    conv = u * cw_ref[CONV_HALO:CONV_W, :] + cb_ref[...]
    for w in range(CONV_HALO):
        conv = conv + sc_ref[w] * cw_ref[w:w + 1, :]
    nc_ref[0:CONV_HALO - 1] = sc_ref[1:CONV_HALO]
    nc_ref[CONV_HALO - 1] = u
    mu = jnp.mean(conv, axis=-1, keepdims=True)
    cen = conv - mu
    var = jnp.mean(cen * cen, axis=-1, keepdims=True)
    ln = cen * lax.rsqrt(var + EPS) * lng_ref[...] + lnb_ref[...]
    ya = _silu(ln) * _silu(proj[:, OFF_A_GATE:OFF_A_GATE + C_CONV])

    bval = proj[:, OFF_B_VAL:OFF_B_VAL + C_POOL]
    lane_p = lax.broadcasted_iota(jnp.int32, (nsmp, C_POOL), 1)
    pooled = jnp.zeros((nsmp, C_POOL), F32)
    acc = bval
    done = 1
    for wi, w in enumerate(POOL_WINDOWS):
        for i in range(done, w):
            acc = acc + sp_ref[POOL_BUF - i]
        done = w
        pooled = jnp.where(lane_p // POOL_GC == wi, acc / float(min(pos + 1, w)), pooled)
    np_ref[0:POOL_BUF - 1] = sp_ref[1:POOL_BUF]
    np_ref[POOL_BUF - 1] = bval
    dpool = (pooled - bval).astype(BF16)
    yb = (jnp.dot(dpool, pw_ref[...], preferred_element_type=F32) * psc_ref[...]
          * _silu(proj[:, OFF_B_GATE:OFF_B_GATE + C_POOL]))
    mixab_s[:, 0:C_CONV] = ya
    mixab_s[:, C_CONV:C_CONV + C_POOL] = yb
    cg_s[...] = _silu(proj[:, OFF_C_GATE:OFF_C_GATE + C_ATT])

    q = proj[:, OFF_Q:OFF_Q + C_ATT]
    k = proj[:, OFF_K:OFF_K + C_ATT]
    qn = q * lax.rsqrt(jnp.dot((q * q).astype(BF16), mavg_ref[...], preferred_element_type=F32) + EPS) * qg_ref[...]
    kn = k * lax.rsqrt(jnp.dot((k * k).astype(BF16), mavg_ref[...], preferred_element_type=F32) + EPS) * kg_ref[...]
    for s in range(N_ATT_SLAB):
        cols = slice(s * LANES, (s + 1) * LANES)
        qs, ks = qn[:, cols], kn[:, cols]
        qr_s[:, cols] = (qs * cos_ref[...] + _swap_halves(qs) * sin_ref[...]) * (HEAD_DIM ** -0.5)
        kr_s[:, cols] = ks * cos_ref[...] + _swap_halves(ks) * sin_ref[...]
    v_s[...] = proj[:, OFF_V:OFF_V + C_ATT]


def _decode_prepare(layer, pos, xs, state_conv, state_pool, norm_g, w_in, conv_w, conv_b, ln_g, ln_b,
                    pool_wbd, pool_scale, qg, kg, cos_s, sin_s, mavg):
    nsmp = xs.shape[0]

    def const(shape):
        nd = len(shape)
        return pl.BlockSpec(shape, lambda i: (0,) * nd)

    def per_layer(shape):
        nd = len(shape)
        return pl.BlockSpec((None,) + shape, lambda i: (layer,) + (0,) * nd)

    in_specs = [
        const((nsmp, D_MODEL)), per_layer((CONV_HALO, nsmp, C_CONV)), per_layer((POOL_BUF, nsmp, C_POOL)),
        per_layer((1, D_MODEL)), per_layer((D_MODEL, D_IN)), per_layer((CONV_W, C_CONV)), per_layer((1, C_CONV)),
        per_layer((1, C_CONV)), per_layer((1, C_CONV)), per_layer((C_POOL, C_POOL)), per_layer((1, C_POOL)),
        per_layer((1, C_ATT)), per_layer((1, C_ATT)), const((1, LANES)), const((1, LANES)), const((C_ATT, C_ATT)),
    ]
    out_dims = [(CONV_HALO, nsmp, C_CONV), (POOL_BUF, nsmp, C_POOL), (nsmp, C_ATT), (nsmp, C_ATT), (nsmp, C_ATT),
                (nsmp, C_CONV + C_POOL), (nsmp, C_ATT)]
    return pl.pallas_call(
        functools.partial(_decode_prepare_kernel, pos=pos),
        out_shape=[jax.ShapeDtypeStruct(d, F32) for d in out_dims],
        grid=(1,),
        in_specs=in_specs,
        out_specs=[const(d) for d in out_dims],
        compiler_params=pltpu.CompilerParams(dimension_semantics=("arbitrary",)),
        name="decode_prepare",
    )(xs, state_conv, state_pool, norm_g, w_in, conv_w, conv_b, ln_g, ln_b, pool_wbd, pool_scale, qg, kg,
      cos_s, sin_s, mavg)


def _sample_start(cond, smp, qr_s, kr_s, v_s, qcol_s, sm_s, sl_s, so_s, kcar_s, vcar_s):
    nsmp = qr_s.shape[0]
    n_pat = float(len(DILATIONS))

    @pl.when(cond)
    def _start():
        mine = lax.broadcasted_iota(jnp.int32, (nsmp, C_ATT), 0) == smp

        def col_tile(ref):
            return _row_to_col_tile(jnp.sum(jnp.where(mine, ref[...], 0.0), axis=0, keepdims=True))

        q_col = col_tile(qr_s)
        k_col = col_tile(kr_s)
        v_col = col_tile(v_s)
        qcol_s[...] = q_col
        kcar_s[...] = k_col
        vcar_s[...] = v_col
        so_s[...] = v_col * n_pat
        for hd in range(N_HEADS):
            hr = slice(hd * HEAD_DIM, (hd + 1) * HEAD_DIM)
            sm_s[hd:hd + 1, :] = jnp.sum(k_col[hr, :] * q_col[hr, :], axis=0, keepdims=True)
        sl_s[...] = jnp.full(sl_s.shape, n_pat, F32)


def _sample_window(lane0, kt_ref, vt_ref, okt_ref, ovt_ref, qcol_s, sm_s, sl_s, so_s, kcar_s, vcar_s, *, win_len):
    blk = kt_ref.shape[1]
    n_tiles = blk // LANES
    cnt = _key_multiplicity(lane0, blk, win_len)
    reach = cnt > 0.0
    last_lane = lax.broadcasted_iota(jnp.int32, (HEAD_DIM, blk), 1) == blk - 1

    for hd in range(N_HEADS):
        hr = slice(hd * HEAD_DIM, (hd + 1) * HEAD_DIM)
        kt = kt_ref[hr, :]
        vt = vt_ref[hr, :]
        s_win = jnp.sum(kt * jnp.concatenate([qcol_s[hr, :]] * n_tiles, axis=1), axis=0, keepdims=True)
        s_win = jnp.where(reach, s_win, NEG)
        m_old = sm_s[hd:hd + 1, :]
        m_new = jnp.maximum(m_old, jnp.max(s_win, axis=-1, keepdims=True))
        w_old = jnp.exp(m_old - m_new)
        p_win = cnt * jnp.exp(s_win - m_new[:, 0:1])
        sm_s[hd:hd + 1, :] = m_new
        sl_s[hd:hd + 1, :] = sl_s[hd:hd + 1, :] * w_old + jnp.sum(p_win, axis=-1, keepdims=True)
        so_s[hr, :] = so_s[hr, :] * w_old + jnp.sum(vt * p_win, axis=-1, keepdims=True)
        k_next = jnp.concatenate([kcar_s[hr, :]] * n_tiles, axis=1)
        v_next = jnp.concatenate([vcar_s[hr, :]] * n_tiles, axis=1)
        okt_ref[hr, :] = jnp.where(last_lane, k_next, pltpu.roll(kt, blk - 1, 1))
        ovt_ref[hr, :] = jnp.where(last_lane, v_next, pltpu.roll(vt, blk - 1, 1))
        kcar_s[hr, :] = jnp.broadcast_to(kt[:, 0:1], (HEAD_DIM, LANES))
        vcar_s[hr, :] = jnp.broadcast_to(vt[:, 0:1], (HEAD_DIM, LANES))


def _sample_done(cond, smp, att_s, sl_s, so_s):
    nsmp = att_s.shape[0]

    @pl.when(cond)
    def _done():
        cols = [so_s[hd * HEAD_DIM:(hd + 1) * HEAD_DIM, :] / sl_s[hd:hd + 1, :] for hd in range(N_HEADS)]
        att_row = jnp.concatenate(cols, axis=0).T[0:1, :]
        mine = lax.broadcasted_iota(jnp.int32, (nsmp, C_ATT), 0) == smp
        att_s[...] = jnp.where(mine, att_row, att_s[...])


def _sample_finish(cond, x_ref, wout_ref, y_ref, att_s, mixab_s, cg_s):
    @pl.when(cond)
    def _finish():
        yc = att_s[...] * cg_s[...]
        mixed = jnp.concatenate([mixab_s[...], yc], axis=-1).astype(BF16)
        y_ref[...] = x_ref[...] + jnp.dot(mixed, wout_ref[...], preferred_element_type=F32)


def _to_channel_major(a):
    depth, bsz, ntok, nh, hd = a.shape
    return jnp.transpose(a, (0, 1, 3, 4, 2)).reshape(depth, bsz, nh * hd, ntok)


def _from_channel_major(a):
    depth, bsz, _, ntok = a.shape
    return jnp.transpose(a.reshape(depth, bsz, N_HEADS, HEAD_DIM, ntok), (0, 1, 4, 2, 3))


def kernel(x_prompt, x_sample, state_conv, state_pool, cache_k_win, cache_v_win, norm_g, w_in, conv_w, conv_b,
           ln_g, ln_b, pool_w, pool_scale, q_norm_g, k_norm_g, w_out):
    depth = w_in.shape[0]
    seq_len = x_prompt.shape[1]
    nsmp = x_sample.shape[0]

    cos_p, sin_p = _rope_tables(jnp.arange(seq_len, dtype=jnp.int32))
    cos_s, sin_s = _rope_tables(jnp.full((1,), PAST_LEN, dtype=jnp.int32))
    mavg = _head_mean_matrix()
    weights = (norm_g[:, None], w_in.astype(BF16), conv_w, conv_b[:, None], ln_g[:, None], ln_b[:, None],
               _pool_block_diag(pool_w).astype(BF16), pool_scale[:, None],
               jnp.tile(q_norm_g, (1, N_HEADS))[:, None], jnp.tile(k_norm_g, (1, N_HEADS))[:, None])
    w_out_b = w_out.astype(BF16)

    kt = _to_channel_major(cache_k_win)
    vt = _to_channel_major(cache_v_win)
    xp = x_prompt
    xs = x_sample.reshape(nsmp, D_MODEL)
    state_conv_pm = jnp.swapaxes(state_conv, 1, 2)
    state_pool_pm = jnp.swapaxes(state_pool, 1, 2)
    kv = None
    conv_p, pool_p, conv_s, pool_s = [], [], [], []
    for layer in range(depth):
        ncs, nps, *decode_new = _decode_prepare(layer, PAST_LEN, xs, state_conv_pm, state_pool_pm, *weights,
                                                cos_s, sin_s, mavg)
        xp, kp, vp, cst, pst, xs, ks, vs = _trunk_layer(
            layer, xp, xs, decode_new, kt, vt, *weights, cos_p, sin_p, mavg, w_out_b, kv_prev=kv)
        kv = (kp, vp, ks, vs)
        conv_p.append(cst)
        pool_p.append(pst)
        conv_s.append(ncs)
        pool_s.append(nps)

    return (xp, xs.reshape(nsmp, 1, D_MODEL), jnp.stack(conv_p), jnp.stack(pool_p),
            _from_channel_major(kv[0]), _from_channel_major(kv[1]),
            jnp.swapaxes(jnp.stack(conv_s), 1, 2), jnp.swapaxes(jnp.stack(pool_s), 1, 2),
            _from_channel_major(kv[2]), _from_channel_major(kv[3]))
```

```python
import functools

import jax
import jax.numpy as jnp
import numpy as np
from jax import lax
from jax.experimental import pallas as pl
from jax.experimental.pallas import tpu as pltpu

F32 = jnp.float32
BF16 = jnp.bfloat16

D_MODEL = 1024
C_CONV = 384
C_POOL = 256
C_ATT = 384
HEAD_DIM = 64
N_HEADS = C_ATT // HEAD_DIM
CONV_W = 31
CONV_HALO = CONV_W - 1
POOL_WINDOWS = (2, 4, 8, 16)
POOL_GC = 64
POOL_BUF = 15
DILATIONS = (1, 4, 16)
WINDOW_KEYS = 128
EPS = 1e-6
ROPE_THETA = 10000.0
D_IN = 3 * C_CONV + 2 * C_POOL + 4 * C_ATT
PAST_LEN = 16384
NEG = -1e30
LOG2_E = 1.4426950408889634

OFF_A_VAL = 0
OFF_A_GLU = OFF_A_VAL + C_CONV
OFF_A_GATE = OFF_A_GLU + C_CONV
OFF_B_VAL = OFF_A_GATE + C_CONV
OFF_B_GATE = OFF_B_VAL + C_POOL
OFF_Q = OFF_B_GATE + C_POOL
OFF_K = OFF_Q + C_ATT
OFF_V = OFF_K + C_ATT
OFF_C_GATE = OFF_V + C_ATT

LANES = 128
N_CONV_SLAB = C_CONV // LANES
N_POOL_SLAB = C_POOL // LANES
N_ATT_SLAB = C_ATT // LANES
VMEM_LIMIT_BYTES = 60 * 1024 * 1024

TQ = 256
RC = 32
CONV_GROUP = 8
U_HALO = 32
B_HALO = 16
QB = WINDOW_KEYS
UNROLL_NEAR = 15
UNROLL_MID = 12
UNROLL_FAR = 16

AB_Q, AB_K, AB_V, AB_O, AB_LSE = range(5)
AB_COUNT = 5


def _sigmoid(x):
    return 0.5 * jnp.tanh(0.5 * x) + 0.5


def _silu(x):
    h = 0.5 * x
    return h * jnp.tanh(h) + h


def _rope_tables(positions):
    half = HEAD_DIM // 2
    inv = ROPE_THETA ** (-jnp.arange(half, dtype=F32) / half)
    ang = positions.astype(F32)[:, None] * inv[None, :]
    cos = jnp.cos(ang)
    sin = jnp.sin(ang)
    cos_h = jnp.concatenate([cos, cos], axis=-1)
    sin_h = jnp.concatenate([-sin, sin], axis=-1)
    reps = LANES // HEAD_DIM
    return jnp.tile(cos_h, (1, reps)), jnp.tile(sin_h, (1, reps))


def _head_mean_matrix():
    idx = np.arange(C_ATT) // HEAD_DIM
    return jnp.asarray((idx[:, None] == idx[None, :]).astype(np.float32) / HEAD_DIM, dtype=BF16)


def _pool_block_diag(pool_w):
    out = jnp.zeros((pool_w.shape[0], C_POOL, C_POOL), pool_w.dtype)
    for g in range(len(POOL_WINDOWS)):
        out = out.at[:, g * POOL_GC:(g + 1) * POOL_GC, g * POOL_GC:(g + 1) * POOL_GC].set(pool_w[:, g])
    return out


def _swap_halves(x):
    lane = lax.broadcasted_iota(jnp.int32, x.shape, 1)
    first_half = (lane % HEAD_DIM) < (HEAD_DIM // 2)
    return jnp.where(first_half, pltpu.roll(x, LANES - HEAD_DIM // 2, 1), pltpu.roll(x, HEAD_DIM // 2, 1))


def _pool_means(loads, pos):
    lane = lax.broadcasted_iota(jnp.int32, loads(0, 0).shape, 1)
    lo = lane < POOL_GC
    posf = (pos + 1).astype(F32)
    outs = []
    for slab in range(N_POOL_SLAB):
        w_lo, w_hi = POOL_WINDOWS[2 * slab], POOL_WINDOWS[2 * slab + 1]
        cur = loads(0, slab)
        s = cur
        for i in range(1, w_lo):
            s = s + loads(i, slab)
        s_lo = s
        for i in range(w_lo, w_hi):
            s = s + loads(i, slab)
        s_hi = s
        cnt_lo = jnp.minimum(posf, float(w_lo))
        cnt_hi = jnp.minimum(posf, float(w_hi))
        pooled = jnp.where(lo, s_lo / cnt_lo, s_hi / cnt_hi)
        outs.append(pooled - cur)
    return outs


N_LAYER_INPUTS = 24


def _layer_kernel(x_ref, x2_ref, ng_ref, win_ref, cw_ref, cb_ref, lng_ref, lnb_ref, pw_ref, psc_ref,
                  qg_ref, kg_ref, cos_ref, sin_ref, mavg_ref, wout_ref,
                  xs_ref, qr_s, kr_s, v_s, mixab_s, cgs_s, kt_ref, vt_ref,
                  y_ref, ko_ref, vo_ref, cst_ref, pst_ref,
                  ys_ref, okt_ref, ovt_ref,
                  h_s, proj, u_buf, b_buf,
                  st_q, st_k, st_v, st_att, ab0, ab1, ab2, ab3, ab4, mix,
                  att_s, qcol_s, sm_s, sl_s, so_s, kcar_s, vcar_s,
                  *, seq_len, win_len):
    ab = (ab0, ab1, ab2, ab3, ab4)
    nt = seq_len // TQ
    b = pl.program_id(0)
    t = pl.program_id(1)
    n_batch = pl.num_programs(0) - 1
    blocks_per_sample = win_len // kt_ref.shape[1]
    row0 = pl.multiple_of(t * TQ, TQ)
    sq_s = h_s.at[:, 0:2 * C_ATT]
    d_s = h_s.at[:, 2 * C_ATT:2 * C_ATT + C_POOL]
    msq_s = proj.at[:, OFF_A_VAL:OFF_A_VAL + 2 * C_ATT]
    yb_s = proj.at[:, OFF_B_VAL:OFF_B_VAL + C_POOL]
    conv_s = st_att

    step = b * nt + t
    part = step % blocks_per_sample
    smp = step // blocks_per_sample

    def sample_start():
        _sample_start(part == 0, smp, qr_s, kr_s, v_s, qcol_s, sm_s, sl_s, so_s, kcar_s, vcar_s)

    def sample_window():
        _sample_window((blocks_per_sample - 1 - part) * kt_ref.shape[1], kt_ref, vt_ref, okt_ref, ovt_ref,
                       qcol_s, sm_s, sl_s, so_s, kcar_s, vcar_s, win_len=win_len)

    def sample_done():
        _sample_done(part == blocks_per_sample - 1, smp, att_s, sl_s, so_s)

    def phase2():
        d_mid = DILATIONS[1]
        per_stream = TQ // d_mid
        for s in range(N_ATT_SLAB):
            for r in range(d_mid):
                src_rows = pl.ds(pl.multiple_of(r * (seq_len // d_mid) + t * per_stream, per_stream), per_stream)
                st_att[s, pl.ds(r, per_stream, stride=d_mid), :] = ab[AB_O][s, src_rows, :]

        def att_chunk(i, c):
            r = pl.multiple_of(i * RC, RC)
            grow = pl.ds(pl.multiple_of(row0 + r, RC), RC)
            for s in range(N_ATT_SLAB):
                mcols = slice(C_CONV + C_POOL + s * LANES, C_CONV + C_POOL + (s + 1) * LANES)
                yc = st_att[s, pl.ds(r, RC), :] * mix[grow, mcols].astype(F32)
                mix[grow, mcols] = yc.astype(BF16)
            return c

        lax.fori_loop(0, TQ // RC, att_chunk, 0, unroll=True)
        y_ref[...] = x2_ref[...] + jnp.dot(mix[pl.ds(row0, TQ), :], wout_ref[...], preferred_element_type=F32)

    def phase1(with_phase2):
        @pl.when(t == 0)
        def _zero_halo():
            u_buf[:, 0:U_HALO, :] = jnp.zeros((N_CONV_SLAB, U_HALO, LANES), F32)
            b_buf[:, 0:B_HALO, :] = jnp.zeros((N_POOL_SLAB, B_HALO, LANES), F32)
            if not with_phase2:
                att_s[...] = jnp.zeros(att_s.shape, F32)

        sample_start()
        if with_phase2:
            phase2()
        sample_window()

        def norm_chunk(i, c):
            r = pl.multiple_of(i * RC, RC)
            x = x_ref[pl.ds(r, RC), :]
            ms = jnp.mean(x * x, axis=-1, keepdims=True)
            h_s[pl.ds(r, RC), :] = (x * lax.rsqrt(ms + EPS) * ng_ref[...]).astype(BF16)
            return c

        lax.fori_loop(0, TQ // RC, norm_chunk, 0, unroll=True)
        proj[...] = jnp.dot(h_s[...], win_ref[...], preferred_element_type=F32)

        def split_chunk(i, c):
            r = pl.multiple_of(i * RC, RC)
            rows = pl.ds(r, RC)
            grow = pl.ds(pl.multiple_of(row0 + r, RC), RC)
            for s in range(N_CONV_SLAB):
                cols = slice(s * LANES, (s + 1) * LANES)
                a_val = proj[rows, OFF_A_VAL + s * LANES:OFF_A_VAL + (s + 1) * LANES]
                a_glu = proj[rows, OFF_A_GLU + s * LANES:OFF_A_GLU + (s + 1) * LANES]
                u_buf[s, pl.ds(U_HALO + r, RC), :] = a_val * _sigmoid(a_glu)
                q = proj[rows, OFF_Q + s * LANES:OFF_Q + (s + 1) * LANES]
                k = proj[rows, OFF_K + s * LANES:OFF_K + (s + 1) * LANES]
                sq_s[rows, cols] = (q * q).astype(BF16)
                sq_s[rows, C_ATT + s * LANES:C_ATT + (s + 1) * LANES] = (k * k).astype(BF16)
                st_v[s, rows, :] = proj[rows, OFF_V + s * LANES:OFF_V + (s + 1) * LANES]
                mix[grow, C_CONV + C_POOL + s * LANES:C_CONV + C_POOL + (s + 1) * LANES] = _silu(
                    proj[rows, OFF_C_GATE + s * LANES:OFF_C_GATE + (s + 1) * LANES]).astype(BF16)
            for s in range(N_POOL_SLAB):
                b_buf[s, pl.ds(B_HALO + r, RC), :] = proj[rows, OFF_B_VAL + s * LANES:OFF_B_VAL + (s + 1) * LANES]
            pos = row0 + r + lax.broadcasted_iota(jnp.int32, (RC, 1), 0)
            dl = _pool_means(lambda sh, s: b_buf[s, pl.ds(r + B_HALO - sh, RC), :], pos)
            for s in range(N_POOL_SLAB):
                d_s[rows, s * LANES:(s + 1) * LANES] = dl[s].astype(BF16)
            return c

        lax.fori_loop(0, TQ // RC, split_chunk, 0, unroll=True)
        msq_s[:, 0:C_ATT] = jnp.dot(sq_s[:, 0:C_ATT], mavg_ref[...], preferred_element_type=F32)
        msq_s[:, C_ATT:2 * C_ATT] = jnp.dot(sq_s[:, C_ATT:2 * C_ATT], mavg_ref[...], preferred_element_type=F32)
        yb_s[...] = jnp.dot(d_s[...], pw_ref[...], preferred_element_type=F32)

        row_groups = TQ // CONV_GROUP
        for s in range(N_CONV_SLAB):
            cols = slice(s * LANES, (s + 1) * LANES)
            acc = [jnp.broadcast_to(cb_ref[:, cols], (CONV_GROUP, LANES))] * row_groups
            for a in range(CONV_GROUP):
                taps = list(range(a, CONV_W, CONV_GROUP))
                cws = [jnp.broadcast_to(cw_ref[w:w + 1, cols], (CONV_GROUP, LANES)) for w in taps]
                for j in range(row_groups + len(taps) - 1):
                    start = (U_HALO - CONV_HALO) + a + CONV_GROUP * j
                    window = u_buf[s, start:start + CONV_GROUP, :]
                    for m in range(len(taps)):
                        if 0 <= j - m < row_groups:
                            acc[j - m] = acc[j - m] + window * cws[m]
            for g in range(row_groups):
                conv_s[s, g * CONV_GROUP:(g + 1) * CONV_GROUP, :] = acc[g]

        def mixer_chunk(i, c):
            r = pl.multiple_of(i * RC, RC)
            rows = pl.ds(r, RC)
            grow = pl.ds(pl.multiple_of(row0 + r, RC), RC)
            conv = [conv_s[s, rows, :] for s in range(N_CONV_SLAB)]
            mu = jnp.sum(conv[0] + conv[1] + conv[2], axis=-1, keepdims=True) * (1.0 / C_CONV)
            cen = [cv - mu for cv in conv]
            var = jnp.sum(cen[0] * cen[0] + cen[1] * cen[1] + cen[2] * cen[2], axis=-1, keepdims=True) * (1.0 / C_CONV)
            rstd = lax.rsqrt(var + EPS)
            for s in range(N_CONV_SLAB):
                cols = slice(s * LANES, (s + 1) * LANES)
                ln = cen[s] * rstd * lng_ref[:, cols] + lnb_ref[:, cols]
                gate = proj[rows, OFF_A_GATE + s * LANES:OFF_A_GATE + (s + 1) * LANES]
                mix[grow, cols] = (_silu(ln) * _silu(gate)).astype(BF16)
            for s in range(N_POOL_SLAB):
                cols = slice(s * LANES, (s + 1) * LANES)
                gate = proj[rows, OFF_B_GATE + s * LANES:OFF_B_GATE + (s + 1) * LANES]
                yb = yb_s[rows, cols] * psc_ref[:, cols] * _silu(gate)
                mix[grow, C_CONV + s * LANES:C_CONV + (s + 1) * LANES] = yb.astype(BF16)
            cos = cos_ref[rows, :]
            sin = sin_ref[rows, :]
            for s in range(N_ATT_SLAB):
                cols = slice(s * LANES, (s + 1) * LANES)
                q = proj[rows, OFF_Q + s * LANES:OFF_Q + (s + 1) * LANES]
                qn = q * lax.rsqrt(msq_s[rows, cols] + EPS) * qg_ref[:, cols]
                qr = qn * cos + _swap_halves(qn) * sin
                st_q[s, rows, :] = qr * (HEAD_DIM ** -0.5 * LOG2_E)
                k = proj[rows, OFF_K + s * LANES:OFF_K + (s + 1) * LANES]
                kn = k * lax.rsqrt(msq_s[rows, C_ATT + s * LANES:C_ATT + (s + 1) * LANES] + EPS) * kg_ref[:, cols]
                st_k[s, rows, :] = kn * cos + _swap_halves(kn) * sin
            return c

        lax.fori_loop(0, TQ // RC, mixer_chunk, 0, unroll=True)

        d_mid = DILATIONS[1]
        per_stream = TQ // d_mid
        for s in range(N_ATT_SLAB):
            ko_ref[s * LANES:(s + 1) * LANES, :] = st_k[s].T
            vo_ref[s * LANES:(s + 1) * LANES, :] = st_v[s].T
            for st, dst in ((st_q, AB_Q), (st_k, AB_K), (st_v, AB_V)):
                for r in range(d_mid):
                    dst_rows = pl.ds(pl.multiple_of(r * (seq_len // d_mid) + t * per_stream, per_stream), per_stream)
                    ab[dst][s, dst_rows, :] = st[s, pl.ds(r, per_stream, stride=d_mid), :]

        @pl.when(t == nt - 1)
        def _write_state():
            for s in range(N_CONV_SLAB):
                cst_ref[:, s * LANES:(s + 1) * LANES] = u_buf[s, TQ + U_HALO - CONV_HALO:TQ + U_HALO, :]
            for s in range(N_POOL_SLAB):
                pst_ref[:, s * LANES:(s + 1) * LANES] = b_buf[s, TQ + B_HALO - POOL_BUF:TQ + B_HALO, :]

        u_buf[:, 0:U_HALO, :] = u_buf[:, TQ:TQ + U_HALO, :]
        b_buf[:, 0:B_HALO, :] = b_buf[:, TQ:TQ + B_HALO, :]
        sample_done()

    def attention():
        d_mid, d_far = DILATIONS[1], DILATIONS[2]
        ratio = d_far // d_mid
        stream_len = seq_len // d_mid
        piece = QB // d_mid
        lane = lax.broadcasted_iota(jnp.int32, (QB, LANES), 1)
        rowi = lax.broadcasted_iota(jnp.int32, (QB, LANES), 0)
        lo = lane < HEAD_DIM

        def masks(key_pos, query_pos):
            cur = key_pos <= query_pos
            prev = key_pos >= query_pos
            cur2 = jnp.concatenate([cur, cur], axis=0)
            return cur2, jnp.concatenate([jnp.concatenate([prev, prev], axis=0), cur2], axis=1)

        cur_ok2, prev_cur_ok2 = masks(lane, rowi)
        ncur_ok2, nprev_cur_ok2 = masks(d_mid * (lane % piece) + lane // piece, d_mid * (rowi % piece) + rowi // piece)

        def attend(q, keys, vals, mask):
            qa = jnp.where(lo, q, 0.0).astype(BF16)
            qb = jnp.where(lo, 0.0, q).astype(BF16)
            q2 = jnp.concatenate([qa, qb], axis=0)
            sc = lax.dot_general(q2, keys.astype(BF16), (((1,), (1,)), ((), ())), preferred_element_type=F32)
            sc = jnp.where(mask, sc, NEG)
            m = jnp.max(sc, axis=-1, keepdims=True)
            p = jnp.exp2(sc - m).astype(BF16)
            v1 = jnp.concatenate([vals.astype(BF16), jnp.ones(vals.shape, BF16)], axis=1)
            ol = jnp.dot(p, v1, preferred_element_type=F32)
            o_u = jnp.where(lo, ol[0:QB, 0:LANES], ol[QB:2 * QB, 0:LANES])
            l_u = jnp.where(lo, ol[0:QB, LANES:2 * LANES], ol[QB:2 * QB, LANES:2 * LANES])
            m_u = jnp.where(lo, m[0:QB], m[QB:2 * QB])
            return o_u, m_u, l_u

        def load(ref, s, pieces):
            tiles = [ref[s, p, :] for p in pieces]
            return tiles[0] if len(tiles) == 1 else jnp.concatenate(tiles, axis=0)

        def save(ref, s, pieces, val):
            n = val.shape[0] // len(pieces)
            for j, p in enumerate(pieces):
                ref[s, p, :] = val[j * n:(j + 1) * n]

        def block(rows, krows, mask, merge, with_lse=True):
            outs = []
            for s in range(N_ATT_SLAB):
                o_u, m_u, l_u = attend(load(ab[AB_Q], s, rows), load(ab[AB_K], s, krows), load(ab[AB_V], s, krows),
                                       mask)
                if merge:
                    lse_old = load(ab[AB_LSE], s, rows)
                    m_new = jnp.maximum(lse_old, m_u)
                    w_old = jnp.exp2(lse_old - m_new)
                    w_u = jnp.exp2(m_u - m_new)
                    den = w_old + l_u * w_u
                    o_n = (load(ab[AB_O], s, rows) * w_old + o_u * w_u) * (1.0 / den)
                else:
                    m_new, den = m_u, l_u
                    o_n = o_u * (1.0 / l_u)
                outs.append((o_n, m_new + jnp.log2(den) if with_lse else None))
            return outs

        def store(rows, outs):
            for s, (o_n, lse_n) in enumerate(outs):
                save(ab[AB_O], s, rows, o_n)
                if lse_n is not None:
                    save(ab[AB_LSE], s, rows, lse_n)

        def token_block(base):
            return [pl.ds(r * stream_len + base, piece) for r in range(d_mid)]

        store(token_block(0), block(token_block(0), token_block(0), ncur_ok2, merge=False))

        def near_unit(i, c):
            base = pl.multiple_of(i * piece, piece)
            rows = token_block(base)
            store(rows, block(rows, token_block(base - piece) + rows, nprev_cur_ok2, merge=False))
            return c

        lax.fori_loop(1, seq_len // QB, near_unit, 0, unroll=UNROLL_NEAR)

        def mid_first(ph, c):
            rows = [pl.ds(pl.multiple_of(ph * stream_len, QB), QB)]
            store(rows, block(rows, rows, cur_ok2, merge=True))
            return c

        lax.fori_loop(0, d_mid, mid_first, 0, unroll=2)
        later_blocks = stream_len // QB - 1

        def mid_unit(i, c):
            start = pl.multiple_of((i // later_blocks) * stream_len + (i % later_blocks + 1) * QB, QB)
            rows = [pl.ds(start, QB)]
            store(rows, block(rows, [pl.ds(start - QB, 2 * QB)], prev_cur_ok2, merge=True))
            return c

        lax.fori_loop(0, d_mid * later_blocks, mid_unit, 0, unroll=UNROLL_MID)

        def far_unit(r, c):
            rows = [pl.ds((r % d_mid) * stream_len + r // d_mid, QB, stride=ratio)]
            store(rows, block(rows, rows, cur_ok2, merge=True, with_lse=False))
            return c

        lax.fori_loop(0, d_far, far_unit, 0, unroll=UNROLL_FAR)

    @pl.when(b == 0)
    def _first_row():
        phase1(with_phase2=False)

    @pl.when(jnp.logical_and(b > 0, b < n_batch))
    def _steady_rows():
        phase1(with_phase2=True)

    @pl.when(b == n_batch)
    def _last_row():
        phase2()
        _sample_finish(t == nt - 1, xs_ref, wout_ref, ys_ref, att_s, mixab_s, cgs_s)

    pl.when(jnp.logical_and(t == nt - 1, b < n_batch))(attention)


def _layer_kernel_with_alias(*refs, n_alias, **kw):
    _layer_kernel(*refs[:N_LAYER_INPUTS], *refs[N_LAYER_INPUTS + n_alias:], **kw)


def _trunk_layer(layer, x, xs, decode_new, kt, vt,
                 norm_g, w_in, conv_w, conv_b, ln_g, ln_b, pool_wbd, pool_scale, qg, kg,
                 cos_t, sin_t, mavg, w_out, kv_prev=None):
    bsz, seq_len, _ = x.shape
    nsmp = xs.shape[0]
    depth, _, _, win_len = kt.shape
    assert seq_len % TQ == 0 and seq_len == QB * DILATIONS[-1] and DILATIONS[0] == 1
    assert win_len == WINDOW_KEYS * DILATIONS[-1]
    nt = seq_len // TQ
    blocks_per_sample, rem = divmod(bsz * nt, nsmp)
    assert rem == 0 and blocks_per_sample >= 1 and win_len % (blocks_per_sample * LANES) == 0
    win_blk = win_len // blocks_per_sample

    def p1(b, t):
        return jnp.minimum(b, bsz - 1), jnp.where(b == bsz, nt - 1, t)

    def p2(b, t):
        return jnp.maximum(b - 1, 0), jnp.where(b == 0, 0, t)

    def sample_block(b, t):
        pb, pt = p1(b, t)
        step = pb * nt + pt
        return (layer, step // blocks_per_sample, 0, blocks_per_sample - 1 - step % blocks_per_sample)

    def const(shape):
        nd = len(shape)
        return pl.BlockSpec(shape, lambda b, t: (0,) * nd)

    def resident(shape):
        nd = len(shape)
        return pl.BlockSpec(shape, lambda b, t: (0,) * nd, pipeline_mode=pl.Buffered(1))

    def per_layer(shape, **kw):
        nd = len(shape)
        return pl.BlockSpec((None,) + shape, lambda b, t: (layer,) + (0,) * nd, **kw)

    in_specs = [
        pl.BlockSpec((None, TQ, D_MODEL), lambda b, t: (*p1(b, t), 0)),
        pl.BlockSpec((None, TQ, D_MODEL), lambda b, t: (*p2(b, t), 0)),
        per_layer((1, D_MODEL)),
        per_layer((D_MODEL, D_IN), pipeline_mode=pl.Buffered(1)),
        per_layer((CONV_W, C_CONV)),
        per_layer((1, C_CONV)),
        per_layer((1, C_CONV)),
        per_layer((1, C_CONV)),
        per_layer((C_POOL, C_POOL)),
        per_layer((1, C_POOL)),
        per_layer((1, C_ATT)),
        per_layer((1, C_ATT)),
        pl.BlockSpec((TQ, LANES), lambda b, t: (p1(b, t)[1], 0)),
        pl.BlockSpec((TQ, LANES), lambda b, t: (p1(b, t)[1], 0)),
        const((C_ATT, C_ATT)),
        per_layer((D_MODEL, D_MODEL), pipeline_mode=pl.Buffered(1)),
        const((nsmp, D_MODEL)),
        const((nsmp, C_ATT)),
        const((nsmp, C_ATT)),
        const((nsmp, C_ATT)),
        const((nsmp, C_CONV + C_POOL)),
        const((nsmp, C_ATT)),
        pl.BlockSpec((None, None, C_ATT, win_blk), sample_block),
        pl.BlockSpec((None, None, C_ATT, win_blk), sample_block),
    ]
    operands = [x, x, norm_g, w_in, conv_w, conv_b, ln_g, ln_b, pool_wbd, pool_scale, qg, kg, cos_t, sin_t, mavg,
                w_out, xs, *decode_new, kt, vt]
    assert len(operands) == N_LAYER_INPUTS
    kv_spec = pl.BlockSpec((None, None, C_ATT, TQ), lambda b, t: (layer, p1(b, t)[0], 0, p1(b, t)[1]))
    out_specs = [
        pl.BlockSpec((None, TQ, D_MODEL), lambda b, t: (*p2(b, t), 0)),
        kv_spec,
        kv_spec,
        pl.BlockSpec((None, CONV_HALO, C_CONV), lambda b, t: (p1(b, t)[0], 0, 0)),
        pl.BlockSpec((None, POOL_BUF, C_POOL), lambda b, t: (p1(b, t)[0], 0, 0)),
        const((nsmp, D_MODEL)),
        pl.BlockSpec((None, None, C_ATT, win_blk), sample_block),
        pl.BlockSpec((None, None, C_ATT, win_blk), sample_block),
    ]
    out_shape = [
        jax.ShapeDtypeStruct((bsz, seq_len, D_MODEL), F32),
        jax.ShapeDtypeStruct((depth, bsz, C_ATT, seq_len), F32),
        jax.ShapeDtypeStruct((depth, bsz, C_ATT, seq_len), F32),
        jax.ShapeDtypeStruct((bsz, CONV_HALO, C_CONV), F32),
        jax.ShapeDtypeStruct((bsz, POOL_BUF, C_POOL), F32),
        jax.ShapeDtypeStruct((nsmp, D_MODEL), F32),
        jax.ShapeDtypeStruct((depth, nsmp, C_ATT, win_len), F32),
        jax.ShapeDtypeStruct((depth, nsmp, C_ATT, win_len), F32),
    ]
    static = dict(seq_len=seq_len, win_len=win_len)
    if kv_prev is None:
        kern = functools.partial(_layer_kernel, **static)
        aliases = {}
    else:
        kern = functools.partial(_layer_kernel_with_alias, n_alias=len(kv_prev), **static)
        in_specs += [pl.BlockSpec(memory_space=pl.ANY)] * len(kv_prev)
        operands += list(kv_prev)
        aliases = {N_LAYER_INPUTS + i: o for i, o in enumerate((1, 2, 6, 7))}
    scratch = [
        pltpu.VMEM((TQ, D_MODEL), BF16),
        pltpu.VMEM((TQ, D_IN), F32),
        pltpu.VMEM((N_CONV_SLAB, TQ + U_HALO, LANES), F32),
        pltpu.VMEM((N_POOL_SLAB, TQ + B_HALO, LANES), F32),
        *[pltpu.VMEM((N_ATT_SLAB, TQ, LANES), F32) for _ in range(4)],
        *[pltpu.VMEM((N_ATT_SLAB, seq_len, LANES), F32) for _ in range(AB_COUNT)],
        pltpu.VMEM((seq_len, D_MODEL), BF16),
        pltpu.VMEM((nsmp, C_ATT), F32),
        pltpu.VMEM((C_ATT, LANES), F32),
        pltpu.VMEM((N_HEADS, LANES), F32),
        pltpu.VMEM((N_HEADS, LANES), F32),
        pltpu.VMEM((C_ATT, LANES), F32),
        pltpu.VMEM((C_ATT, LANES), F32),
        pltpu.VMEM((C_ATT, LANES), F32),
    ]
    return pl.pallas_call(
        kern,
        out_shape=out_shape,
        grid=(bsz + 1, nt),
        in_specs=in_specs,
        out_specs=out_specs,
        scratch_shapes=scratch,
        input_output_aliases=aliases,
        compiler_params=pltpu.CompilerParams(
            dimension_semantics=("arbitrary", "arbitrary"),
            vmem_limit_bytes=VMEM_LIMIT_BYTES,
        ),
        name="trunk_layer",
    )(*operands)


def _row_to_col_tile(row):
    return jnp.broadcast_to(row, (LANES, row.shape[1])).T


def _key_multiplicity(lane0, blk, win_len):
    t = lane0 + lax.broadcasted_iota(jnp.int32, (1, blk), 1)
    delta = win_len - t
    cnt = jnp.zeros((1, blk), F32)
    for d in DILATIONS:
        hit = jnp.logical_and(delta % d == 0, delta <= d * WINDOW_KEYS)
        cnt = cnt + hit.astype(F32)
    return cnt


def _decode_prepare_kernel(x_ref, sc_ref, sp_ref,
                           ng_ref, win_ref, cw_ref, cb_ref, lng_ref, lnb_ref, pw_ref, psc_ref, qg_ref, kg_ref,
                           cos_ref, sin_ref, mavg_ref,
                           nc_ref, np_ref, qr_s, kr_s, v_s, mixab_s, cg_s, *, pos):
    nsmp = x_ref.shape[0]

    x = x_ref[...]
    ms = jnp.mean(x * x, axis=-1, keepdims=True)
    h = (x * lax.rsqrt(ms + EPS) * ng_ref[...]).astype(BF16)
    proj = jnp.dot(h, win_ref[...], preferred_element_type=F32)

    u = proj[:, OFF_A_VAL:OFF_A_VAL + C_CONV] * _sigmoid(proj[:, OFF_A_GLU:OFF_A_GLU + C_CONV])
    conv = u * cw_ref[CONV_HALO:CONV_W, :] + cb_ref[...]
    for w in range(CONV_HALO):
        conv = conv + sc_ref[w] * cw_ref[w:w + 1, :]
    nc_ref[0:CONV_HALO - 1] = sc_ref[1:CONV_HALO]
    nc_ref[CONV_HALO - 1] = u
    mu = jnp.mean(conv, axis=-1, keepdims=True)
    cen = conv - mu
    var = jnp.mean(cen * cen, axis=-1, keepdims=True)
    ln = cen * lax.rsqrt(var + EPS) * lng_ref[...] + lnb_ref[...]
    ya = _silu(ln) * _silu(proj[:, OFF_A_GATE:OFF_A_GATE + C_CONV])

    bval = proj[:, OFF_B_VAL:OFF_B_VAL + C_POOL]
    lane_p = lax.broadcasted_iota(jnp.int32, (nsmp, C_POOL), 1)
    pooled = jnp.zeros((nsmp, C_POOL), F32)
    acc = bval
    done = 1
    for wi, w in enumerate(POOL_WINDOWS):
        for i in range(done, w):
            acc = acc + sp_ref[POOL_BUF - i]
        done = w
        pooled = jnp.where(lane_p // POOL_GC == wi, acc / float(min(pos + 1, w)), pooled)
    np_ref[0:POOL_BUF - 1] = sp_ref[1:POOL_BUF]
    np_ref[POOL_BUF - 1] = bval
    dpool = (pooled - bval).astype(BF16)
    yb = (jnp.dot(dpool, pw_ref[...], preferred_element_type=F32) * psc_ref[...]
          * _silu(proj[:, OFF_B_GATE:OFF_B_GATE + C_POOL]))
    mixab_s[:, 0:C_CONV] = ya
    mixab_s[:, C_CONV:C_CONV + C_POOL] = yb
    cg_s[...] = _silu(proj[:, OFF_C_GATE:OFF_C_GATE + C_ATT])

    q = proj[:, OFF_Q:OFF_Q + C_ATT]
    k = proj[:, OFF_K:OFF_K + C_ATT]
    qn = q * lax.rsqrt(jnp.dot((q * q).astype(BF16), mavg_ref[...], preferred_element_type=F32) + EPS) * qg_ref[...]
    kn = k * lax.rsqrt(jnp.dot((k * k).astype(BF16), mavg_ref[...], preferred_element_type=F32) + EPS) * kg_ref[...]
    for s in range(N_ATT_SLAB):
        cols = slice(s * LANES, (s + 1) * LANES)
        qs, ks = qn[:, cols], kn[:, cols]
        qr_s[:, cols] = (qs * cos_ref[...] + _swap_halves(qs) * sin_ref[...]) * (HEAD_DIM ** -0.5)
        kr_s[:, cols] = ks * cos_ref[...] + _swap_halves(ks) * sin_ref[...]
    v_s[...] = proj[:, OFF_V:OFF_V + C_ATT]


def _decode_prepare(layer, pos, xs, state_conv, state_pool, norm_g, w_in, conv_w, conv_b, ln_g, ln_b,
                    pool_wbd, pool_scale, qg, kg, cos_s, sin_s, mavg):
    nsmp = xs.shape[0]

    def const(shape):
        nd = len(shape)
        return pl.BlockSpec(shape, lambda i: (0,) * nd)

    def per_layer(shape):
        nd = len(shape)
        return pl.BlockSpec((None,) + shape, lambda i: (layer,) + (0,) * nd)

    in_specs = [
        const((nsmp, D_MODEL)), per_layer((CONV_HALO, nsmp, C_CONV)), per_layer((POOL_BUF, nsmp, C_POOL)),
        per_layer((1, D_MODEL)), per_layer((D_MODEL, D_IN)), per_layer((CONV_W, C_CONV)), per_layer((1, C_CONV)),
        per_layer((1, C_CONV)), per_layer((1, C_CONV)), per_layer((C_POOL, C_POOL)), per_layer((1, C_POOL)),
        per_layer((1, C_ATT)), per_layer((1, C_ATT)), const((1, LANES)), const((1, LANES)), const((C_ATT, C_ATT)),
    ]
    out_dims = [(CONV_HALO, nsmp, C_CONV), (POOL_BUF, nsmp, C_POOL), (nsmp, C_ATT), (nsmp, C_ATT), (nsmp, C_ATT),
                (nsmp, C_CONV + C_POOL), (nsmp, C_ATT)]
    return pl.pallas_call(
        functools.partial(_decode_prepare_kernel, pos=pos),
        out_shape=[jax.ShapeDtypeStruct(d, F32) for d in out_dims],
        grid=(1,),
        in_specs=in_specs,
        out_specs=[const(d) for d in out_dims],
        compiler_params=pltpu.CompilerParams(dimension_semantics=("arbitrary",)),
        name="decode_prepare",
    )(xs, state_conv, state_pool, norm_g, w_in, conv_w, conv_b, ln_g, ln_b, pool_wbd, pool_scale, qg, kg,
      cos_s, sin_s, mavg)


def _sample_start(cond, smp, qr_s, kr_s, v_s, qcol_s, sm_s, sl_s, so_s, kcar_s, vcar_s):
    nsmp = qr_s.shape[0]
    n_pat = float(len(DILATIONS))

    @pl.when(cond)
    def _start():
        mine = lax.broadcasted_iota(jnp.int32, (nsmp, C_ATT), 0) == smp

        def col_tile(ref):
            return _row_to_col_tile(jnp.sum(jnp.where(mine, ref[...], 0.0), axis=0, keepdims=True))

        q_col = col_tile(qr_s)
        k_col = col_tile(kr_s)
        v_col = col_tile(v_s)
        qcol_s[...] = q_col
        kcar_s[...] = k_col
        vcar_s[...] = v_col
        so_s[...] = v_col * n_pat
        for hd in range(N_HEADS):
            hr = slice(hd * HEAD_DIM, (hd + 1) * HEAD_DIM)
            sm_s[hd:hd + 1, :] = jnp.sum(k_col[hr, :] * q_col[hr, :], axis=0, keepdims=True)
        sl_s[...] = jnp.full(sl_s.shape, n_pat, F32)


def _sample_window(lane0, kt_ref, vt_ref, okt_ref, ovt_ref, qcol_s, sm_s, sl_s, so_s, kcar_s, vcar_s, *, win_len):
    blk = kt_ref.shape[1]
    n_tiles = blk // LANES
    cnt = _key_multiplicity(lane0, blk, win_len)
    reach = cnt > 0.0
    last_lane = lax.broadcasted_iota(jnp.int32, (HEAD_DIM, blk), 1) == blk - 1

    for hd in range(N_HEADS):
        hr = slice(hd * HEAD_DIM, (hd + 1) * HEAD_DIM)
        kt = kt_ref[hr, :]
        vt = vt_ref[hr, :]
        s_win = jnp.sum(kt * jnp.concatenate([qcol_s[hr, :]] * n_tiles, axis=1), axis=0, keepdims=True)
        s_win = jnp.where(reach, s_win, NEG)
        m_old = sm_s[hd:hd + 1, :]
        m_new = jnp.maximum(m_old, jnp.max(s_win, axis=-1, keepdims=True))
        w_old = jnp.exp(m_old - m_new)
        p_win = cnt * jnp.exp(s_win - m_new[:, 0:1])
        sm_s[hd:hd + 1, :] = m_new
        sl_s[hd:hd + 1, :] = sl_s[hd:hd + 1, :] * w_old + jnp.sum(p_win, axis=-1, keepdims=True)
        so_s[hr, :] = so_s[hr, :] * w_old + jnp.sum(vt * p_win, axis=-1, keepdims=True)
        k_next = jnp.concatenate([kcar_s[hr, :]] * n_tiles, axis=1)
        v_next = jnp.concatenate([vcar_s[hr, :]] * n_tiles, axis=1)
        okt_ref[hr, :] = jnp.where(last_lane, k_next, pltpu.roll(kt, blk - 1, 1))
        ovt_ref[hr, :] = jnp.where(last_lane, v_next, pltpu.roll(vt, blk - 1, 1))
        kcar_s[hr, :] = jnp.broadcast_to(kt[:, 0:1], (HEAD_DIM, LANES))
        vcar_s[hr, :] = jnp.broadcast_to(vt[:, 0:1], (HEAD_DIM, LANES))


def _sample_done(cond, smp, att_s, sl_s, so_s):
    nsmp = att_s.shape[0]

    @pl.when(cond)
    def _done():
        cols = [so_s[hd * HEAD_DIM:(hd + 1) * HEAD_DIM, :] / sl_s[hd:hd + 1, :] for hd in range(N_HEADS)]
        att_row = jnp.concatenate(cols, axis=0).T[0:1, :]
        mine = lax.broadcasted_iota(jnp.int32, (nsmp, C_ATT), 0) == smp
        att_s[...] = jnp.where(mine, att_row, att_s[...])


def _sample_finish(cond, x_ref, wout_ref, y_ref, att_s, mixab_s, cg_s):
    @pl.when(cond)
    def _finish():
        yc = att_s[...] * cg_s[...]
        mixed = jnp.concatenate([mixab_s[...], yc], axis=-1).astype(BF16)
        y_ref[...] = x_ref[...] + jnp.dot(mixed, wout_ref[...], preferred_element_type=F32)


def _to_channel_major(a):
    depth, bsz, ntok, nh, hd = a.shape
    return jnp.transpose(a, (0, 1, 3, 4, 2)).reshape(depth, bsz, nh * hd, ntok)


def _from_channel_major(a):
    depth, bsz, _, ntok = a.shape
    return jnp.transpose(a.reshape(depth, bsz, N_HEADS, HEAD_DIM, ntok), (0, 1, 4, 2, 3))


def kernel(x_prompt, x_sample, state_conv, state_pool, cache_k_win, cache_v_win, norm_g, w_in, conv_w, conv_b,
           ln_g, ln_b, pool_w, pool_scale, q_norm_g, k_norm_g, w_out):
    depth = w_in.shape[0]
    seq_len = x_prompt.shape[1]
    nsmp = x_sample.shape[0]

    cos_p, sin_p = _rope_tables(jnp.arange(seq_len, dtype=jnp.int32))
    cos_s, sin_s = _rope_tables(jnp.full((1,), PAST_LEN, dtype=jnp.int32))
    mavg = _head_mean_matrix()
    weights = (norm_g[:, None], w_in.astype(BF16), conv_w, conv_b[:, None], ln_g[:, None], ln_b[:, None],
               _pool_block_diag(pool_w).astype(BF16), pool_scale[:, None],
               jnp.tile(q_norm_g, (1, N_HEADS))[:, None], jnp.tile(k_norm_g, (1, N_HEADS))[:, None])
    w_out_b = w_out.astype(BF16)

    kt = _to_channel_major(cache_k_win)
    vt = _to_channel_major(cache_v_win)
    xp = x_prompt
    xs = x_sample.reshape(nsmp, D_MODEL)
    state_conv_pm = jnp.swapaxes(state_conv, 1, 2)
    state_pool_pm = jnp.swapaxes(state_pool, 1, 2)
    kv = None
    conv_p, pool_p, conv_s, pool_s = [], [], [], []
    for layer in range(depth):
        ncs, nps, *decode_new = _decode_prepare(layer, PAST_LEN, xs, state_conv_pm, state_pool_pm, *weights,
                                                cos_s, sin_s, mavg)
        xp, kp, vp, cst, pst, xs, ks, vs = _trunk_layer(
            layer, xp, xs, decode_new, kt, vt, *weights, cos_p, sin_p, mavg, w_out_b, kv_prev=kv)
        kv = (kp, vp, ks, vs)
        conv_p.append(cst)
        pool_p.append(pst)
        conv_s.append(ncs)
        pool_s.append(nps)

    return (xp, xs.reshape(nsmp, 1, D_MODEL), jnp.stack(conv_p), jnp.stack(pool_p),
            _from_channel_major(kv[0]), _from_channel_major(kv[1]),
            jnp.swapaxes(jnp.stack(conv_s), 1, 2), jnp.swapaxes(jnp.stack(pool_s), 1, 2),
            _from_channel_major(kv[2]), _from_channel_major(kv[3]))
```

```python
import functools

import jax
import jax.numpy as jnp
import numpy as np
from jax import lax
from jax.experimental import pallas as pl
from jax.experimental.pallas import tpu as pltpu

F32 = jnp.float32
BF16 = jnp.bfloat16

D_MODEL = 1024
C_CONV = 384
C_POOL = 256
C_ATT = 384
HEAD_DIM = 64
N_HEADS = C_ATT // HEAD_DIM
CONV_W = 31
CONV_HALO = CONV_W - 1
POOL_WINDOWS = (2, 4, 8, 16)
POOL_GC = 64
POOL_BUF = 15
DILATIONS = (1, 4, 16)
WINDOW_KEYS = 128
EPS = 1e-6
ROPE_THETA = 10000.0
D_IN = 3 * C_CONV + 2 * C_POOL + 4 * C_ATT
PAST_LEN = 16384
NEG = -1e30
LOG2_E = 1.4426950408889634

OFF_A_VAL = 0
OFF_A_GLU = OFF_A_VAL + C_CONV
OFF_A_GATE = OFF_A_GLU + C_CONV
OFF_B_VAL = OFF_A_GATE + C_CONV
OFF_B_GATE = OFF_B_VAL + C_POOL
OFF_Q = OFF_B_GATE + C_POOL
OFF_K = OFF_Q + C_ATT
OFF_V = OFF_K + C_ATT
OFF_C_GATE = OFF_V + C_ATT

LANES = 128
N_CONV_SLAB = C_CONV // LANES
N_POOL_SLAB = C_POOL // LANES
N_ATT_SLAB = C_ATT // LANES
VMEM_LIMIT_BYTES = 60 * 1024 * 1024

TQ = 256
RC = 32
CONV_GROUP = 8
U_HALO = 32
B_HALO = 16
QB = WINDOW_KEYS
UNROLL_NEAR = 15
UNROLL_MID = 12
UNROLL_FAR = 16

AB_Q, AB_K, AB_V, AB_O, AB_LSE = range(5)
AB_COUNT = 5


def _sigmoid(x):
    return 0.5 * jnp.tanh(0.5 * x) + 0.5


def _silu(x):
    h = 0.5 * x
    return h * jnp.tanh(h) + h


def _rope_tables(positions):
    half = HEAD_DIM // 2
    inv = ROPE_THETA ** (-jnp.arange(half, dtype=F32) / half)
    ang = positions.astype(F32)[:, None] * inv[None, :]
    cos = jnp.cos(ang)
    sin = jnp.sin(ang)
    cos_h = jnp.concatenate([cos, cos], axis=-1)
    sin_h = jnp.concatenate([-sin, sin], axis=-1)
    reps = LANES // HEAD_DIM
    return jnp.tile(cos_h, (1, reps)), jnp.tile(sin_h, (1, reps))


def _head_mean_matrix():
    idx = np.arange(C_ATT) // HEAD_DIM
    return jnp.asarray((idx[:, None] == idx[None, :]).astype(np.float32) / HEAD_DIM, dtype=BF16)


def _pool_block_diag(pool_w):
    out = jnp.zeros((pool_w.shape[0], C_POOL, C_POOL), pool_w.dtype)
    for g in range(len(POOL_WINDOWS)):
        out = out.at[:, g * POOL_GC:(g + 1) * POOL_GC, g * POOL_GC:(g + 1) * POOL_GC].set(pool_w[:, g])
    return out


def _swap_halves(x):
    lane = lax.broadcasted_iota(jnp.int32, x.shape, 1)
    first_half = (lane % HEAD_DIM) < (HEAD_DIM // 2)
    return jnp.where(first_half, pltpu.roll(x, LANES - HEAD_DIM // 2, 1), pltpu.roll(x, HEAD_DIM // 2, 1))


def _pool_means(loads, pos):
    lane = lax.broadcasted_iota(jnp.int32, loads(0, 0).shape, 1)
    lo = lane < POOL_GC
    posf = (pos + 1).astype(F32)
    outs = []
    for slab in range(N_POOL_SLAB):
        w_lo, w_hi = POOL_WINDOWS[2 * slab], POOL_WINDOWS[2 * slab + 1]
        cur = loads(0, slab)
        s = cur
        for i in range(1, w_lo):
            s = s + loads(i, slab)
        s_lo = s
        for i in range(w_lo, w_hi):
            s = s + loads(i, slab)
        s_hi = s
        cnt_lo = jnp.minimum(posf, float(w_lo))
        cnt_hi = jnp.minimum(posf, float(w_hi))
        pooled = jnp.where(lo, s_lo / cnt_lo, s_hi / cnt_hi)
        outs.append(pooled - cur)
    return outs


N_LAYER_INPUTS = 24


def _layer_kernel(x_ref, x2_ref, ng_ref, win_ref, cw_ref, cb_ref, lng_ref, lnb_ref, pw_ref, psc_ref,
                  qg_ref, kg_ref, cos_ref, sin_ref, mavg_ref, wout_ref,
                  xs_ref, qr_s, kr_s, v_s, mixab_s, cgs_s, kt_ref, vt_ref,
                  y_ref, ko_ref, vo_ref, cst_ref, pst_ref,
                  ys_ref, okt_ref, ovt_ref,
                  h_s, proj, u_buf, b_buf,
                  st_q, st_k, st_v, st_att, ab0, ab1, ab2, ab3, ab4, mix,
                  att_s, qcol_s, sm_s, sl_s, so_s, kcar_s, vcar_s,
                  *, seq_len, win_len):
    ab = (ab0, ab1, ab2, ab3, ab4)
    nt = seq_len // TQ
    b = pl.program_id(0)
    t = pl.program_id(1)
    n_batch = pl.num_programs(0) - 1
    blocks_per_sample = win_len // kt_ref.shape[1]
    row0 = pl.multiple_of(t * TQ, TQ)
    sq_s = h_s.at[:, 0:2 * C_ATT]
    d_s = h_s.at[:, 2 * C_ATT:2 * C_ATT + C_POOL]
    msq_s = proj.at[:, OFF_A_VAL:OFF_A_VAL + 2 * C_ATT]
    yb_s = proj.at[:, OFF_B_VAL:OFF_B_VAL + C_POOL]
    conv_s = st_att

    step = b * nt + t
    part = step % blocks_per_sample
    smp = step // blocks_per_sample

    def sample_start():
        _sample_start(part == 0, smp, qr_s, kr_s, v_s, qcol_s, sm_s, sl_s, so_s, kcar_s, vcar_s)

    def sample_window():
        _sample_window((blocks_per_sample - 1 - part) * kt_ref.shape[1], kt_ref, vt_ref, okt_ref, ovt_ref,
                       qcol_s, sm_s, sl_s, so_s, kcar_s, vcar_s, win_len=win_len)

    def sample_done():
        _sample_done(part == blocks_per_sample - 1, smp, att_s, sl_s, so_s)

    def phase2():
        d_mid = DILATIONS[1]
        per_stream = TQ // d_mid
        for s in range(N_ATT_SLAB):
            for r in range(d_mid):
                src_rows = pl.ds(pl.multiple_of(r * (seq_len // d_mid) + t * per_stream, per_stream), per_stream)
                st_att[s, pl.ds(r, per_stream, stride=d_mid), :] = ab[AB_O][s, src_rows, :]

        def att_chunk(i, c):
            r = pl.multiple_of(i * RC, RC)
            grow = pl.ds(pl.multiple_of(row0 + r, RC), RC)
            for s in range(N_ATT_SLAB):
                mcols = slice(C_CONV + C_POOL + s * LANES, C_CONV + C_POOL + (s + 1) * LANES)
                yc = st_att[s, pl.ds(r, RC), :] * mix[grow, mcols].astype(F32)
                mix[grow, mcols] = yc.astype(BF16)
            return c

        lax.fori_loop(0, TQ // RC, att_chunk, 0, unroll=True)
        y_ref[...] = x2_ref[...] + jnp.dot(mix[pl.ds(row0, TQ), :], wout_ref[...], preferred_element_type=F32)

    def phase1(with_phase2):
        @pl.when(t == 0)
        def _zero_halo():
            u_buf[:, 0:U_HALO, :] = jnp.zeros((N_CONV_SLAB, U_HALO, LANES), F32)
            b_buf[:, 0:B_HALO, :] = jnp.zeros((N_POOL_SLAB, B_HALO, LANES), F32)
            if not with_phase2:
                att_s[...] = jnp.zeros(att_s.shape, F32)

        sample_start()
        if with_phase2:
            phase2()
        sample_window()

        def norm_chunk(i, c):
            r = pl.multiple_of(i * RC, RC)
            x = x_ref[pl.ds(r, RC), :]
            ms = jnp.mean(x * x, axis=-1, keepdims=True)
            h_s[pl.ds(r, RC), :] = (x * lax.rsqrt(ms + EPS) * ng_ref[...]).astype(BF16)
            return c

        lax.fori_loop(0, TQ // RC, norm_chunk, 0, unroll=True)
        proj[...] = jnp.dot(h_s[...], win_ref[...], preferred_element_type=F32)

        def split_chunk(i, c):
            r = pl.multiple_of(i * RC, RC)
            rows = pl.ds(r, RC)
            grow = pl.ds(pl.multiple_of(row0 + r, RC), RC)
            for s in range(N_CONV_SLAB):
                cols = slice(s * LANES, (s + 1) * LANES)
                a_val = proj[rows, OFF_A_VAL + s * LANES:OFF_A_VAL + (s + 1) * LANES]
                a_glu = proj[rows, OFF_A_GLU + s * LANES:OFF_A_GLU + (s + 1) * LANES]
                u_buf[s, pl.ds(U_HALO + r, RC), :] = a_val * _sigmoid(a_glu)
                q = proj[rows, OFF_Q + s * LANES:OFF_Q + (s + 1) * LANES]
                k = proj[rows, OFF_K + s * LANES:OFF_K + (s + 1) * LANES]
                sq_s[rows, cols] = (q * q).astype(BF16)
                sq_s[rows, C_ATT + s * LANES:C_ATT + (s + 1) * LANES] = (k * k).astype(BF16)
                st_v[s, rows, :] = proj[rows, OFF_V + s * LANES:OFF_V + (s + 1) * LANES]
                mix[grow, C_CONV + C_POOL + s * LANES:C_CONV + C_POOL + (s + 1) * LANES] = _silu(
                    proj[rows, OFF_C_GATE + s * LANES:OFF_C_GATE + (s + 1) * LANES]).astype(BF16)
            for s in range(N_POOL_SLAB):
                b_buf[s, pl.ds(B_HALO + r, RC), :] = proj[rows, OFF_B_VAL + s * LANES:OFF_B_VAL + (s + 1) * LANES]
            pos = row0 + r + lax.broadcasted_iota(jnp.int32, (RC, 1), 0)
            dl = _pool_means(lambda sh, s: b_buf[s, pl.ds(r + B_HALO - sh, RC), :], pos)
            for s in range(N_POOL_SLAB):
                d_s[rows, s * LANES:(s + 1) * LANES] = dl[s].astype(BF16)
            return c

        lax.fori_loop(0, TQ // RC, split_chunk, 0, unroll=True)
        msq_s[:, 0:C_ATT] = jnp.dot(sq_s[:, 0:C_ATT], mavg_ref[...], preferred_element_type=F32)
        msq_s[:, C_ATT:2 * C_ATT] = jnp.dot(sq_s[:, C_ATT:2 * C_ATT], mavg_ref[...], preferred_element_type=F32)
        yb_s[...] = jnp.dot(d_s[...], pw_ref[...], preferred_element_type=F32)

        row_groups = TQ // CONV_GROUP
        for s in range(N_CONV_SLAB):
            cols = slice(s * LANES, (s + 1) * LANES)
            acc = [jnp.broadcast_to(cb_ref[:, cols], (CONV_GROUP, LANES))] * row_groups
            for a in range(CONV_GROUP):
                taps = list(range(a, CONV_W, CONV_GROUP))
                cws = [jnp.broadcast_to(cw_ref[w:w + 1, cols], (CONV_GROUP, LANES)) for w in taps]
                for j in range(row_groups + len(taps) - 1):
                    start = (U_HALO - CONV_HALO) + a + CONV_GROUP * j
                    window = u_buf[s, start:start + CONV_GROUP, :]
                    for m in range(len(taps)):
                        if 0 <= j - m < row_groups:
                            acc[j - m] = acc[j - m] + window * cws[m]
            for g in range(row_groups):
                conv_s[s, g * CONV_GROUP:(g + 1) * CONV_GROUP, :] = acc[g]

        def mixer_chunk(i, c):
            r = pl.multiple_of(i * RC, RC)
            rows = pl.ds(r, RC)
            grow = pl.ds(pl.multiple_of(row0 + r, RC), RC)
            conv = [conv_s[s, rows, :] for s in range(N_CONV_SLAB)]
            mu = jnp.sum(conv[0] + conv[1] + conv[2], axis=-1, keepdims=True) * (1.0 / C_CONV)
            cen = [cv - mu for cv in conv]
            var = jnp.sum(cen[0] * cen[0] + cen[1] * cen[1] + cen[2] * cen[2], axis=-1, keepdims=True) * (1.0 / C_CONV)
            rstd = lax.rsqrt(var + EPS)
            for s in range(N_CONV_SLAB):
                cols = slice(s * LANES, (s + 1) * LANES)
                ln = cen[s] * rstd * lng_ref[:, cols] + lnb_ref[:, cols]
                gate = proj[rows, OFF_A_GATE + s * LANES:OFF_A_GATE + (s + 1) * LANES]
                mix[grow, cols] = (_silu(ln) * _silu(gate)).astype(BF16)
            for s in range(N_POOL_SLAB):
                cols = slice(s * LANES, (s + 1) * LANES)
                gate = proj[rows, OFF_B_GATE + s * LANES:OFF_B_GATE + (s + 1) * LANES]
                yb = yb_s[rows, cols] * psc_ref[:, cols] * _silu(gate)
                mix[grow, C_CONV + s * LANES:C_CONV + (s + 1) * LANES] = yb.astype(BF16)
            cos = cos_ref[rows, :]
            sin = sin_ref[rows, :]
            for s in range(N_ATT_SLAB):
                cols = slice(s * LANES, (s + 1) * LANES)
                q = proj[rows, OFF_Q + s * LANES:OFF_Q + (s + 1) * LANES]
                qn = q * lax.rsqrt(msq_s[rows, cols] + EPS) * qg_ref[:, cols]
                qr = qn * cos + _swap_halves(qn) * sin
                st_q[s, rows, :] = qr * (HEAD_DIM ** -0.5 * LOG2_E)
                k = proj[rows, OFF_K + s * LANES:OFF_K + (s + 1) * LANES]
                kn = k * lax.rsqrt(msq_s[rows, C_ATT + s * LANES:C_ATT + (s + 1) * LANES] + EPS) * kg_ref[:, cols]
                st_k[s, rows, :] = kn * cos + _swap_halves(kn) * sin
            return c

        lax.fori_loop(0, TQ // RC, mixer_chunk, 0, unroll=True)

        d_mid = DILATIONS[1]
        per_stream = TQ // d_mid
        for s in range(N_ATT_SLAB):
            ko_ref[s * LANES:(s + 1) * LANES, :] = st_k[s].T
            vo_ref[s * LANES:(s + 1) * LANES, :] = st_v[s].T
            for st, dst in ((st_q, AB_Q), (st_k, AB_K), (st_v, AB_V)):
                for r in range(d_mid):
                    dst_rows = pl.ds(pl.multiple_of(r * (seq_len // d_mid) + t * per_stream, per_stream), per_stream)
                    ab[dst][s, dst_rows, :] = st[s, pl.ds(r, per_stream, stride=d_mid), :]

        @pl.when(t == nt - 1)
        def _write_state():
            for s in range(N_CONV_SLAB):
                cst_ref[:, s * LANES:(s + 1) * LANES] = u_buf[s, TQ + U_HALO - CONV_HALO:TQ + U_HALO, :]
            for s in range(N_POOL_SLAB):
                pst_ref[:, s * LANES:(s + 1) * LANES] = b_buf[s, TQ + B_HALO - POOL_BUF:TQ + B_HALO, :]

        u_buf[:, 0:U_HALO, :] = u_buf[:, TQ:TQ + U_HALO, :]
        b_buf[:, 0:B_HALO, :] = b_buf[:, TQ:TQ + B_HALO, :]
        sample_done()

    def attention():
        d_mid, d_far = DILATIONS[1], DILATIONS[2]
        ratio = d_far // d_mid
        stream_len = seq_len // d_mid
        piece = QB // d_mid
        lane = lax.broadcasted_iota(jnp.int32, (QB, LANES), 1)
        rowi = lax.broadcasted_iota(jnp.int32, (QB, LANES), 0)
        lo = lane < HEAD_DIM

        def masks(key_pos, query_pos):
            cur = key_pos <= query_pos
            prev = key_pos >= query_pos
            cur2 = jnp.concatenate([cur, cur], axis=0)
            return cur2, jnp.concatenate([jnp.concatenate([prev, prev], axis=0), cur2], axis=1)

        cur_ok2, prev_cur_ok2 = masks(lane, rowi)
        ncur_ok2, nprev_cur_ok2 = masks(d_mid * (lane % piece) + lane // piece, d_mid * (rowi % piece) + rowi // piece)

        def attend(q, keys, vals, mask):
            qa = jnp.where(lo, q, 0.0).astype(BF16)
            qb = jnp.where(lo, 0.0, q).astype(BF16)
            q2 = jnp.concatenate([qa, qb], axis=0)
            sc = lax.dot_general(q2, keys.astype(BF16), (((1,), (1,)), ((), ())), preferred_element_type=F32)
            sc = jnp.where(mask, sc, NEG)
            m = jnp.max(sc, axis=-1, keepdims=True)
            p = jnp.exp2(sc - m).astype(BF16)
            v1 = jnp.concatenate([vals.astype(BF16), jnp.ones(vals.shape, BF16)], axis=1)
            ol = jnp.dot(p, v1, preferred_element_type=F32)
            o_u = jnp.where(lo, ol[0:QB, 0:LANES], ol[QB:2 * QB, 0:LANES])
            l_u = jnp.where(lo, ol[0:QB, LANES:2 * LANES], ol[QB:2 * QB, LANES:2 * LANES])
            m_u = jnp.where(lo, m[0:QB], m[QB:2 * QB])
            return o_u, m_u, l_u

        def load(ref, s, pieces):
            tiles = [ref[s, p, :] for p in pieces]
            return tiles[0] if len(tiles) == 1 else jnp.concatenate(tiles, axis=0)

        def save(ref, s, pieces, val):
            n = val.shape[0] // len(pieces)
            for j, p in enumerate(pieces):
                ref[s, p, :] = val[j * n:(j + 1) * n]

        def block(rows, krows, mask, merge, with_lse=True):
            outs = []
            for s in range(N_ATT_SLAB):
                o_u, m_u, l_u = attend(load(ab[AB_Q], s, rows), load(ab[AB_K], s, krows), load(ab[AB_V], s, krows),
                                       mask)
                if merge:
                    lse_old = load(ab[AB_LSE], s, rows)
                    m_new = jnp.maximum(lse_old, m_u)
                    w_old = jnp.exp2(lse_old - m_new)
                    w_u = jnp.exp2(m_u - m_new)
                    den = w_old + l_u * w_u
                    o_n = (load(ab[AB_O], s, rows) * w_old + o_u * w_u) * (1.0 / den)
                else:
                    m_new, den = m_u, l_u
                    o_n = o_u * (1.0 / l_u)
                outs.append((o_n, m_new + jnp.log2(den) if with_lse else None))
            return outs

        def store(rows, outs):
            for s, (o_n, lse_n) in enumerate(outs):
                save(ab[AB_O], s, rows, o_n)
                if lse_n is not None:
                    save(ab[AB_LSE], s, rows, lse_n)

        def token_block(base):
            return [pl.ds(r * stream_len + base, piece) for r in range(d_mid)]

        store(token_block(0), block(token_block(0), token_block(0), ncur_ok2, merge=False))

        def near_unit(i, c):
            base = pl.multiple_of(i * piece, piece)
            rows = token_block(base)
            store(rows, block(rows, token_block(base - piece) + rows, nprev_cur_ok2, merge=False))
            return c

        lax.fori_loop(1, seq_len // QB, near_unit, 0, unroll=UNROLL_NEAR)

        def mid_first(ph, c):
            rows = [pl.ds(pl.multiple_of(ph * stream_len, QB), QB)]
            store(rows, block(rows, rows, cur_ok2, merge=True))
            return c

        lax.fori_loop(0, d_mid, mid_first, 0, unroll=True)
        later_blocks = stream_len // QB - 1

        def mid_unit(i, c):
            start = pl.multiple_of((i // later_blocks) * stream_len + (i % later_blocks + 1) * QB, QB)
            rows = [pl.ds(start, QB)]
            store(rows, block(rows, [pl.ds(start - QB, 2 * QB)], prev_cur_ok2, merge=True))
            return c

        lax.fori_loop(0, d_mid * later_blocks, mid_unit, 0, unroll=UNROLL_MID)

        def far_unit(r, c):
            rows = [pl.ds((r % d_mid) * stream_len + r // d_mid, QB, stride=ratio)]
            store(rows, block(rows, rows, cur_ok2, merge=True, with_lse=False))
            return c

        lax.fori_loop(0, d_far, far_unit, 0, unroll=UNROLL_FAR)

    @pl.when(b == 0)
    def _first_row():
        phase1(with_phase2=False)

    @pl.when(jnp.logical_and(b > 0, b < n_batch))
    def _steady_rows():
        phase1(with_phase2=True)

    @pl.when(b == n_batch)
    def _last_row():
        phase2()
        _sample_finish(t == nt - 1, xs_ref, wout_ref, ys_ref, att_s, mixab_s, cgs_s)

    pl.when(jnp.logical_and(t == nt - 1, b < n_batch))(attention)


def _layer_kernel_with_alias(*refs, n_alias, **kw):
    _layer_kernel(*refs[:N_LAYER_INPUTS], *refs[N_LAYER_INPUTS + n_alias:], **kw)


def _trunk_layer(layer, x, xs, decode_new, kt, vt,
                 norm_g, w_in, conv_w, conv_b, ln_g, ln_b, pool_wbd, pool_scale, qg, kg,
                 cos_t, sin_t, mavg, w_out, kv_prev=None):
    bsz, seq_len, _ = x.shape
    nsmp = xs.shape[0]
    depth, _, _, win_len = kt.shape
    assert seq_len % TQ == 0 and seq_len == QB * DILATIONS[-1] and DILATIONS[0] == 1
    assert win_len == WINDOW_KEYS * DILATIONS[-1]
    nt = seq_len // TQ
    blocks_per_sample, rem = divmod(bsz * nt, nsmp)
    assert rem == 0 and blocks_per_sample >= 1 and win_len % (blocks_per_sample * LANES) == 0
    win_blk = win_len // blocks_per_sample

    def p1(b, t):
        return jnp.minimum(b, bsz - 1), jnp.where(b == bsz, nt - 1, t)

    def p2(b, t):
        return jnp.maximum(b - 1, 0), jnp.where(b == 0, 0, t)

    def sample_block(b, t):
        pb, pt = p1(b, t)
        step = pb * nt + pt
        return (layer, step // blocks_per_sample, 0, blocks_per_sample - 1 - step % blocks_per_sample)

    def const(shape):
        nd = len(shape)
        return pl.BlockSpec(shape, lambda b, t: (0,) * nd)

    def resident(shape):
        nd = len(shape)
        return pl.BlockSpec(shape, lambda b, t: (0,) * nd, pipeline_mode=pl.Buffered(1))

    def per_layer(shape, **kw):
        nd = len(shape)
        return pl.BlockSpec((None,) + shape, lambda b, t: (layer,) + (0,) * nd, **kw)

    in_specs = [
        pl.BlockSpec((None, TQ, D_MODEL), lambda b, t: (*p1(b, t), 0)),
        pl.BlockSpec((None, TQ, D_MODEL), lambda b, t: (*p2(b, t), 0)),
        per_layer((1, D_MODEL)),
        per_layer((D_MODEL, D_IN), pipeline_mode=pl.Buffered(1)),
        per_layer((CONV_W, C_CONV)),
        per_layer((1, C_CONV)),
        per_layer((1, C_CONV)),
        per_layer((1, C_CONV)),
        per_layer((C_POOL, C_POOL)),
        per_layer((1, C_POOL)),
        per_layer((1, C_ATT)),
        per_layer((1, C_ATT)),
        pl.BlockSpec((TQ, LANES), lambda b, t: (p1(b, t)[1], 0)),
        pl.BlockSpec((TQ, LANES), lambda b, t: (p1(b, t)[1], 0)),
        const((C_ATT, C_ATT)),
        per_layer((D_MODEL, D_MODEL), pipeline_mode=pl.Buffered(1)),
        const((nsmp, D_MODEL)),
        const((nsmp, C_ATT)),
        const((nsmp, C_ATT)),
        const((nsmp, C_ATT)),
        const((nsmp, C_CONV + C_POOL)),
        const((nsmp, C_ATT)),
        pl.BlockSpec((None, None, C_ATT, win_blk), sample_block),
        pl.BlockSpec((None, None, C_ATT, win_blk), sample_block),
    ]
    operands = [x, x, norm_g, w_in, conv_w, conv_b, ln_g, ln_b, pool_wbd, pool_scale, qg, kg, cos_t, sin_t, mavg,
                w_out, xs, *decode_new, kt, vt]
    assert len(operands) == N_LAYER_INPUTS
    kv_spec = pl.BlockSpec((None, None, C_ATT, TQ), lambda b, t: (layer, p1(b, t)[0], 0, p1(b, t)[1]))
    out_specs = [
        pl.BlockSpec((None, TQ, D_MODEL), lambda b, t: (*p2(b, t), 0)),
        kv_spec,
        kv_spec,
        pl.BlockSpec((None, CONV_HALO, C_CONV), lambda b, t: (p1(b, t)[0], 0, 0)),
        pl.BlockSpec((None, POOL_BUF, C_POOL), lambda b, t: (p1(b, t)[0], 0, 0)),
        const((nsmp, D_MODEL)),
        pl.BlockSpec((None, None, C_ATT, win_blk), sample_block),
        pl.BlockSpec((None, None, C_ATT, win_blk), sample_block),
    ]
    out_shape = [
        jax.ShapeDtypeStruct((bsz, seq_len, D_MODEL), F32),
        jax.ShapeDtypeStruct((depth, bsz, C_ATT, seq_len), F32),
        jax.ShapeDtypeStruct((depth, bsz, C_ATT, seq_len), F32),
        jax.ShapeDtypeStruct((bsz, CONV_HALO, C_CONV), F32),
        jax.ShapeDtypeStruct((bsz, POOL_BUF, C_POOL), F32),
        jax.ShapeDtypeStruct((nsmp, D_MODEL), F32),
        jax.ShapeDtypeStruct((depth, nsmp, C_ATT, win_len), F32),
        jax.ShapeDtypeStruct((depth, nsmp, C_ATT, win_len), F32),
    ]
    static = dict(seq_len=seq_len, win_len=win_len)
    if kv_prev is None:
        kern = functools.partial(_layer_kernel, **static)
        aliases = {}
    else:
        kern = functools.partial(_layer_kernel_with_alias, n_alias=len(kv_prev), **static)
        in_specs += [pl.BlockSpec(memory_space=pl.ANY)] * len(kv_prev)
        operands += list(kv_prev)
        aliases = {N_LAYER_INPUTS + i: o for i, o in enumerate((1, 2, 6, 7))}
    scratch = [
        pltpu.VMEM((TQ, D_MODEL), BF16),
        pltpu.VMEM((TQ, D_IN), F32),
        pltpu.VMEM((N_CONV_SLAB, TQ + U_HALO, LANES), F32),
        pltpu.VMEM((N_POOL_SLAB, TQ + B_HALO, LANES), F32),
        *[pltpu.VMEM((N_ATT_SLAB, TQ, LANES), F32) for _ in range(4)],
        *[pltpu.VMEM((N_ATT_SLAB, seq_len, LANES), F32) for _ in range(AB_COUNT)],
        pltpu.VMEM((seq_len, D_MODEL), BF16),
        pltpu.VMEM((nsmp, C_ATT), F32),
        pltpu.VMEM((C_ATT, LANES), F32),
        pltpu.VMEM((N_HEADS, LANES), F32),
        pltpu.VMEM((N_HEADS, LANES), F32),
        pltpu.VMEM((C_ATT, LANES), F32),
        pltpu.VMEM((C_ATT, LANES), F32),
        pltpu.VMEM((C_ATT, LANES), F32),
    ]
    return pl.pallas_call(
        kern,
        out_shape=out_shape,
        grid=(bsz + 1, nt),
        in_specs=in_specs,
        out_specs=out_specs,
        scratch_shapes=scratch,
        input_output_aliases=aliases,
        compiler_params=pltpu.CompilerParams(
            dimension_semantics=("arbitrary", "arbitrary"),
            vmem_limit_bytes=VMEM_LIMIT_BYTES,
        ),
        name="trunk_layer",
    )(*operands)


def _row_to_col_tile(row):
    return jnp.broadcast_to(row, (LANES, row.shape[1])).T


def _key_multiplicity(lane0, blk, win_len):
    t = lane0 + lax.broadcasted_iota(jnp.int32, (1, blk), 1)
    delta = win_len - t
    cnt = jnp.zeros((1, blk), F32)
    for d in DILATIONS:
        hit = jnp.logical_and(delta % d == 0, delta <= d * WINDOW_KEYS)
        cnt = cnt + hit.astype(F32)
    return cnt


def _decode_prepare_kernel(x_ref, sc_ref, sp_ref,
                           ng_ref, win_ref, cw_ref, cb_ref, lng_ref, lnb_ref, pw_ref, psc_ref, qg_ref, kg_ref,
                           cos_ref, sin_ref, mavg_ref,
                           nc_ref, np_ref, qr_s, kr_s, v_s, mixab_s, cg_s, *, pos):
    nsmp = x_ref.shape[0]

    x = x_ref[...]
    ms = jnp.mean(x * x, axis=-1, keepdims=True)
    h = (x * lax.rsqrt(ms + EPS) * ng_ref[...]).astype(BF16)
    proj = jnp.dot(h, win_ref[...], preferred_element_type=F32)

    u = proj[:, OFF_A_VAL:OFF_A_VAL + C_CONV] * _sigmoid(proj[:, OFF_A_GLU:OFF_A_GLU + C_CONV])
    conv = u * cw_ref[CONV_HALO:CONV_W, :] + cb_ref[...]
    for w in range(CONV_HALO):
        conv = conv + sc_ref[w] * cw_ref[w:w + 1, :]
    nc_ref[0:CONV_HALO - 1] = sc_ref[1:CONV_HALO]
    nc_ref[CONV_HALO - 1] = u
    mu = jnp.mean(conv, axis=-1, keepdims=True)
    cen = conv - mu
    var = jnp.mean(cen * cen, axis=-1, keepdims=True)
    ln = cen * lax.rsqrt(var + EPS) * lng_ref[...] + lnb_ref[...]
    ya = _silu(ln) * _silu(proj[:, OFF_A_GATE:OFF_A_GATE + C_CONV])

    bval = proj[:, OFF_B_VAL:OFF_B_VAL + C_POOL]
    lane_p = lax.broadcasted_iota(jnp.int32, (nsmp, C_POOL), 1)
    pooled = jnp.zeros((nsmp, C_POOL), F32)
    acc = bval
    done = 1
    for wi, w in enumerate(POOL_WINDOWS):
        for i in range(done, w):
            acc = acc + sp_ref[POOL_BUF - i]
        done = w
        pooled = jnp.where(lane_p // POOL_GC == wi, acc / float(min(pos + 1, w)), pooled)
    np_ref[0:POOL_BUF - 1] = sp_ref[1:POOL_BUF]
    np_ref[POOL_BUF - 1] = bval
    dpool = (pooled - bval).astype(BF16)
    yb = (jnp.dot(dpool, pw_ref[...], preferred_element_type=F32) * psc_ref[...]
          * _silu(proj[:, OFF_B_GATE:OFF_B_GATE + C_POOL]))
    mixab_s[:, 0:C_CONV] = ya
    mixab_s[:, C_CONV:C_CONV + C_POOL] = yb
    cg_s[...] = _silu(proj[:, OFF_C_GATE:OFF_C_GATE + C_ATT])

    q = proj[:, OFF_Q:OFF_Q + C_ATT]
    k = proj[:, OFF_K:OFF_K + C_ATT]
    qn = q * lax.rsqrt(jnp.dot((q * q).astype(BF16), mavg_ref[...], preferred_element_type=F32) + EPS) * qg_ref[...]
    kn = k * lax.rsqrt(jnp.dot((k * k).astype(BF16), mavg_ref[...], preferred_element_type=F32) + EPS) * kg_ref[...]
    for s in range(N_ATT_SLAB):
        cols = slice(s * LANES, (s + 1) * LANES)
        qs, ks = qn[:, cols], kn[:, cols]
        qr_s[:, cols] = (qs * cos_ref[...] + _swap_halves(qs) * sin_ref[...]) * (HEAD_DIM ** -0.5)
        kr_s[:, cols] = ks * cos_ref[...] + _swap_halves(ks) * sin_ref[...]
    v_s[...] = proj[:, OFF_V:OFF_V + C_ATT]


def _decode_prepare(layer, pos, xs, state_conv, state_pool, norm_g, w_in, conv_w, conv_b, ln_g, ln_b,
                    pool_wbd, pool_scale, qg, kg, cos_s, sin_s, mavg):
    nsmp = xs.shape[0]

    def const(shape):
        nd = len(shape)
        return pl.BlockSpec(shape, lambda i: (0,) * nd)

    def per_layer(shape):
        nd = len(shape)
        return pl.BlockSpec((None,) + shape, lambda i: (layer,) + (0,) * nd)

    in_specs = [
        const((nsmp, D_MODEL)), per_layer((CONV_HALO, nsmp, C_CONV)), per_layer((POOL_BUF, nsmp, C_POOL)),
        per_layer((1, D_MODEL)), per_layer((D_MODEL, D_IN)), per_layer((CONV_W, C_CONV)), per_layer((1, C_CONV)),
        per_layer((1, C_CONV)), per_layer((1, C_CONV)), per_layer((C_POOL, C_POOL)), per_layer((1, C_POOL)),
        per_layer((1, C_ATT)), per_layer((1, C_ATT)), const((1, LANES)), const((1, LANES)), const((C_ATT, C_ATT)),
    ]
    out_dims = [(CONV_HALO, nsmp, C_CONV), (POOL_BUF, nsmp, C_POOL), (nsmp, C_ATT), (nsmp, C_ATT), (nsmp, C_ATT),
                (nsmp, C_CONV + C_POOL), (nsmp, C_ATT)]
    return pl.pallas_call(
        functools.partial(_decode_prepare_kernel, pos=pos),
        out_shape=[jax.ShapeDtypeStruct(d, F32) for d in out_dims],
        grid=(1,),
        in_specs=in_specs,
        out_specs=[const(d) for d in out_dims],
        compiler_params=pltpu.CompilerParams(dimension_semantics=("arbitrary",)),
        name="decode_prepare",
    )(xs, state_conv, state_pool, norm_g, w_in, conv_w, conv_b, ln_g, ln_b, pool_wbd, pool_scale, qg, kg,
      cos_s, sin_s, mavg)


def _sample_start(cond, smp, qr_s, kr_s, v_s, qcol_s, sm_s, sl_s, so_s, kcar_s, vcar_s):
    nsmp = qr_s.shape[0]
    n_pat = float(len(DILATIONS))

    @pl.when(cond)
    def _start():
        mine = lax.broadcasted_iota(jnp.int32, (nsmp, C_ATT), 0) == smp

        def col_tile(ref):
            return _row_to_col_tile(jnp.sum(jnp.where(mine, ref[...], 0.0), axis=0, keepdims=True))

        q_col = col_tile(qr_s)
        k_col = col_tile(kr_s)
        v_col = col_tile(v_s)
        qcol_s[...] = q_col
        kcar_s[...] = k_col
        vcar_s[...] = v_col
        so_s[...] = v_col * n_pat
        for hd in range(N_HEADS):
            hr = slice(hd * HEAD_DIM, (hd + 1) * HEAD_DIM)
            sm_s[hd:hd + 1, :] = jnp.sum(k_col[hr, :] * q_col[hr, :], axis=0, keepdims=True)
        sl_s[...] = jnp.full(sl_s.shape, n_pat, F32)


def _sample_window(lane0, kt_ref, vt_ref, okt_ref, ovt_ref, qcol_s, sm_s, sl_s, so_s, kcar_s, vcar_s, *, win_len):
    blk = kt_ref.shape[1]
    n_tiles = blk // LANES
    cnt = _key_multiplicity(lane0, blk, win_len)
    reach = cnt > 0.0
    last_lane = lax.broadcasted_iota(jnp.int32, (HEAD_DIM, blk), 1) == blk - 1

    for hd in range(N_HEADS):
        hr = slice(hd * HEAD_DIM, (hd + 1) * HEAD_DIM)
        kt = kt_ref[hr, :]
        vt = vt_ref[hr, :]
        s_win = jnp.sum(kt * jnp.concatenate([qcol_s[hr, :]] * n_tiles, axis=1), axis=0, keepdims=True)
        s_win = jnp.where(reach, s_win, NEG)
        m_old = sm_s[hd:hd + 1, :]
        m_new = jnp.maximum(m_old, jnp.max(s_win, axis=-1, keepdims=True))
        w_old = jnp.exp(m_old - m_new)
        p_win = cnt * jnp.exp(s_win - m_new[:, 0:1])
        sm_s[hd:hd + 1, :] = m_new
        sl_s[hd:hd + 1, :] = sl_s[hd:hd + 1, :] * w_old + jnp.sum(p_win, axis=-1, keepdims=True)
        so_s[hr, :] = so_s[hr, :] * w_old + jnp.sum(vt * p_win, axis=-1, keepdims=True)
        k_next = jnp.concatenate([kcar_s[hr, :]] * n_tiles, axis=1)
        v_next = jnp.concatenate([vcar_s[hr, :]] * n_tiles, axis=1)
        okt_ref[hr, :] = jnp.where(last_lane, k_next, pltpu.roll(kt, blk - 1, 1))
        ovt_ref[hr, :] = jnp.where(last_lane, v_next, pltpu.roll(vt, blk - 1, 1))
        kcar_s[hr, :] = jnp.broadcast_to(kt[:, 0:1], (HEAD_DIM, LANES))
        vcar_s[hr, :] = jnp.broadcast_to(vt[:, 0:1], (HEAD_DIM, LANES))


def _sample_done(cond, smp, att_s, sl_s, so_s):
    nsmp = att_s.shape[0]

    @pl.when(cond)
    def _done():
        cols = [so_s[hd * HEAD_DIM:(hd + 1) * HEAD_DIM, :] / sl_s[hd:hd + 1, :] for hd in range(N_HEADS)]
        att_row = jnp.concatenate(cols, axis=0).T[0:1, :]
        mine = lax.broadcasted_iota(jnp.int32, (nsmp, C_ATT), 0) == smp
        att_s[...] = jnp.where(mine, att_row, att_s[...])


def _sample_finish(cond, x_ref, wout_ref, y_ref, att_s, mixab_s, cg_s):
    @pl.when(cond)
    def _finish():
        yc = att_s[...] * cg_s[...]
        mixed = jnp.concatenate([mixab_s[...], yc], axis=-1).astype(BF16)
        y_ref[...] = x_ref[...] + jnp.dot(mixed, wout_ref[...], preferred_element_type=F32)


def _to_channel_major(a):
    depth, bsz, ntok, nh, hd = a.shape
    return jnp.transpose(a, (0, 1, 3, 4, 2)).reshape(depth, bsz, nh * hd, ntok)


def _from_channel_major(a):
    depth, bsz, _, ntok = a.shape
    return jnp.transpose(a.reshape(depth, bsz, N_HEADS, HEAD_DIM, ntok), (0, 1, 4, 2, 3))


def kernel(x_prompt, x_sample, state_conv, state_pool, cache_k_win, cache_v_win, norm_g, w_in, conv_w, conv_b,
           ln_g, ln_b, pool_w, pool_scale, q_norm_g, k_norm_g, w_out):
    depth = w_in.shape[0]
    seq_len = x_prompt.shape[1]
    nsmp = x_sample.shape[0]

    cos_p, sin_p = _rope_tables(jnp.arange(seq_len, dtype=jnp.int32))
    cos_s, sin_s = _rope_tables(jnp.full((1,), PAST_LEN, dtype=jnp.int32))
    mavg = _head_mean_matrix()
    weights = (norm_g[:, None], w_in.astype(BF16), conv_w, conv_b[:, None], ln_g[:, None], ln_b[:, None],
               _pool_block_diag(pool_w).astype(BF16), pool_scale[:, None],
               jnp.tile(q_norm_g, (1, N_HEADS))[:, None], jnp.tile(k_norm_g, (1, N_HEADS))[:, None])
    w_out_b = w_out.astype(BF16)

    kt = _to_channel_major(cache_k_win)
    vt = _to_channel_major(cache_v_win)
    xp = x_prompt
    xs = x_sample.reshape(nsmp, D_MODEL)
    state_conv_pm = jnp.swapaxes(state_conv, 1, 2)
    state_pool_pm = jnp.swapaxes(state_pool, 1, 2)
    kv = None
    conv_p, pool_p, conv_s, pool_s = [], [], [], []
    for layer in range(depth):
        ncs, nps, *decode_new = _decode_prepare(layer, PAST_LEN, xs, state_conv_pm, state_pool_pm, *weights,
                                                cos_s, sin_s, mavg)
        xp, kp, vp, cst, pst, xs, ks, vs = _trunk_layer(
            layer, xp, xs, decode_new, kt, vt, *weights, cos_p, sin_p, mavg, w_out_b, kv_prev=kv)
        kv = (kp, vp, ks, vs)
        conv_p.append(cst)
        pool_p.append(pst)
        conv_s.append(ncs)
        pool_s.append(nps)

    return (xp, xs.reshape(nsmp, 1, D_MODEL), jnp.stack(conv_p), jnp.stack(pool_p),
            _from_channel_major(kv[0]), _from_channel_major(kv[1]),
            jnp.swapaxes(jnp.stack(conv_s), 1, 2), jnp.swapaxes(jnp.stack(pool_s), 1, 2),
            _from_channel_major(kv[2]), _from_channel_major(kv[3]))
```

```python
import functools

import jax
import jax.numpy as jnp
import numpy as np
from jax import lax
from jax.experimental import pallas as pl
from jax.experimental.pallas import tpu as pltpu

F32 = jnp.float32
BF16 = jnp.bfloat16

D_MODEL = 1024
C_CONV = 384
C_POOL = 256
C_ATT = 384
HEAD_DIM = 64
N_HEADS = C_ATT // HEAD_DIM
CONV_W = 31
CONV_HALO = CONV_W - 1
POOL_WINDOWS = (2, 4, 8, 16)
POOL_GC = 64
POOL_BUF = 15
DILATIONS = (1, 4, 16)
WINDOW_KEYS = 128
EPS = 1e-6
ROPE_THETA = 10000.0
D_IN = 3 * C_CONV + 2 * C_POOL + 4 * C_ATT
PAST_LEN = 16384
NEG = -1e30
LOG2_E = 1.4426950408889634

OFF_A_VAL = 0
OFF_A_GLU = OFF_A_VAL + C_CONV
OFF_A_GATE = OFF_A_GLU + C_CONV
OFF_B_VAL = OFF_A_GATE + C_CONV
OFF_B_GATE = OFF_B_VAL + C_POOL
OFF_Q = OFF_B_GATE + C_POOL
OFF_K = OFF_Q + C_ATT
OFF_V = OFF_K + C_ATT
OFF_C_GATE = OFF_V + C_ATT

LANES = 128
N_CONV_SLAB = C_CONV // LANES
N_POOL_SLAB = C_POOL // LANES
N_ATT_SLAB = C_ATT // LANES
VMEM_LIMIT_BYTES = 60 * 1024 * 1024

TQ = 256
RC = 32
CONV_GROUP = 8
U_HALO = 32
B_HALO = 16
QB = WINDOW_KEYS
UNROLL_NEAR = 15
UNROLL_MID = 12
UNROLL_FAR = 16

AB_Q, AB_K, AB_V, AB_O, AB_LSE = range(5)
AB_COUNT = 5


def _sigmoid(x):
    return 0.5 * jnp.tanh(0.5 * x) + 0.5


def _silu(x):
    h = 0.5 * x
    return h * jnp.tanh(h) + h


def _rope_tables(positions):
    half = HEAD_DIM // 2
    inv = np.float32(ROPE_THETA) ** (-np.arange(half, dtype=np.float32) / np.float32(half))
    ang = np.asarray(positions, np.float32)[:, None] * inv[None, :]
    cos = np.cos(ang)
    sin = np.sin(ang)
    cos_h = np.concatenate([cos, cos], axis=-1)
    sin_h = np.concatenate([-sin, sin], axis=-1)
    reps = LANES // HEAD_DIM
    return jnp.asarray(np.tile(cos_h, (1, reps))), jnp.asarray(np.tile(sin_h, (1, reps)))


def _head_mean_matrix():
    idx = np.arange(C_ATT) // HEAD_DIM
    return jnp.asarray((idx[:, None] == idx[None, :]).astype(np.float32) / HEAD_DIM, dtype=BF16)


def _pool_block_diag(pool_w):
    n_grp = len(POOL_WINDOWS)
    same_group = np.eye(n_grp, dtype=np.float32)[None, :, None, :, None]
    return (pool_w[:, :, :, None, :] * same_group).reshape(pool_w.shape[0], C_POOL, C_POOL)


VEC_NORM_G = 0
VEC_CONV_B = VEC_NORM_G + D_MODEL
VEC_LN_G = VEC_CONV_B + C_CONV
VEC_LN_B = VEC_LN_G + C_CONV
VEC_POOL_SCALE = VEC_LN_B + C_CONV
VEC_Q_GAIN = VEC_POOL_SCALE + C_POOL
VEC_K_GAIN = VEC_Q_GAIN + C_ATT
VEC_LEN = VEC_K_GAIN + C_ATT


def _pack_vectors(norm_g, conv_b, ln_g, ln_b, pool_scale, q_norm_g, k_norm_g):
    parts = [norm_g, conv_b, ln_g, ln_b, pool_scale, jnp.tile(q_norm_g, (1, N_HEADS)), jnp.tile(k_norm_g, (1, N_HEADS))]
    row, off = 0.0, 0
    for p in parts:
        row = row + jnp.pad(p, ((0, 0), (off, VEC_LEN - off - p.shape[1])))
        off += p.shape[1]
    return row[:, None]


def _vector_views(vec_ref):
    bounds = (VEC_NORM_G, VEC_CONV_B, VEC_LN_G, VEC_LN_B, VEC_POOL_SCALE, VEC_Q_GAIN, VEC_K_GAIN, VEC_LEN)
    return tuple(vec_ref.at[:, lo:hi] for lo, hi in zip(bounds[:-1], bounds[1:]))


def _swap_halves(x):
    lane = lax.broadcasted_iota(jnp.int32, x.shape, 1)
    first_half = (lane % HEAD_DIM) < (HEAD_DIM // 2)
    return jnp.where(first_half, pltpu.roll(x, LANES - HEAD_DIM // 2, 1), pltpu.roll(x, HEAD_DIM // 2, 1))


def _pool_means(loads, pos):
    lane = lax.broadcasted_iota(jnp.int32, loads(0, 0).shape, 1)
    lo = lane < POOL_GC
    posf = (pos + 1).astype(F32)
    outs = []
    for slab in range(N_POOL_SLAB):
        w_lo, w_hi = POOL_WINDOWS[2 * slab], POOL_WINDOWS[2 * slab + 1]
        cur = loads(0, slab)
        s = cur
        for i in range(1, w_lo):
            s = s + loads(i, slab)
        s_lo = s
        for i in range(w_lo, w_hi):
            s = s + loads(i, slab)
        s_hi = s
        cnt_lo = jnp.minimum(posf, float(w_lo))
        cnt_hi = jnp.minimum(posf, float(w_hi))
        pooled = jnp.where(lo, s_lo / cnt_lo, s_hi / cnt_hi)
        outs.append(pooled - cur)
    return outs


N_LAYER_INPUTS = 18


def _layer_kernel(x_ref, x2_ref, vec_ref, win_ref, cw_ref, pw_ref, cos_ref, sin_ref, mavg_ref, wout_ref,
                  xs_ref, qr_s, kr_s, v_s, mixab_s, cgs_s, kt_ref, vt_ref,
                  y_ref, ko_ref, vo_ref, cst_ref, pst_ref,
                  ys_ref, okt_ref, ovt_ref,
                  h_s, proj, u_buf, b_buf,
                  st_q, st_k, st_v, st_att, ab0, ab1, ab2, ab3, ab4, mix,
                  att_s, qcol_s, sm_s, sl_s, so_s, kcar_s, vcar_s,
                  *, seq_len, win_len):
    ab = (ab0, ab1, ab2, ab3, ab4)
    ng_ref, cb_ref, lng_ref, lnb_ref, psc_ref, qg_ref, kg_ref = _vector_views(vec_ref)
    nt = seq_len // TQ
    b = pl.program_id(0)
    t = pl.program_id(1)
    n_batch = pl.num_programs(0) - 1
    blocks_per_sample = win_len // kt_ref.shape[1]
    row0 = pl.multiple_of(t * TQ, TQ)
    sq_s = h_s.at[:, 0:2 * C_ATT]
    d_s = h_s.at[:, 2 * C_ATT:2 * C_ATT + C_POOL]
    msq_s = proj.at[:, OFF_A_VAL:OFF_A_VAL + 2 * C_ATT]
    yb_s = proj.at[:, OFF_B_VAL:OFF_B_VAL + C_POOL]
    conv_s = st_att

    step = b * nt + t
    part = step % blocks_per_sample
    smp = step // blocks_per_sample

    def sample_start():
        _sample_start(part == 0, smp, qr_s, kr_s, v_s, qcol_s, sm_s, sl_s, so_s, kcar_s, vcar_s)

    def sample_window():
        _sample_window((blocks_per_sample - 1 - part) * kt_ref.shape[1], kt_ref, vt_ref, okt_ref, ovt_ref,
                       qcol_s, sm_s, sl_s, so_s, kcar_s, vcar_s, win_len=win_len)

    def sample_done():
        _sample_done(part == blocks_per_sample - 1, smp, att_s, sl_s, so_s)

    def phase2():
        d_mid = DILATIONS[1]
        per_stream = TQ // d_mid
        for s in range(N_ATT_SLAB):
            for r in range(d_mid):
                src_rows = pl.ds(pl.multiple_of(r * (seq_len // d_mid) + t * per_stream, per_stream), per_stream)
                st_att[s, pl.ds(r, per_stream, stride=d_mid), :] = ab[AB_O][s, src_rows, :]

        def att_chunk(i, c):
            r = pl.multiple_of(i * RC, RC)
            grow = pl.ds(pl.multiple_of(row0 + r, RC), RC)
            for s in range(N_ATT_SLAB):
                mcols = slice(C_CONV + C_POOL + s * LANES, C_CONV + C_POOL + (s + 1) * LANES)
                yc = st_att[s, pl.ds(r, RC), :] * mix[grow, mcols].astype(F32)
                mix[grow, mcols] = yc.astype(BF16)
            return c

        lax.fori_loop(0, TQ // RC, att_chunk, 0, unroll=True)
        y_ref[...] = x2_ref[...] + jnp.dot(mix[pl.ds(row0, TQ), :], wout_ref[...], preferred_element_type=F32)

    def phase1(with_phase2):
        @pl.when(t == 0)
        def _zero_halo():
            u_buf[:, 0:U_HALO, :] = jnp.zeros((N_CONV_SLAB, U_HALO, LANES), F32)
            b_buf[:, 0:B_HALO, :] = jnp.zeros((N_POOL_SLAB, B_HALO, LANES), F32)
            if not with_phase2:
                att_s[...] = jnp.zeros(att_s.shape, F32)

        sample_start()
        if with_phase2:
            phase2()
        sample_window()

        def norm_chunk(i, c):
            r = pl.multiple_of(i * RC, RC)
            x = x_ref[pl.ds(r, RC), :]
            ms = jnp.mean(x * x, axis=-1, keepdims=True)
            h_s[pl.ds(r, RC), :] = (x * lax.rsqrt(ms + EPS) * ng_ref[...]).astype(BF16)
            return c

        lax.fori_loop(0, TQ // RC, norm_chunk, 0, unroll=True)
        proj[...] = jnp.dot(h_s[...], win_ref[...], preferred_element_type=F32)

        def split_chunk(i, c):
            r = pl.multiple_of(i * RC, RC)
            rows = pl.ds(r, RC)
            grow = pl.ds(pl.multiple_of(row0 + r, RC), RC)
            for s in range(N_CONV_SLAB):
                cols = slice(s * LANES, (s + 1) * LANES)
                a_val = proj[rows, OFF_A_VAL + s * LANES:OFF_A_VAL + (s + 1) * LANES]
                a_glu = proj[rows, OFF_A_GLU + s * LANES:OFF_A_GLU + (s + 1) * LANES]
                u_buf[s, pl.ds(U_HALO + r, RC), :] = a_val * _sigmoid(a_glu)
                q = proj[rows, OFF_Q + s * LANES:OFF_Q + (s + 1) * LANES]
                k = proj[rows, OFF_K + s * LANES:OFF_K + (s + 1) * LANES]
                sq_s[rows, cols] = (q * q).astype(BF16)
                sq_s[rows, C_ATT + s * LANES:C_ATT + (s + 1) * LANES] = (k * k).astype(BF16)
                st_v[s, rows, :] = proj[rows, OFF_V + s * LANES:OFF_V + (s + 1) * LANES]
                mix[grow, C_CONV + C_POOL + s * LANES:C_CONV + C_POOL + (s + 1) * LANES] = _silu(
                    proj[rows, OFF_C_GATE + s * LANES:OFF_C_GATE + (s + 1) * LANES]).astype(BF16)
            for s in range(N_POOL_SLAB):
                b_buf[s, pl.ds(B_HALO + r, RC), :] = proj[rows, OFF_B_VAL + s * LANES:OFF_B_VAL + (s + 1) * LANES]
            pos = row0 + r + lax.broadcasted_iota(jnp.int32, (RC, 1), 0)
            dl = _pool_means(lambda sh, s: b_buf[s, pl.ds(r + B_HALO - sh, RC), :], pos)
            for s in range(N_POOL_SLAB):
                d_s[rows, s * LANES:(s + 1) * LANES] = dl[s].astype(BF16)
            return c

        lax.fori_loop(0, TQ // RC, split_chunk, 0, unroll=True)
        msq_s[:, 0:C_ATT] = jnp.dot(sq_s[:, 0:C_ATT], mavg_ref[...], preferred_element_type=F32)
        msq_s[:, C_ATT:2 * C_ATT] = jnp.dot(sq_s[:, C_ATT:2 * C_ATT], mavg_ref[...], preferred_element_type=F32)
        yb_s[...] = jnp.dot(d_s[...], pw_ref[...], preferred_element_type=F32)

        row_groups = TQ // CONV_GROUP
        for s in range(N_CONV_SLAB):
            cols = slice(s * LANES, (s + 1) * LANES)
            acc = [jnp.broadcast_to(cb_ref[:, cols], (CONV_GROUP, LANES))] * row_groups
            for a in range(CONV_GROUP):
                taps = list(range(a, CONV_W, CONV_GROUP))
                cws = [jnp.broadcast_to(cw_ref[w:w + 1, cols], (CONV_GROUP, LANES)) for w in taps]
                for j in range(row_groups + len(taps) - 1):
                    start = (U_HALO - CONV_HALO) + a + CONV_GROUP * j
                    window = u_buf[s, start:start + CONV_GROUP, :]
                    for m in range(len(taps)):
                        if 0 <= j - m < row_groups:
                            acc[j - m] = acc[j - m] + window * cws[m]
            for g in range(row_groups):
                conv_s[s, g * CONV_GROUP:(g + 1) * CONV_GROUP, :] = acc[g]

        def mixer_chunk(i, c):
            r = pl.multiple_of(i * RC, RC)
            rows = pl.ds(r, RC)
            grow = pl.ds(pl.multiple_of(row0 + r, RC), RC)
            conv = [conv_s[s, rows, :] for s in range(N_CONV_SLAB)]
            mu = jnp.sum(conv[0] + conv[1] + conv[2], axis=-1, keepdims=True) * (1.0 / C_CONV)
            cen = [cv - mu for cv in conv]
            var = jnp.sum(cen[0] * cen[0] + cen[1] * cen[1] + cen[2] * cen[2], axis=-1, keepdims=True) * (1.0 / C_CONV)
            rstd = lax.rsqrt(var + EPS)
            for s in range(N_CONV_SLAB):
                cols = slice(s * LANES, (s + 1) * LANES)
                ln = cen[s] * rstd * lng_ref[:, cols] + lnb_ref[:, cols]
                gate = proj[rows, OFF_A_GATE + s * LANES:OFF_A_GATE + (s + 1) * LANES]
                mix[grow, cols] = (_silu(ln) * _silu(gate)).astype(BF16)
            for s in range(N_POOL_SLAB):
                cols = slice(s * LANES, (s + 1) * LANES)
                gate = proj[rows, OFF_B_GATE + s * LANES:OFF_B_GATE + (s + 1) * LANES]
                yb = yb_s[rows, cols] * psc_ref[:, cols] * _silu(gate)
                mix[grow, C_CONV + s * LANES:C_CONV + (s + 1) * LANES] = yb.astype(BF16)
            cos = cos_ref[rows, :]
            sin = sin_ref[rows, :]
            for s in range(N_ATT_SLAB):
                cols = slice(s * LANES, (s + 1) * LANES)
                q = proj[rows, OFF_Q + s * LANES:OFF_Q + (s + 1) * LANES]
                qn = q * lax.rsqrt(msq_s[rows, cols] + EPS) * qg_ref[:, cols]
                qr = qn * cos + _swap_halves(qn) * sin
                st_q[s, rows, :] = qr * (HEAD_DIM ** -0.5 * LOG2_E)
                k = proj[rows, OFF_K + s * LANES:OFF_K + (s + 1) * LANES]
                kn = k * lax.rsqrt(msq_s[rows, C_ATT + s * LANES:C_ATT + (s + 1) * LANES] + EPS) * kg_ref[:, cols]
                st_k[s, rows, :] = kn * cos + _swap_halves(kn) * sin
            return c

        lax.fori_loop(0, TQ // RC, mixer_chunk, 0, unroll=True)

        d_mid = DILATIONS[1]
        per_stream = TQ // d_mid
        for s in range(N_ATT_SLAB):
            ko_ref[s * LANES:(s + 1) * LANES, :] = st_k[s].T
            vo_ref[s * LANES:(s + 1) * LANES, :] = st_v[s].T
            for st, dst in ((st_q, AB_Q), (st_k, AB_K), (st_v, AB_V)):
                for r in range(d_mid):
                    dst_rows = pl.ds(pl.multiple_of(r * (seq_len // d_mid) + t * per_stream, per_stream), per_stream)
                    ab[dst][s, dst_rows, :] = st[s, pl.ds(r, per_stream, stride=d_mid), :]

        @pl.when(t == nt - 1)
        def _write_state():
            for s in range(N_CONV_SLAB):
                cst_ref[:, s * LANES:(s + 1) * LANES] = u_buf[s, TQ + U_HALO - CONV_HALO:TQ + U_HALO, :]
            for s in range(N_POOL_SLAB):
                pst_ref[:, s * LANES:(s + 1) * LANES] = b_buf[s, TQ + B_HALO - POOL_BUF:TQ + B_HALO, :]

        u_buf[:, 0:U_HALO, :] = u_buf[:, TQ:TQ + U_HALO, :]
        b_buf[:, 0:B_HALO, :] = b_buf[:, TQ:TQ + B_HALO, :]
        sample_done()

    def attention():
        d_mid, d_far = DILATIONS[1], DILATIONS[2]
        ratio = d_far // d_mid
        stream_len = seq_len // d_mid
        piece = QB // d_mid
        lane = lax.broadcasted_iota(jnp.int32, (QB, LANES), 1)
        rowi = lax.broadcasted_iota(jnp.int32, (QB, LANES), 0)
        lo = lane < HEAD_DIM

        def masks(key_pos, query_pos):
            cur = key_pos <= query_pos
            prev = key_pos >= query_pos
            cur2 = jnp.concatenate([cur, cur], axis=0)
            return cur2, jnp.concatenate([jnp.concatenate([prev, prev], axis=0), cur2], axis=1)

        cur_ok2, prev_cur_ok2 = masks(lane, rowi)
        ncur_ok2, nprev_cur_ok2 = masks(d_mid * (lane % piece) + lane // piece, d_mid * (rowi % piece) + rowi // piece)

        def attend(q, keys, vals, mask):
            qa = jnp.where(lo, q, 0.0).astype(BF16)
            qb = jnp.where(lo, 0.0, q).astype(BF16)
            q2 = jnp.concatenate([qa, qb], axis=0)
            sc = lax.dot_general(q2, keys.astype(BF16), (((1,), (1,)), ((), ())), preferred_element_type=F32)
            sc = jnp.where(mask, sc, NEG)
            m = jnp.max(sc, axis=-1, keepdims=True)
            p = jnp.exp2(sc - m).astype(BF16)
            v1 = jnp.concatenate([vals.astype(BF16), jnp.ones(vals.shape, BF16)], axis=1)
            ol = jnp.dot(p, v1, preferred_element_type=F32)
            o_u = jnp.where(lo, ol[0:QB, 0:LANES], ol[QB:2 * QB, 0:LANES])
            l_u = jnp.where(lo, ol[0:QB, LANES:2 * LANES], ol[QB:2 * QB, LANES:2 * LANES])
            m_u = jnp.where(lo, m[0:QB], m[QB:2 * QB])
            return o_u, m_u, l_u

        def load(ref, s, pieces):
            tiles = [ref[s, p, :] for p in pieces]
            return tiles[0] if len(tiles) == 1 else jnp.concatenate(tiles, axis=0)

        def save(ref, s, pieces, val):
            n = val.shape[0] // len(pieces)
            for j, p in enumerate(pieces):
                ref[s, p, :] = val[j * n:(j + 1) * n]

        def block(rows, krows, mask, merge, with_lse=True):
            outs = []
            for s in range(N_ATT_SLAB):
                o_u, m_u, l_u = attend(load(ab[AB_Q], s, rows), load(ab[AB_K], s, krows), load(ab[AB_V], s, krows),
                                       mask)
                if merge:
                    lse_old = load(ab[AB_LSE], s, rows)
                    m_new = jnp.maximum(lse_old, m_u)
                    w_old = jnp.exp2(lse_old - m_new)
                    w_u = jnp.exp2(m_u - m_new)
                    den = w_old + l_u * w_u
                    o_n = (load(ab[AB_O], s, rows) * w_old + o_u * w_u) * (1.0 / den)
                else:
                    m_new, den = m_u, l_u
                    o_n = o_u * (1.0 / l_u)
                outs.append((o_n, m_new + jnp.log2(den) if with_lse else None))
            return outs

        def store(rows, outs):
            for s, (o_n, lse_n) in enumerate(outs):
                save(ab[AB_O], s, rows, o_n)
                if lse_n is not None:
                    save(ab[AB_LSE], s, rows, lse_n)

        def token_block(base):
            return [pl.ds(r * stream_len + base, piece) for r in range(d_mid)]

        store(token_block(0), block(token_block(0), token_block(0), ncur_ok2, merge=False))

        def near_unit(i, c):
            base = pl.multiple_of(i * piece, piece)
            rows = token_block(base)
            store(rows, block(rows, token_block(base - piece) + rows, nprev_cur_ok2, merge=False))
            return c

        lax.fori_loop(1, seq_len // QB, near_unit, 0, unroll=UNROLL_NEAR)

        def mid_first(ph, c):
            rows = [pl.ds(pl.multiple_of(ph * stream_len, QB), QB)]
            store(rows, block(rows, rows, cur_ok2, merge=True))
            return c

        lax.fori_loop(0, d_mid, mid_first, 0, unroll=True)
        later_blocks = stream_len // QB - 1

        def mid_unit(i, c):
            start = pl.multiple_of((i // later_blocks) * stream_len + (i % later_blocks + 1) * QB, QB)
            rows = [pl.ds(start, QB)]
            store(rows, block(rows, [pl.ds(start - QB, 2 * QB)], prev_cur_ok2, merge=True))
            return c

        lax.fori_loop(0, d_mid * later_blocks, mid_unit, 0, unroll=UNROLL_MID)

        def far_unit(r, c):
            rows = [pl.ds((r % d_mid) * stream_len + r // d_mid, QB, stride=ratio)]
            store(rows, block(rows, rows, cur_ok2, merge=True, with_lse=False))
            return c

        lax.fori_loop(0, d_far, far_unit, 0, unroll=UNROLL_FAR)

    @pl.when(b == 0)
    def _first_row():
        phase1(with_phase2=False)

    @pl.when(jnp.logical_and(b > 0, b < n_batch))
    def _steady_rows():
        phase1(with_phase2=True)

    @pl.when(b == n_batch)
    def _last_row():
        phase2()
        _sample_finish(t == nt - 1, xs_ref, wout_ref, ys_ref, att_s, mixab_s, cgs_s)

    pl.when(jnp.logical_and(t == nt - 1, b < n_batch))(attention)


def _layer_kernel_with_alias(*refs, n_alias, **kw):
    _layer_kernel(*refs[:N_LAYER_INPUTS], *refs[N_LAYER_INPUTS + n_alias:], **kw)


def _trunk_layer(layer, x, xs, decode_new, kt, vt, vecs, w_in, conv_w, pool_wbd, cos_t, sin_t, mavg, w_out,
                 kv_prev=None):
    bsz, seq_len, _ = x.shape
    nsmp = xs.shape[0]
    depth, _, _, win_len = kt.shape
    assert seq_len % TQ == 0 and seq_len == QB * DILATIONS[-1] and DILATIONS[0] == 1
    assert win_len == WINDOW_KEYS * DILATIONS[-1]
    nt = seq_len // TQ
    blocks_per_sample, rem = divmod(bsz * nt, nsmp)
    assert rem == 0 and blocks_per_sample >= 1 and win_len % (blocks_per_sample * LANES) == 0
    win_blk = win_len // blocks_per_sample

    def p1(b, t):
        return jnp.minimum(b, bsz - 1), jnp.where(b == bsz, nt - 1, t)

    def p2(b, t):
        return jnp.maximum(b - 1, 0), jnp.where(b == 0, 0, t)

    def sample_block(b, t):
        pb, pt = p1(b, t)
        step = pb * nt + pt
        return (layer, step // blocks_per_sample, 0, blocks_per_sample - 1 - step % blocks_per_sample)

    def const(shape):
        nd = len(shape)
        return pl.BlockSpec(shape, lambda b, t: (0,) * nd)

    def resident(shape):
        nd = len(shape)
        return pl.BlockSpec(shape, lambda b, t: (0,) * nd, pipeline_mode=pl.Buffered(1))

    def per_layer(shape, **kw):
        nd = len(shape)
        return pl.BlockSpec((None,) + shape, lambda b, t: (layer,) + (0,) * nd, **kw)

    in_specs = [
        pl.BlockSpec((None, TQ, D_MODEL), lambda b, t: (*p1(b, t), 0)),
        pl.BlockSpec((None, TQ, D_MODEL), lambda b, t: (*p2(b, t), 0)),
        per_layer((1, VEC_LEN)),
        per_layer((D_MODEL, D_IN), pipeline_mode=pl.Buffered(1)),
        per_layer((CONV_W, C_CONV)),
        per_layer((C_POOL, C_POOL)),
        pl.BlockSpec((TQ, LANES), lambda b, t: (p1(b, t)[1], 0)),
        pl.BlockSpec((TQ, LANES), lambda b, t: (p1(b, t)[1], 0)),
        const((C_ATT, C_ATT)),
        per_layer((D_MODEL, D_MODEL), pipeline_mode=pl.Buffered(1)),
        const((nsmp, D_MODEL)),
        const((nsmp, C_ATT)),
        const((nsmp, C_ATT)),
        const((nsmp, C_ATT)),
        const((nsmp, C_CONV + C_POOL)),
        const((nsmp, C_ATT)),
        pl.BlockSpec((None, None, C_ATT, win_blk), sample_block),
        pl.BlockSpec((None, None, C_ATT, win_blk), sample_block),
    ]
    operands = [x, x, vecs, w_in, conv_w, pool_wbd, cos_t, sin_t, mavg, w_out, xs, *decode_new, kt, vt]
    assert len(operands) == N_LAYER_INPUTS
    kv_spec = pl.BlockSpec((None, None, C_ATT, TQ), lambda b, t: (layer, p1(b, t)[0], 0, p1(b, t)[1]))
    out_specs = [
        pl.BlockSpec((None, TQ, D_MODEL), lambda b, t: (*p2(b, t), 0)),
        kv_spec,
        kv_spec,
        pl.BlockSpec((None, CONV_HALO, C_CONV), lambda b, t: (p1(b, t)[0], 0, 0)),
        pl.BlockSpec((None, POOL_BUF, C_POOL), lambda b, t: (p1(b, t)[0], 0, 0)),
        const((nsmp, D_MODEL)),
        pl.BlockSpec((None, None, C_ATT, win_blk), sample_block),
        pl.BlockSpec((None, None, C_ATT, win_blk), sample_block),
    ]
    out_shape = [
        jax.ShapeDtypeStruct((bsz, seq_len, D_MODEL), F32),
        jax.ShapeDtypeStruct((depth, bsz, C_ATT, seq_len), F32),
        jax.ShapeDtypeStruct((depth, bsz, C_ATT, seq_len), F32),
        jax.ShapeDtypeStruct((bsz, CONV_HALO, C_CONV), F32),
        jax.ShapeDtypeStruct((bsz, POOL_BUF, C_POOL), F32),
        jax.ShapeDtypeStruct((nsmp, D_MODEL), F32),
        jax.ShapeDtypeStruct((depth, nsmp, C_ATT, win_len), F32),
        jax.ShapeDtypeStruct((depth, nsmp, C_ATT, win_len), F32),
    ]
    static = dict(seq_len=seq_len, win_len=win_len)
    if kv_prev is None:
        kern = functools.partial(_layer_kernel, **static)
        aliases = {}
    else:
        kern = functools.partial(_layer_kernel_with_alias, n_alias=len(kv_prev), **static)
        in_specs += [pl.BlockSpec(memory_space=pl.ANY)] * len(kv_prev)
        operands += list(kv_prev)
        aliases = {N_LAYER_INPUTS + i: o for i, o in enumerate((1, 2, 6, 7))}
    scratch = [
        pltpu.VMEM((TQ, D_MODEL), BF16),
        pltpu.VMEM((TQ, D_IN), F32),
        pltpu.VMEM((N_CONV_SLAB, TQ + U_HALO, LANES), F32),
        pltpu.VMEM((N_POOL_SLAB, TQ + B_HALO, LANES), F32),
        *[pltpu.VMEM((N_ATT_SLAB, TQ, LANES), F32) for _ in range(4)],
        *[pltpu.VMEM((N_ATT_SLAB, seq_len, LANES), F32) for _ in range(AB_COUNT)],
        pltpu.VMEM((seq_len, D_MODEL), BF16),
        pltpu.VMEM((nsmp, C_ATT), F32),
        pltpu.VMEM((C_ATT, LANES), F32),
        pltpu.VMEM((N_HEADS, LANES), F32),
        pltpu.VMEM((N_HEADS, LANES), F32),
        pltpu.VMEM((C_ATT, LANES), F32),
        pltpu.VMEM((C_ATT, LANES), F32),
        pltpu.VMEM((C_ATT, LANES), F32),
    ]
    return pl.pallas_call(
        kern,
        out_shape=out_shape,
        grid=(bsz + 1, nt),
        in_specs=in_specs,
        out_specs=out_specs,
        scratch_shapes=scratch,
        input_output_aliases=aliases,
        compiler_params=pltpu.CompilerParams(
            dimension_semantics=("arbitrary", "arbitrary"),
            vmem_limit_bytes=VMEM_LIMIT_BYTES,
        ),
        name="trunk_layer",
    )(*operands)


def _row_to_col_tile(row):
    return jnp.broadcast_to(row, (LANES, row.shape[1])).T


def _key_multiplicity(lane0, blk, win_len):
    t = lane0 + lax.broadcasted_iota(jnp.int32, (1, blk), 1)
    delta = win_len - t
    cnt = jnp.zeros((1, blk), F32)
    for d in DILATIONS:
        hit = jnp.logical_and(delta % d == 0, delta <= d * WINDOW_KEYS)
        cnt = cnt + hit.astype(F32)
    return cnt


def _decode_prepare_kernel(x_ref, sc_ref, sp_ref, vec_ref, win_ref, cw_ref, pw_ref, cos_ref, sin_ref, mavg_ref,
                           nc_ref, np_ref, qr_s, kr_s, v_s, mixab_s, cg_s, *, pos):
    nsmp = x_ref.shape[0]
    ng_ref, cb_ref, lng_ref, lnb_ref, psc_ref, qg_ref, kg_ref = _vector_views(vec_ref)

    x = x_ref[...]
    ms = jnp.mean(x * x, axis=-1, keepdims=True)
    h = (x * lax.rsqrt(ms + EPS) * ng_ref[...]).astype(BF16)
    proj = jnp.dot(h, win_ref[...], preferred_element_type=F32)

    u = proj[:, OFF_A_VAL:OFF_A_VAL + C_CONV] * _sigmoid(proj[:, OFF_A_GLU:OFF_A_GLU + C_CONV])
    conv = u * cw_ref[CONV_HALO:CONV_W, :] + cb_ref[...]
    for w in range(CONV_HALO):
        conv = conv + sc_ref[w] * cw_ref[w:w + 1, :]
    nc_ref[0:CONV_HALO - 1] = sc_ref[1:CONV_HALO]
    nc_ref[CONV_HALO - 1] = u
    mu = jnp.mean(conv, axis=-1, keepdims=True)
    cen = conv - mu
    var = jnp.mean(cen * cen, axis=-1, keepdims=True)
    ln = cen * lax.rsqrt(var + EPS) * lng_ref[...] + lnb_ref[...]
    ya = _silu(ln) * _silu(proj[:, OFF_A_GATE:OFF_A_GATE + C_CONV])

    bval = proj[:, OFF_B_VAL:OFF_B_VAL + C_POOL]
    lane_p = lax.broadcasted_iota(jnp.int32, (nsmp, C_POOL), 1)
    pooled = jnp.zeros((nsmp, C_POOL), F32)
    acc = bval
    done = 1
    for wi, w in enumerate(POOL_WINDOWS):
        for i in range(done, w):
            acc = acc + sp_ref[POOL_BUF - i]
        done = w
        pooled = jnp.where(lane_p // POOL_GC == wi, acc / float(min(pos + 1, w)), pooled)
    np_ref[0:POOL_BUF - 1] = sp_ref[1:POOL_BUF]
    np_ref[POOL_BUF - 1] = bval
    dpool = (pooled - bval).astype(BF16)
    yb = (jnp.dot(dpool, pw_ref[...], preferred_element_type=F32) * psc_ref[...]
          * _silu(proj[:, OFF_B_GATE:OFF_B_GATE + C_POOL]))
    mixab_s[:, 0:C_CONV] = ya
    mixab_s[:, C_CONV:C_CONV + C_POOL] = yb
    cg_s[...] = _silu(proj[:, OFF_C_GATE:OFF_C_GATE + C_ATT])

    q = proj[:, OFF_Q:OFF_Q + C_ATT]
    k = proj[:, OFF_K:OFF_K + C_ATT]
    qn = q * lax.rsqrt(jnp.dot((q * q).astype(BF16), mavg_ref[...], preferred_element_type=F32) + EPS) * qg_ref[...]
    kn = k * lax.rsqrt(jnp.dot((k * k).astype(BF16), mavg_ref[...], preferred_element_type=F32) + EPS) * kg_ref[...]
    for s in range(N_ATT_SLAB):
        cols = slice(s * LANES, (s + 1) * LANES)
        qs, ks = qn[:, cols], kn[:, cols]
        qr_s[:, cols] = (qs * cos_ref[...] + _swap_halves(qs) * sin_ref[...]) * (HEAD_DIM ** -0.5)
        kr_s[:, cols] = ks * cos_ref[...] + _swap_halves(ks) * sin_ref[...]
    v_s[...] = proj[:, OFF_V:OFF_V + C_ATT]


def _decode_prepare(layer, pos, xs, state_conv, state_pool, vecs, w_in, conv_w, pool_wbd, cos_s, sin_s, mavg):
    nsmp = xs.shape[0]

    def const(shape):
        nd = len(shape)
        return pl.BlockSpec(shape, lambda i: (0,) * nd)

    def per_layer(shape):
        nd = len(shape)
        return pl.BlockSpec((None,) + shape, lambda i: (layer,) + (0,) * nd)

    in_specs = [
        const((nsmp, D_MODEL)), per_layer((CONV_HALO, nsmp, C_CONV)), per_layer((POOL_BUF, nsmp, C_POOL)),
        per_layer((1, VEC_LEN)), per_layer((D_MODEL, D_IN)), per_layer((CONV_W, C_CONV)), per_layer((C_POOL, C_POOL)),
        const((1, LANES)), const((1, LANES)), const((C_ATT, C_ATT)),
    ]
    out_dims = [(CONV_HALO, nsmp, C_CONV), (POOL_BUF, nsmp, C_POOL), (nsmp, C_ATT), (nsmp, C_ATT), (nsmp, C_ATT),
                (nsmp, C_CONV + C_POOL), (nsmp, C_ATT)]
    return pl.pallas_call(
        functools.partial(_decode_prepare_kernel, pos=pos),
        out_shape=[jax.ShapeDtypeStruct(d, F32) for d in out_dims],
        grid=(1,),
        in_specs=in_specs,
        out_specs=[const(d) for d in out_dims],
        compiler_params=pltpu.CompilerParams(dimension_semantics=("arbitrary",)),
        name="decode_prepare",
    )(xs, state_conv, state_pool, vecs, w_in, conv_w, pool_wbd, cos_s, sin_s, mavg)


def _sample_start(cond, smp, qr_s, kr_s, v_s, qcol_s, sm_s, sl_s, so_s, kcar_s, vcar_s):
    nsmp = qr_s.shape[0]
    n_pat = float(len(DILATIONS))

    @pl.when(cond)
    def _start():
        mine = lax.broadcasted_iota(jnp.int32, (nsmp, C_ATT), 0) == smp

        def col_tile(ref):
            return _row_to_col_tile(jnp.sum(jnp.where(mine, ref[...], 0.0), axis=0, keepdims=True))

        q_col = col_tile(qr_s)
        k_col = col_tile(kr_s)
        v_col = col_tile(v_s)
        qcol_s[...] = q_col
        kcar_s[...] = k_col
        vcar_s[...] = v_col
        so_s[...] = v_col * n_pat
        for hd in range(N_HEADS):
            hr = slice(hd * HEAD_DIM, (hd + 1) * HEAD_DIM)
            sm_s[hd:hd + 1, :] = jnp.sum(k_col[hr, :] * q_col[hr, :], axis=0, keepdims=True)
        sl_s[...] = jnp.full(sl_s.shape, n_pat, F32)


def _sample_window(lane0, kt_ref, vt_ref, okt_ref, ovt_ref, qcol_s, sm_s, sl_s, so_s, kcar_s, vcar_s, *, win_len):
    blk = kt_ref.shape[1]
    n_tiles = blk // LANES
    cnt = _key_multiplicity(lane0, blk, win_len)
    reach = cnt > 0.0
    last_lane = lax.broadcasted_iota(jnp.int32, (HEAD_DIM, blk), 1) == blk - 1

    for hd in range(N_HEADS):
        hr = slice(hd * HEAD_DIM, (hd + 1) * HEAD_DIM)
        kt = kt_ref[hr, :]
        vt = vt_ref[hr, :]
        s_win = jnp.sum(kt * jnp.concatenate([qcol_s[hr, :]] * n_tiles, axis=1), axis=0, keepdims=True)
        s_win = jnp.where(reach, s_win, NEG)
        m_old = sm_s[hd:hd + 1, :]
        m_new = jnp.maximum(m_old, jnp.max(s_win, axis=-1, keepdims=True))
        w_old = jnp.exp(m_old - m_new)
        p_win = cnt * jnp.exp(s_win - m_new[:, 0:1])
        sm_s[hd:hd + 1, :] = m_new
        sl_s[hd:hd + 1, :] = sl_s[hd:hd + 1, :] * w_old + jnp.sum(p_win, axis=-1, keepdims=True)
        so_s[hr, :] = so_s[hr, :] * w_old + jnp.sum(vt * p_win, axis=-1, keepdims=True)
        k_next = jnp.concatenate([kcar_s[hr, :]] * n_tiles, axis=1)
        v_next = jnp.concatenate([vcar_s[hr, :]] * n_tiles, axis=1)
        okt_ref[hr, :] = jnp.where(last_lane, k_next, pltpu.roll(kt, blk - 1, 1))
        ovt_ref[hr, :] = jnp.where(last_lane, v_next, pltpu.roll(vt, blk - 1, 1))
        kcar_s[hr, :] = jnp.broadcast_to(kt[:, 0:1], (HEAD_DIM, LANES))
        vcar_s[hr, :] = jnp.broadcast_to(vt[:, 0:1], (HEAD_DIM, LANES))


def _sample_done(cond, smp, att_s, sl_s, so_s):
    nsmp = att_s.shape[0]

    @pl.when(cond)
    def _done():
        cols = [so_s[hd * HEAD_DIM:(hd + 1) * HEAD_DIM, :] / sl_s[hd:hd + 1, :] for hd in range(N_HEADS)]
        att_row = jnp.concatenate(cols, axis=0).T[0:1, :]
        mine = lax.broadcasted_iota(jnp.int32, (nsmp, C_ATT), 0) == smp
        att_s[...] = jnp.where(mine, att_row, att_s[...])


def _sample_finish(cond, x_ref, wout_ref, y_ref, att_s, mixab_s, cg_s):
    @pl.when(cond)
    def _finish():
        yc = att_s[...] * cg_s[...]
        mixed = jnp.concatenate([mixab_s[...], yc], axis=-1).astype(BF16)
        y_ref[...] = x_ref[...] + jnp.dot(mixed, wout_ref[...], preferred_element_type=F32)


def _to_channel_major(a):
    depth, bsz, ntok, nh, hd = a.shape
    return jnp.transpose(a, (0, 1, 3, 4, 2)).reshape(depth, bsz, nh * hd, ntok)


def _from_channel_major(a):
    depth, bsz, _, ntok = a.shape
    return jnp.transpose(a.reshape(depth, bsz, N_HEADS, HEAD_DIM, ntok), (0, 1, 4, 2, 3))


def kernel(x_prompt, x_sample, state_conv, state_pool, cache_k_win, cache_v_win, norm_g, w_in, conv_w, conv_b,
           ln_g, ln_b, pool_w, pool_scale, q_norm_g, k_norm_g, w_out):
    depth = w_in.shape[0]
    seq_len = x_prompt.shape[1]
    nsmp = x_sample.shape[0]

    cos_p, sin_p = _rope_tables(np.arange(seq_len))
    cos_s, sin_s = _rope_tables(np.full((1,), PAST_LEN))
    mavg = _head_mean_matrix()
    weights = (_pack_vectors(norm_g, conv_b, ln_g, ln_b, pool_scale, q_norm_g, k_norm_g), w_in.astype(BF16), conv_w,
               _pool_block_diag(pool_w).astype(BF16))
    w_out_b = w_out.astype(BF16)

    kt = _to_channel_major(cache_k_win)
    vt = _to_channel_major(cache_v_win)
    xp = x_prompt
    xs = x_sample.reshape(nsmp, D_MODEL)
    state_conv_pm = jnp.swapaxes(state_conv, 1, 2)
    state_pool_pm = jnp.swapaxes(state_pool, 1, 2)
    kv = None
    conv_p, pool_p, conv_s, pool_s = [], [], [], []
    for layer in range(depth):
        ncs, nps, *decode_new = _decode_prepare(layer, PAST_LEN, xs, state_conv_pm, state_pool_pm, *weights,
                                                cos_s, sin_s, mavg)
        xp, kp, vp, cst, pst, xs, ks, vs = _trunk_layer(
            layer, xp, xs, decode_new, kt, vt, *weights, cos_p, sin_p, mavg, w_out_b, kv_prev=kv)
        kv = (kp, vp, ks, vs)
        conv_p.append(cst)
        pool_p.append(pst)
        conv_s.append(ncs)
        pool_s.append(nps)

    return (xp, xs.reshape(nsmp, 1, D_MODEL), jnp.stack(conv_p), jnp.stack(pool_p),
            _from_channel_major(kv[0]), _from_channel_major(kv[1]),
            jnp.swapaxes(jnp.stack(conv_s), 1, 2), jnp.swapaxes(jnp.stack(pool_s), 1, 2),
            _from_channel_major(kv[2]), _from_channel_major(kv[3]))
```

```python
import functools

import jax
import jax.numpy as jnp
import numpy as np
from jax import lax
from jax.experimental import pallas as pl
from jax.experimental.pallas import tpu as pltpu

F32 = jnp.float32
BF16 = jnp.bfloat16

D_MODEL = 1024
C_CONV = 384
C_POOL = 256
C_ATT = 384
HEAD_DIM = 64
N_HEADS = C_ATT // HEAD_DIM
CONV_W = 31
CONV_HALO = CONV_W - 1
POOL_WINDOWS = (2, 4, 8, 16)
POOL_GC = 64
POOL_BUF = 15
DILATIONS = (1, 4, 16)
WINDOW_KEYS = 128
EPS = 1e-6
ROPE_THETA = 10000.0
D_IN = 3 * C_CONV + 2 * C_POOL + 4 * C_ATT
PAST_LEN = 16384
NEG = -1e30
LOG2_E = 1.4426950408889634

OFF_A_VAL = 0
OFF_A_GLU = OFF_A_VAL + C_CONV
OFF_A_GATE = OFF_A_GLU + C_CONV
OFF_B_VAL = OFF_A_GATE + C_CONV
OFF_B_GATE = OFF_B_VAL + C_POOL
OFF_Q = OFF_B_GATE + C_POOL
OFF_K = OFF_Q + C_ATT
OFF_V = OFF_K + C_ATT
OFF_C_GATE = OFF_V + C_ATT

LANES = 128
N_CONV_SLAB = C_CONV // LANES
N_POOL_SLAB = C_POOL // LANES
N_ATT_SLAB = C_ATT // LANES
VMEM_LIMIT_BYTES = 60 * 1024 * 1024

TQ = 256
RC = 32
CONV_GROUP = 8
U_HALO = 32
B_HALO = 16
QB = WINDOW_KEYS
UNROLL_NEAR = 15
UNROLL_MID = 12
UNROLL_FAR = 16

AB_Q, AB_K, AB_V, AB_O, AB_LSE = range(5)
AB_COUNT = 5


def _sigmoid(x):
    return 0.5 * jnp.tanh(0.5 * x) + 0.5


def _silu(x):
    h = 0.5 * x
    return h * jnp.tanh(h) + h


def _rope_tables(positions):
    half = HEAD_DIM // 2
    inv = np.float32(ROPE_THETA) ** (-np.arange(half, dtype=np.float32) / np.float32(half))
    ang = np.asarray(positions, np.float32)[:, None] * inv[None, :]
    cos = np.cos(ang)
    sin = np.sin(ang)
    cos_h = np.concatenate([cos, cos], axis=-1)
    sin_h = np.concatenate([-sin, sin], axis=-1)
    reps = LANES // HEAD_DIM
    return jnp.asarray(np.tile(cos_h, (1, reps))), jnp.asarray(np.tile(sin_h, (1, reps)))


def _head_mean_matrix():
    idx = np.arange(C_ATT) // HEAD_DIM
    return jnp.asarray((idx[:, None] == idx[None, :]).astype(np.float32) / HEAD_DIM, dtype=BF16)


def _pool_block_diag(pool_w):
    n_grp = len(POOL_WINDOWS)
    same_group = np.eye(n_grp, dtype=np.float32)[None, :, None, :, None]
    return (pool_w[:, :, :, None, :] * same_group).reshape(pool_w.shape[0], C_POOL, C_POOL)


VEC_NORM_G = 0
VEC_CONV_B = VEC_NORM_G + D_MODEL
VEC_LN_G = VEC_CONV_B + C_CONV
VEC_LN_B = VEC_LN_G + C_CONV
VEC_POOL_SCALE = VEC_LN_B + C_CONV
VEC_Q_GAIN = VEC_POOL_SCALE + C_POOL
VEC_K_GAIN = VEC_Q_GAIN + C_ATT
VEC_LEN = VEC_K_GAIN + C_ATT


def _pack_vectors(norm_g, conv_b, ln_g, ln_b, pool_scale, q_norm_g, k_norm_g):
    parts = [norm_g, conv_b, ln_g, ln_b, pool_scale, jnp.tile(q_norm_g, (1, N_HEADS)), jnp.tile(k_norm_g, (1, N_HEADS))]
    row, off = 0.0, 0
    for p in parts:
        row = row + jnp.pad(p, ((0, 0), (off, VEC_LEN - off - p.shape[1])))
        off += p.shape[1]
    return row[:, None]


def _vector_views(vec_ref):
    bounds = (VEC_NORM_G, VEC_CONV_B, VEC_LN_G, VEC_LN_B, VEC_POOL_SCALE, VEC_Q_GAIN, VEC_K_GAIN, VEC_LEN)
    return tuple(vec_ref.at[:, lo:hi] for lo, hi in zip(bounds[:-1], bounds[1:]))


def _swap_halves(x):
    lane = lax.broadcasted_iota(jnp.int32, x.shape, 1)
    first_half = (lane % HEAD_DIM) < (HEAD_DIM // 2)
    return jnp.where(first_half, pltpu.roll(x, LANES - HEAD_DIM // 2, 1), pltpu.roll(x, HEAD_DIM // 2, 1))


def _pool_means(loads, pos):
    lane = lax.broadcasted_iota(jnp.int32, loads(0, 0).shape, 1)
    lo = lane < POOL_GC
    posf = (pos + 1).astype(F32)
    outs = []
    for slab in range(N_POOL_SLAB):
        w_lo, w_hi = POOL_WINDOWS[2 * slab], POOL_WINDOWS[2 * slab + 1]
        cur = loads(0, slab)
        s = cur
        for i in range(1, w_lo):
            s = s + loads(i, slab)
        s_lo = s
        for i in range(w_lo, w_hi):
            s = s + loads(i, slab)
        s_hi = s
        cnt_lo = jnp.minimum(posf, float(w_lo))
        cnt_hi = jnp.minimum(posf, float(w_hi))
        pooled = jnp.where(lo, s_lo / cnt_lo, s_hi / cnt_hi)
        outs.append(pooled - cur)
    return outs


N_LAYER_INPUTS = 18


def _layer_kernel(x_ref, x2_ref, vec_ref, win_ref, cw_ref, pw_ref, cos_ref, sin_ref, mavg_ref, wout_ref,
                  xs_ref, qr_s, kr_s, v_s, mixab_s, cgs_s, kt_ref, vt_ref,
                  y_ref, ko_ref, vo_ref, cst_ref, pst_ref,
                  ys_ref, okt_ref, ovt_ref,
                  h_s, proj, u_buf, b_buf,
                  st_q, st_k, st_v, st_att, ab0, ab1, ab2, ab3, ab4, mix,
                  att_s, qcol_s, sm_s, sl_s, so_s, kcar_s, vcar_s,
                  *, seq_len, win_len):
    ab = (ab0, ab1, ab2, ab3, ab4)
    ng_ref, cb_ref, lng_ref, lnb_ref, psc_ref, qg_ref, kg_ref = _vector_views(vec_ref)
    nt = seq_len // TQ
    b = pl.program_id(0)
    t = pl.program_id(1)
    n_batch = pl.num_programs(0) - 1
    blocks_per_sample = win_len // kt_ref.shape[1]
    row0 = pl.multiple_of(t * TQ, TQ)
    sq_s = h_s.at[:, 0:2 * C_ATT]
    d_s = h_s.at[:, 2 * C_ATT:2 * C_ATT + C_POOL]
    msq_s = proj.at[:, OFF_A_VAL:OFF_A_VAL + 2 * C_ATT]
    yb_s = proj.at[:, OFF_B_VAL:OFF_B_VAL + C_POOL]
    conv_s = st_att

    step = b * nt + t
    part = step % blocks_per_sample
    smp = step // blocks_per_sample

    def sample_start():
        _sample_start(part == 0, smp, qr_s, kr_s, v_s, qcol_s, sm_s, sl_s, so_s, kcar_s, vcar_s)

    def sample_window():
        _sample_window((blocks_per_sample - 1 - part) * kt_ref.shape[1], kt_ref, vt_ref, okt_ref, ovt_ref,
                       qcol_s, sm_s, sl_s, so_s, kcar_s, vcar_s, win_len=win_len)

    def sample_done():
        _sample_done(part == blocks_per_sample - 1, smp, att_s, sl_s, so_s)

    def phase2():
        d_mid = DILATIONS[1]
        per_stream = TQ // d_mid
        for s in range(N_ATT_SLAB):
            for r in range(d_mid):
                src_rows = pl.ds(pl.multiple_of(r * (seq_len // d_mid) + t * per_stream, per_stream), per_stream)
                st_att[s, pl.ds(r, per_stream, stride=d_mid), :] = ab[AB_O][s, src_rows, :]

        def att_chunk(i, c):
            r = pl.multiple_of(i * RC, RC)
            grow = pl.ds(pl.multiple_of(row0 + r, RC), RC)
            for s in range(N_ATT_SLAB):
                mcols = slice(C_CONV + C_POOL + s * LANES, C_CONV + C_POOL + (s + 1) * LANES)
                yc = st_att[s, pl.ds(r, RC), :] * mix[grow, mcols].astype(F32)
                mix[grow, mcols] = yc.astype(BF16)
            return c

        lax.fori_loop(0, TQ // RC, att_chunk, 0, unroll=True)
        y_ref[...] = x2_ref[...] + jnp.dot(mix[pl.ds(row0, TQ), :], wout_ref[...], preferred_element_type=F32)

    def phase1(with_phase2):
        @pl.when(t == 0)
        def _zero_halo():
            u_buf[:, 0:U_HALO, :] = jnp.zeros((N_CONV_SLAB, U_HALO, LANES), F32)
            b_buf[:, 0:B_HALO, :] = jnp.zeros((N_POOL_SLAB, B_HALO, LANES), F32)
            if not with_phase2:
                att_s[...] = jnp.zeros(att_s.shape, F32)

        sample_start()
        if with_phase2:
            phase2()
        sample_window()

        def norm_chunk(i, c):
            r = pl.multiple_of(i * RC, RC)
            x = x_ref[pl.ds(r, RC), :]
            ms = jnp.mean(x * x, axis=-1, keepdims=True)
            h_s[pl.ds(r, RC), :] = (x * lax.rsqrt(ms + EPS) * ng_ref[...]).astype(BF16)
            return c

        lax.fori_loop(0, TQ // RC, norm_chunk, 0, unroll=True)
        proj[...] = jnp.dot(h_s[...], win_ref[...], preferred_element_type=F32)

        def split_chunk(i, c):
            r = pl.multiple_of(i * RC, RC)
            rows = pl.ds(r, RC)
            grow = pl.ds(pl.multiple_of(row0 + r, RC), RC)
            for s in range(N_CONV_SLAB):
                cols = slice(s * LANES, (s + 1) * LANES)
                a_val = proj[rows, OFF_A_VAL + s * LANES:OFF_A_VAL + (s + 1) * LANES]
                a_glu = proj[rows, OFF_A_GLU + s * LANES:OFF_A_GLU + (s + 1) * LANES]
                u_buf[s, pl.ds(U_HALO + r, RC), :] = a_val * _sigmoid(a_glu)
                q = proj[rows, OFF_Q + s * LANES:OFF_Q + (s + 1) * LANES]
                k = proj[rows, OFF_K + s * LANES:OFF_K + (s + 1) * LANES]
                sq_s[rows, cols] = (q * q).astype(BF16)
                sq_s[rows, C_ATT + s * LANES:C_ATT + (s + 1) * LANES] = (k * k).astype(BF16)
                st_v[s, rows, :] = proj[rows, OFF_V + s * LANES:OFF_V + (s + 1) * LANES]
                mix[grow, C_CONV + C_POOL + s * LANES:C_CONV + C_POOL + (s + 1) * LANES] = _silu(
                    proj[rows, OFF_C_GATE + s * LANES:OFF_C_GATE + (s + 1) * LANES]).astype(BF16)
            for s in range(N_POOL_SLAB):
                b_buf[s, pl.ds(B_HALO + r, RC), :] = proj[rows, OFF_B_VAL + s * LANES:OFF_B_VAL + (s + 1) * LANES]
            pos = row0 + r + lax.broadcasted_iota(jnp.int32, (RC, 1), 0)
            dl = _pool_means(lambda sh, s: b_buf[s, pl.ds(r + B_HALO - sh, RC), :], pos)
            for s in range(N_POOL_SLAB):
                d_s[rows, s * LANES:(s + 1) * LANES] = dl[s].astype(BF16)
            return c

        lax.fori_loop(0, TQ // RC, split_chunk, 0, unroll=True)
        msq_s[:, 0:C_ATT] = jnp.dot(sq_s[:, 0:C_ATT], mavg_ref[...], preferred_element_type=F32)
        msq_s[:, C_ATT:2 * C_ATT] = jnp.dot(sq_s[:, C_ATT:2 * C_ATT], mavg_ref[...], preferred_element_type=F32)
        yb_s[...] = jnp.dot(d_s[...], pw_ref[...], preferred_element_type=F32)

        row_groups = TQ // CONV_GROUP
        for s in range(N_CONV_SLAB):
            cols = slice(s * LANES, (s + 1) * LANES)
            acc = [jnp.broadcast_to(cb_ref[:, cols], (CONV_GROUP, LANES))] * row_groups
            for a in range(CONV_GROUP):
                taps = list(range(a, CONV_W, CONV_GROUP))
                cws = [jnp.broadcast_to(cw_ref[w:w + 1, cols], (CONV_GROUP, LANES)) for w in taps]
                for j in range(row_groups + len(taps) - 1):
                    start = (U_HALO - CONV_HALO) + a + CONV_GROUP * j
                    window = u_buf[s, start:start + CONV_GROUP, :]
                    for m in range(len(taps)):
                        if 0 <= j - m < row_groups:
                            acc[j - m] = acc[j - m] + window * cws[m]
            for g in range(row_groups):
                conv_s[s, g * CONV_GROUP:(g + 1) * CONV_GROUP, :] = acc[g]

        def mixer_chunk(i, c):
            r = pl.multiple_of(i * RC, RC)
            rows = pl.ds(r, RC)
            grow = pl.ds(pl.multiple_of(row0 + r, RC), RC)
            conv = [conv_s[s, rows, :] for s in range(N_CONV_SLAB)]
            mu = jnp.sum(conv[0] + conv[1] + conv[2], axis=-1, keepdims=True) * (1.0 / C_CONV)
            cen = [cv - mu for cv in conv]
            var = jnp.sum(cen[0] * cen[0] + cen[1] * cen[1] + cen[2] * cen[2], axis=-1, keepdims=True) * (1.0 / C_CONV)
            rstd = lax.rsqrt(var + EPS)
            for s in range(N_CONV_SLAB):
                cols = slice(s * LANES, (s + 1) * LANES)
                ln = cen[s] * rstd * lng_ref[:, cols] + lnb_ref[:, cols]
                gate = proj[rows, OFF_A_GATE + s * LANES:OFF_A_GATE + (s + 1) * LANES]
                mix[grow, cols] = (_silu(ln) * _silu(gate)).astype(BF16)
            for s in range(N_POOL_SLAB):
                cols = slice(s * LANES, (s + 1) * LANES)
                gate = proj[rows, OFF_B_GATE + s * LANES:OFF_B_GATE + (s + 1) * LANES]
                yb = yb_s[rows, cols] * psc_ref[:, cols] * _silu(gate)
                mix[grow, C_CONV + s * LANES:C_CONV + (s + 1) * LANES] = yb.astype(BF16)
            cos = cos_ref[grow, :]
            sin = sin_ref[grow, :]
            for s in range(N_ATT_SLAB):
                cols = slice(s * LANES, (s + 1) * LANES)
                q = proj[rows, OFF_Q + s * LANES:OFF_Q + (s + 1) * LANES]
                qn = q * lax.rsqrt(msq_s[rows, cols] + EPS) * qg_ref[:, cols]
                qr = qn * cos + _swap_halves(qn) * sin
                st_q[s, rows, :] = qr * (HEAD_DIM ** -0.5 * LOG2_E)
                k = proj[rows, OFF_K + s * LANES:OFF_K + (s + 1) * LANES]
                kn = k * lax.rsqrt(msq_s[rows, C_ATT + s * LANES:C_ATT + (s + 1) * LANES] + EPS) * kg_ref[:, cols]
                st_k[s, rows, :] = kn * cos + _swap_halves(kn) * sin
            return c

        lax.fori_loop(0, TQ // RC, mixer_chunk, 0, unroll=True)

        d_mid = DILATIONS[1]
        per_stream = TQ // d_mid
        for s in range(N_ATT_SLAB):
            ko_ref[s * LANES:(s + 1) * LANES, :] = st_k[s].T
            vo_ref[s * LANES:(s + 1) * LANES, :] = st_v[s].T
            for st, dst in ((st_q, AB_Q), (st_k, AB_K), (st_v, AB_V)):
                for r in range(d_mid):
                    dst_rows = pl.ds(pl.multiple_of(r * (seq_len // d_mid) + t * per_stream, per_stream), per_stream)
                    ab[dst][s, dst_rows, :] = st[s, pl.ds(r, per_stream, stride=d_mid), :]

        @pl.when(t == nt - 1)
        def _write_state():
            for s in range(N_CONV_SLAB):
                cst_ref[:, s * LANES:(s + 1) * LANES] = u_buf[s, TQ + U_HALO - CONV_HALO:TQ + U_HALO, :]
            for s in range(N_POOL_SLAB):
                pst_ref[:, s * LANES:(s + 1) * LANES] = b_buf[s, TQ + B_HALO - POOL_BUF:TQ + B_HALO, :]

        u_buf[:, 0:U_HALO, :] = u_buf[:, TQ:TQ + U_HALO, :]
        b_buf[:, 0:B_HALO, :] = b_buf[:, TQ:TQ + B_HALO, :]
        sample_done()

    def attention():
        d_mid, d_far = DILATIONS[1], DILATIONS[2]
        ratio = d_far // d_mid
        stream_len = seq_len // d_mid
        piece = QB // d_mid
        lane = lax.broadcasted_iota(jnp.int32, (QB, LANES), 1)
        rowi = lax.broadcasted_iota(jnp.int32, (QB, LANES), 0)
        lo = lane < HEAD_DIM

        def masks(key_pos, query_pos):
            cur = key_pos <= query_pos
            prev = key_pos >= query_pos
            cur2 = jnp.concatenate([cur, cur], axis=0)
            return cur2, jnp.concatenate([jnp.concatenate([prev, prev], axis=0), cur2], axis=1)

        cur_ok2, prev_cur_ok2 = masks(lane, rowi)
        ncur_ok2, nprev_cur_ok2 = masks(d_mid * (lane % piece) + lane // piece, d_mid * (rowi % piece) + rowi // piece)

        def attend(q, keys, vals, mask):
            qa = jnp.where(lo, q, 0.0).astype(BF16)
            qb = jnp.where(lo, 0.0, q).astype(BF16)
            q2 = jnp.concatenate([qa, qb], axis=0)
            sc = lax.dot_general(q2, keys.astype(BF16), (((1,), (1,)), ((), ())), preferred_element_type=F32)
            sc = jnp.where(mask, sc, NEG)
            m = jnp.max(sc, axis=-1, keepdims=True)
            p = jnp.exp2(sc - m).astype(BF16)
            v1 = jnp.concatenate([vals.astype(BF16), jnp.ones(vals.shape, BF16)], axis=1)
            ol = jnp.dot(p, v1, preferred_element_type=F32)
            o_u = jnp.where(lo, ol[0:QB, 0:LANES], ol[QB:2 * QB, 0:LANES])
            l_u = jnp.where(lo, ol[0:QB, LANES:2 * LANES], ol[QB:2 * QB, LANES:2 * LANES])
            m_u = jnp.where(lo, m[0:QB], m[QB:2 * QB])
            return o_u, m_u, l_u

        def load(ref, s, pieces):
            tiles = [ref[s, p, :] for p in pieces]
            return tiles[0] if len(tiles) == 1 else jnp.concatenate(tiles, axis=0)

        def save(ref, s, pieces, val):
            n = val.shape[0] // len(pieces)
            for j, p in enumerate(pieces):
                ref[s, p, :] = val[j * n:(j + 1) * n]

        def block(rows, krows, mask, merge, with_lse=True):
            outs = []
            for s in range(N_ATT_SLAB):
                o_u, m_u, l_u = attend(load(ab[AB_Q], s, rows), load(ab[AB_K], s, krows), load(ab[AB_V], s, krows),
                                       mask)
                if merge:
                    lse_old = load(ab[AB_LSE], s, rows)
                    m_new = jnp.maximum(lse_old, m_u)
                    w_old = jnp.exp2(lse_old - m_new)
                    w_u = jnp.exp2(m_u - m_new)
                    den = w_old + l_u * w_u
                    o_n = (load(ab[AB_O], s, rows) * w_old + o_u * w_u) * (1.0 / den)
                else:
                    m_new, den = m_u, l_u
                    o_n = o_u * (1.0 / l_u)
                outs.append((o_n, m_new + jnp.log2(den) if with_lse else None))
            return outs

        def store(rows, outs):
            for s, (o_n, lse_n) in enumerate(outs):
                save(ab[AB_O], s, rows, o_n)
                if lse_n is not None:
                    save(ab[AB_LSE], s, rows, lse_n)

        def token_block(base):
            return [pl.ds(r * stream_len + base, piece) for r in range(d_mid)]

        store(token_block(0), block(token_block(0), token_block(0), ncur_ok2, merge=False))

        def near_unit(i, c):
            base = pl.multiple_of(i * piece, piece)
            rows = token_block(base)
            store(rows, block(rows, token_block(base - piece) + rows, nprev_cur_ok2, merge=False))
            return c

        lax.fori_loop(1, seq_len // QB, near_unit, 0, unroll=UNROLL_NEAR)

        def mid_first(ph, c):
            rows = [pl.ds(pl.multiple_of(ph * stream_len, QB), QB)]
            store(rows, block(rows, rows, cur_ok2, merge=True))
            return c

        lax.fori_loop(0, d_mid, mid_first, 0, unroll=True)
        later_blocks = stream_len // QB - 1

        def mid_unit(i, c):
            start = pl.multiple_of((i // later_blocks) * stream_len + (i % later_blocks + 1) * QB, QB)
            rows = [pl.ds(start, QB)]
            store(rows, block(rows, [pl.ds(start - QB, 2 * QB)], prev_cur_ok2, merge=True))
            return c

        lax.fori_loop(0, d_mid * later_blocks, mid_unit, 0, unroll=UNROLL_MID)

        def far_unit(r, c):
            rows = [pl.ds((r % d_mid) * stream_len + r // d_mid, QB, stride=ratio)]
            store(rows, block(rows, rows, cur_ok2, merge=True, with_lse=False))
            return c

        lax.fori_loop(0, d_far, far_unit, 0, unroll=UNROLL_FAR)

    @pl.when(b == 0)
    def _first_row():
        phase1(with_phase2=False)

    @pl.when(jnp.logical_and(b > 0, b < n_batch))
    def _steady_rows():
        phase1(with_phase2=True)

    @pl.when(b == n_batch)
    def _last_row():
        phase2()
        _sample_finish(t == nt - 1, xs_ref, wout_ref, ys_ref, att_s, mixab_s, cgs_s)

    pl.when(jnp.logical_and(t == nt - 1, b < n_batch))(attention)


def _layer_kernel_with_alias(*refs, n_alias, **kw):
    _layer_kernel(*refs[:N_LAYER_INPUTS], *refs[N_LAYER_INPUTS + n_alias:], **kw)


def _trunk_layer(layer, x, xs, decode_new, kt, vt, vecs, w_in, conv_w, pool_wbd, cos_t, sin_t, mavg, w_out,
                 kv_prev=None):
    bsz, seq_len, _ = x.shape
    nsmp = xs.shape[0]
    depth, _, _, win_len = kt.shape
    assert seq_len % TQ == 0 and seq_len == QB * DILATIONS[-1] and DILATIONS[0] == 1
    assert win_len == WINDOW_KEYS * DILATIONS[-1]
    nt = seq_len // TQ
    blocks_per_sample, rem = divmod(bsz * nt, nsmp)
    assert rem == 0 and blocks_per_sample >= 1 and win_len % (blocks_per_sample * LANES) == 0
    win_blk = win_len // blocks_per_sample

    def p1(b, t):
        return jnp.minimum(b, bsz - 1), jnp.where(b == bsz, nt - 1, t)

    def p2(b, t):
        return jnp.maximum(b - 1, 0), jnp.where(b == 0, 0, t)

    def sample_block(b, t):
        pb, pt = p1(b, t)
        step = pb * nt + pt
        return (layer, step // blocks_per_sample, 0, blocks_per_sample - 1 - step % blocks_per_sample)

    def const(shape):
        nd = len(shape)
        return pl.BlockSpec(shape, lambda b, t: (0,) * nd)

    def resident(shape):
        nd = len(shape)
        return pl.BlockSpec(shape, lambda b, t: (0,) * nd, pipeline_mode=pl.Buffered(1))

    def per_layer(shape, **kw):
        nd = len(shape)
        return pl.BlockSpec((None,) + shape, lambda b, t: (layer,) + (0,) * nd, **kw)

    in_specs = [
        pl.BlockSpec((None, TQ, D_MODEL), lambda b, t: (*p1(b, t), 0)),
        pl.BlockSpec((None, TQ, D_MODEL), lambda b, t: (*p2(b, t), 0)),
        per_layer((1, VEC_LEN)),
        per_layer((D_MODEL, D_IN), pipeline_mode=pl.Buffered(1)),
        per_layer((CONV_W, C_CONV)),
        per_layer((C_POOL, C_POOL)),
        resident((seq_len, LANES)),
        resident((seq_len, LANES)),
        const((C_ATT, C_ATT)),
        per_layer((D_MODEL, D_MODEL), pipeline_mode=pl.Buffered(1)),
        const((nsmp, D_MODEL)),
        const((nsmp, C_ATT)),
        const((nsmp, C_ATT)),
        const((nsmp, C_ATT)),
        const((nsmp, C_CONV + C_POOL)),
        const((nsmp, C_ATT)),
        pl.BlockSpec((None, None, C_ATT, win_blk), sample_block),
        pl.BlockSpec((None, None, C_ATT, win_blk), sample_block),
    ]
    operands = [x, x, vecs, w_in, conv_w, pool_wbd, cos_t, sin_t, mavg, w_out, xs, *decode_new, kt, vt]
    assert len(operands) == N_LAYER_INPUTS
    kv_spec = pl.BlockSpec((None, None, C_ATT, TQ), lambda b, t: (layer, p1(b, t)[0], 0, p1(b, t)[1]))
    out_specs = [
        pl.BlockSpec((None, TQ, D_MODEL), lambda b, t: (*p2(b, t), 0)),
        kv_spec,
        kv_spec,
        pl.BlockSpec((None, CONV_HALO, C_CONV), lambda b, t: (p1(b, t)[0], 0, 0)),
        pl.BlockSpec((None, POOL_BUF, C_POOL), lambda b, t: (p1(b, t)[0], 0, 0)),
        const((nsmp, D_MODEL)),
        pl.BlockSpec((None, None, C_ATT, win_blk), sample_block),
        pl.BlockSpec((None, None, C_ATT, win_blk), sample_block),
    ]
    out_shape = [
        jax.ShapeDtypeStruct((bsz, seq_len, D_MODEL), F32),
        jax.ShapeDtypeStruct((depth, bsz, C_ATT, seq_len), F32),
        jax.ShapeDtypeStruct((depth, bsz, C_ATT, seq_len), F32),
        jax.ShapeDtypeStruct((bsz, CONV_HALO, C_CONV), F32),
        jax.ShapeDtypeStruct((bsz, POOL_BUF, C_POOL), F32),
        jax.ShapeDtypeStruct((nsmp, D_MODEL), F32),
        jax.ShapeDtypeStruct((depth, nsmp, C_ATT, win_len), F32),
        jax.ShapeDtypeStruct((depth, nsmp, C_ATT, win_len), F32),
    ]
    static = dict(seq_len=seq_len, win_len=win_len)
    if kv_prev is None:
        kern = functools.partial(_layer_kernel, **static)
        aliases = {}
    else:
        kern = functools.partial(_layer_kernel_with_alias, n_alias=len(kv_prev), **static)
        in_specs += [pl.BlockSpec(memory_space=pl.ANY)] * len(kv_prev)
        operands += list(kv_prev)
        aliases = {N_LAYER_INPUTS + i: o for i, o in enumerate((1, 2, 6, 7))}
    scratch = [
        pltpu.VMEM((TQ, D_MODEL), BF16),
        pltpu.VMEM((TQ, D_IN), F32),
        pltpu.VMEM((N_CONV_SLAB, TQ + U_HALO, LANES), F32),
        pltpu.VMEM((N_POOL_SLAB, TQ + B_HALO, LANES), F32),
        *[pltpu.VMEM((N_ATT_SLAB, TQ, LANES), F32) for _ in range(4)],
        *[pltpu.VMEM((N_ATT_SLAB, seq_len, LANES), F32) for _ in range(AB_COUNT)],
        pltpu.VMEM((seq_len, D_MODEL), BF16),
        pltpu.VMEM((nsmp, C_ATT), F32),
        pltpu.VMEM((C_ATT, LANES), F32),
        pltpu.VMEM((N_HEADS, LANES), F32),
        pltpu.VMEM((N_HEADS, LANES), F32),
        pltpu.VMEM((C_ATT, LANES), F32),
        pltpu.VMEM((C_ATT, LANES), F32),
        pltpu.VMEM((C_ATT, LANES), F32),
    ]
    return pl.pallas_call(
        kern,
        out_shape=out_shape,
        grid=(bsz + 1, nt),
        in_specs=in_specs,
        out_specs=out_specs,
        scratch_shapes=scratch,
        input_output_aliases=aliases,
        compiler_params=pltpu.CompilerParams(
            dimension_semantics=("arbitrary", "arbitrary"),
            vmem_limit_bytes=VMEM_LIMIT_BYTES,
        ),
        name="trunk_layer",
    )(*operands)


def _row_to_col_tile(row):
    return jnp.broadcast_to(row, (LANES, row.shape[1])).T


def _key_multiplicity(lane0, blk, win_len):
    t = lane0 + lax.broadcasted_iota(jnp.int32, (1, blk), 1)
    delta = win_len - t
    cnt = jnp.zeros((1, blk), F32)
    for d in DILATIONS:
        hit = jnp.logical_and(delta % d == 0, delta <= d * WINDOW_KEYS)
        cnt = cnt + hit.astype(F32)
    return cnt


def _decode_prepare_kernel(x_ref, sc_ref, sp_ref, vec_ref, win_ref, cw_ref, pw_ref, cos_ref, sin_ref, mavg_ref,
                           nc_ref, np_ref, qr_s, kr_s, v_s, mixab_s, cg_s, *, pos):
    nsmp = x_ref.shape[0]
    ng_ref, cb_ref, lng_ref, lnb_ref, psc_ref, qg_ref, kg_ref = _vector_views(vec_ref)

    x = x_ref[...]
    ms = jnp.mean(x * x, axis=-1, keepdims=True)
    h = (x * lax.rsqrt(ms + EPS) * ng_ref[...]).astype(BF16)
    proj = jnp.dot(h, win_ref[...], preferred_element_type=F32)

    u = proj[:, OFF_A_VAL:OFF_A_VAL + C_CONV] * _sigmoid(proj[:, OFF_A_GLU:OFF_A_GLU + C_CONV])
    conv = u * cw_ref[CONV_HALO:CONV_W, :] + cb_ref[...]
    for w in range(CONV_HALO):
        conv = conv + sc_ref[w] * cw_ref[w:w + 1, :]
    nc_ref[0:CONV_HALO - 1] = sc_ref[1:CONV_HALO]
    nc_ref[CONV_HALO - 1] = u
    mu = jnp.mean(conv, axis=-1, keepdims=True)
    cen = conv - mu
    var = jnp.mean(cen * cen, axis=-1, keepdims=True)
    ln = cen * lax.rsqrt(var + EPS) * lng_ref[...] + lnb_ref[...]
    ya = _silu(ln) * _silu(proj[:, OFF_A_GATE:OFF_A_GATE + C_CONV])

    bval = proj[:, OFF_B_VAL:OFF_B_VAL + C_POOL]
    lane_p = lax.broadcasted_iota(jnp.int32, (nsmp, C_POOL), 1)
    pooled = jnp.zeros((nsmp, C_POOL), F32)
    acc = bval
    done = 1
    for wi, w in enumerate(POOL_WINDOWS):
        for i in range(done, w):
            acc = acc + sp_ref[POOL_BUF - i]
        done = w
        pooled = jnp.where(lane_p // POOL_GC == wi, acc / float(min(pos + 1, w)), pooled)
    np_ref[0:POOL_BUF - 1] = sp_ref[1:POOL_BUF]
    np_ref[POOL_BUF - 1] = bval
    dpool = (pooled - bval).astype(BF16)
    yb = (jnp.dot(dpool, pw_ref[...], preferred_element_type=F32) * psc_ref[...]
          * _silu(proj[:, OFF_B_GATE:OFF_B_GATE + C_POOL]))
    mixab_s[:, 0:C_CONV] = ya
    mixab_s[:, C_CONV:C_CONV + C_POOL] = yb
    cg_s[...] = _silu(proj[:, OFF_C_GATE:OFF_C_GATE + C_ATT])

    q = proj[:, OFF_Q:OFF_Q + C_ATT]
    k = proj[:, OFF_K:OFF_K + C_ATT]
    qn = q * lax.rsqrt(jnp.dot((q * q).astype(BF16), mavg_ref[...], preferred_element_type=F32) + EPS) * qg_ref[...]
    kn = k * lax.rsqrt(jnp.dot((k * k).astype(BF16), mavg_ref[...], preferred_element_type=F32) + EPS) * kg_ref[...]
    for s in range(N_ATT_SLAB):
        cols = slice(s * LANES, (s + 1) * LANES)
        qs, ks = qn[:, cols], kn[:, cols]
        qr_s[:, cols] = (qs * cos_ref[...] + _swap_halves(qs) * sin_ref[...]) * (HEAD_DIM ** -0.5)
        kr_s[:, cols] = ks * cos_ref[...] + _swap_halves(ks) * sin_ref[...]
    v_s[...] = proj[:, OFF_V:OFF_V + C_ATT]


def _decode_prepare(layer, pos, xs, state_conv, state_pool, vecs, w_in, conv_w, pool_wbd, cos_s, sin_s, mavg):
    nsmp = xs.shape[0]

    def const(shape):
        nd = len(shape)
        return pl.BlockSpec(shape, lambda i: (0,) * nd)

    def per_layer(shape):
        nd = len(shape)
        return pl.BlockSpec((None,) + shape, lambda i: (layer,) + (0,) * nd)

    in_specs = [
        const((nsmp, D_MODEL)), per_layer((CONV_HALO, nsmp, C_CONV)), per_layer((POOL_BUF, nsmp, C_POOL)),
        per_layer((1, VEC_LEN)), per_layer((D_MODEL, D_IN)), per_layer((CONV_W, C_CONV)), per_layer((C_POOL, C_POOL)),
        const((1, LANES)), const((1, LANES)), const((C_ATT, C_ATT)),
    ]
    out_dims = [(CONV_HALO, nsmp, C_CONV), (POOL_BUF, nsmp, C_POOL), (nsmp, C_ATT), (nsmp, C_ATT), (nsmp, C_ATT),
                (nsmp, C_CONV + C_POOL), (nsmp, C_ATT)]
    return pl.pallas_call(
        functools.partial(_decode_prepare_kernel, pos=pos),
        out_shape=[jax.ShapeDtypeStruct(d, F32) for d in out_dims],
        grid=(1,),
        in_specs=in_specs,
        out_specs=[const(d) for d in out_dims],
        compiler_params=pltpu.CompilerParams(dimension_semantics=("arbitrary",)),
        name="decode_prepare",
    )(xs, state_conv, state_pool, vecs, w_in, conv_w, pool_wbd, cos_s, sin_s, mavg)


def _sample_start(cond, smp, qr_s, kr_s, v_s, qcol_s, sm_s, sl_s, so_s, kcar_s, vcar_s):
    nsmp = qr_s.shape[0]
    n_pat = float(len(DILATIONS))

    @pl.when(cond)
    def _start():
        mine = lax.broadcasted_iota(jnp.int32, (nsmp, C_ATT), 0) == smp

        def col_tile(ref):
            return _row_to_col_tile(jnp.sum(jnp.where(mine, ref[...], 0.0), axis=0, keepdims=True))

        q_col = col_tile(qr_s)
        k_col = col_tile(kr_s)
        v_col = col_tile(v_s)
        qcol_s[...] = q_col
        kcar_s[...] = k_col
        vcar_s[...] = v_col
        so_s[...] = v_col * n_pat
        for hd in range(N_HEADS):
            hr = slice(hd * HEAD_DIM, (hd + 1) * HEAD_DIM)
            sm_s[hd:hd + 1, :] = jnp.sum(k_col[hr, :] * q_col[hr, :], axis=0, keepdims=True)
        sl_s[...] = jnp.full(sl_s.shape, n_pat, F32)


def _sample_window(lane0, kt_ref, vt_ref, okt_ref, ovt_ref, qcol_s, sm_s, sl_s, so_s, kcar_s, vcar_s, *, win_len):
    blk = kt_ref.shape[1]
    n_tiles = blk // LANES
    cnt = _key_multiplicity(lane0, blk, win_len)
    reach = cnt > 0.0
    last_lane = lax.broadcasted_iota(jnp.int32, (HEAD_DIM, blk), 1) == blk - 1

    for hd in range(N_HEADS):
        hr = slice(hd * HEAD_DIM, (hd + 1) * HEAD_DIM)
        kt = kt_ref[hr, :]
        vt = vt_ref[hr, :]
        s_win = jnp.sum(kt * jnp.concatenate([qcol_s[hr, :]] * n_tiles, axis=1), axis=0, keepdims=True)
        s_win = jnp.where(reach, s_win, NEG)
        m_old = sm_s[hd:hd + 1, :]
        m_new = jnp.maximum(m_old, jnp.max(s_win, axis=-1, keepdims=True))
        w_old = jnp.exp(m_old - m_new)
        p_win = cnt * jnp.exp(s_win - m_new[:, 0:1])
        sm_s[hd:hd + 1, :] = m_new
        sl_s[hd:hd + 1, :] = sl_s[hd:hd + 1, :] * w_old + jnp.sum(p_win, axis=-1, keepdims=True)
        so_s[hr, :] = so_s[hr, :] * w_old + jnp.sum(vt * p_win, axis=-1, keepdims=True)
        k_next = jnp.concatenate([kcar_s[hr, :]] * n_tiles, axis=1)
        v_next = jnp.concatenate([vcar_s[hr, :]] * n_tiles, axis=1)
        okt_ref[hr, :] = jnp.where(last_lane, k_next, pltpu.roll(kt, blk - 1, 1))
        ovt_ref[hr, :] = jnp.where(last_lane, v_next, pltpu.roll(vt, blk - 1, 1))
        kcar_s[hr, :] = jnp.broadcast_to(kt[:, 0:1], (HEAD_DIM, LANES))
        vcar_s[hr, :] = jnp.broadcast_to(vt[:, 0:1], (HEAD_DIM, LANES))


def _sample_done(cond, smp, att_s, sl_s, so_s):
    nsmp = att_s.shape[0]

    @pl.when(cond)
    def _done():
        cols = [so_s[hd * HEAD_DIM:(hd + 1) * HEAD_DIM, :] / sl_s[hd:hd + 1, :] for hd in range(N_HEADS)]
        att_row = jnp.concatenate(cols, axis=0).T[0:1, :]
        mine = lax.broadcasted_iota(jnp.int32, (nsmp, C_ATT), 0) == smp
        att_s[...] = jnp.where(mine, att_row, att_s[...])


def _sample_finish(cond, x_ref, wout_ref, y_ref, att_s, mixab_s, cg_s):
    @pl.when(cond)
    def _finish():
        yc = att_s[...] * cg_s[...]
        mixed = jnp.concatenate([mixab_s[...], yc], axis=-1).astype(BF16)
        y_ref[...] = x_ref[...] + jnp.dot(mixed, wout_ref[...], preferred_element_type=F32)


def _to_channel_major(a):
    depth, bsz, ntok, nh, hd = a.shape
    return jnp.transpose(a, (0, 1, 3, 4, 2)).reshape(depth, bsz, nh * hd, ntok)


def _from_channel_major(a):
    depth, bsz, _, ntok = a.shape
    return jnp.transpose(a.reshape(depth, bsz, N_HEADS, HEAD_DIM, ntok), (0, 1, 4, 2, 3))


def kernel(x_prompt, x_sample, state_conv, state_pool, cache_k_win, cache_v_win, norm_g, w_in, conv_w, conv_b,
           ln_g, ln_b, pool_w, pool_scale, q_norm_g, k_norm_g, w_out):
    depth = w_in.shape[0]
    seq_len = x_prompt.shape[1]
    nsmp = x_sample.shape[0]

    cos_p, sin_p = _rope_tables(np.arange(seq_len))
    cos_s, sin_s = _rope_tables(np.full((1,), PAST_LEN))
    mavg = _head_mean_matrix()
    weights = (_pack_vectors(norm_g, conv_b, ln_g, ln_b, pool_scale, q_norm_g, k_norm_g), w_in.astype(BF16), conv_w,
               _pool_block_diag(pool_w).astype(BF16))
    w_out_b = w_out.astype(BF16)

    kt = _to_channel_major(cache_k_win)
    vt = _to_channel_major(cache_v_win)
    xp = x_prompt
    xs = x_sample.reshape(nsmp, D_MODEL)
    state_conv_pm = jnp.swapaxes(state_conv, 1, 2)
    state_pool_pm = jnp.swapaxes(state_pool, 1, 2)
    kv = None
    conv_p, pool_p, conv_s, pool_s = [], [], [], []
    for layer in range(depth):
        ncs, nps, *decode_new = _decode_prepare(layer, PAST_LEN, xs, state_conv_pm, state_pool_pm, *weights,
                                                cos_s, sin_s, mavg)
        xp, kp, vp, cst, pst, xs, ks, vs = _trunk_layer(
            layer, xp, xs, decode_new, kt, vt, *weights, cos_p, sin_p, mavg, w_out_b, kv_prev=kv)
        kv = (kp, vp, ks, vs)
        conv_p.append(cst)
        pool_p.append(pst)
        conv_s.append(ncs)
        pool_s.append(nps)

    return (xp, xs.reshape(nsmp, 1, D_MODEL), jnp.stack(conv_p), jnp.stack(pool_p),
            _from_channel_major(kv[0]), _from_channel_major(kv[1]),
            jnp.swapaxes(jnp.stack(conv_s), 1, 2), jnp.swapaxes(jnp.stack(pool_s), 1, 2),
            _from_channel_major(kv[2]), _from_channel_major(kv[3]))
```

```python
import functools

import jax
import jax.numpy as jnp
import numpy as np
from jax import lax
from jax.experimental import pallas as pl
from jax.experimental.pallas import tpu as pltpu

F32 = jnp.float32
BF16 = jnp.bfloat16

D_MODEL = 1024
C_CONV = 384
C_POOL = 256
C_ATT = 384
HEAD_DIM = 64
N_HEADS = C_ATT // HEAD_DIM
CONV_W = 31
CONV_HALO = CONV_W - 1
POOL_WINDOWS = (2, 4, 8, 16)
POOL_GC = 64
POOL_BUF = 15
DILATIONS = (1, 4, 16)
WINDOW_KEYS = 128
EPS = 1e-6
ROPE_THETA = 10000.0
D_IN = 3 * C_CONV + 2 * C_POOL + 4 * C_ATT
PAST_LEN = 16384
NEG = -1e30
LOG2_E = 1.4426950408889634

OFF_A_VAL = 0
OFF_A_GLU = OFF_A_VAL + C_CONV
OFF_A_GATE = OFF_A_GLU + C_CONV
OFF_B_VAL = OFF_A_GATE + C_CONV
OFF_B_GATE = OFF_B_VAL + C_POOL
OFF_Q = OFF_B_GATE + C_POOL
OFF_K = OFF_Q + C_ATT
OFF_V = OFF_K + C_ATT
OFF_C_GATE = OFF_V + C_ATT

LANES = 128
N_CONV_SLAB = C_CONV // LANES
N_POOL_SLAB = C_POOL // LANES
N_ATT_SLAB = C_ATT // LANES
VMEM_LIMIT_BYTES = 60 * 1024 * 1024

TQ = 256
RC = 32
CONV_GROUP = 8
U_HALO = 32
B_HALO = 16
QB = WINDOW_KEYS
UNROLL_NEAR = 15
UNROLL_MID = 12
UNROLL_FAR = 16

AB_Q, AB_K, AB_V, AB_O, AB_LSE = range(5)
AB_COUNT = 5


def _sigmoid(x):
    return 0.5 * jnp.tanh(0.5 * x) + 0.5


def _silu(x):
    h = 0.5 * x
    return h * jnp.tanh(h) + h


def _rope_tables(positions):
    half = HEAD_DIM // 2
    inv = np.float32(ROPE_THETA) ** (-np.arange(half, dtype=np.float32) / np.float32(half))
    ang = np.asarray(positions, np.float32)[:, None] * inv[None, :]
    cos = np.cos(ang)
    sin = np.sin(ang)
    cos_h = np.concatenate([cos, cos], axis=-1)
    sin_h = np.concatenate([-sin, sin], axis=-1)
    reps = LANES // HEAD_DIM
    return jnp.asarray(np.tile(cos_h, (1, reps))), jnp.asarray(np.tile(sin_h, (1, reps)))


def _head_mean_matrix():
    idx = np.arange(C_ATT) // HEAD_DIM
    return jnp.asarray((idx[:, None] == idx[None, :]).astype(np.float32) / HEAD_DIM, dtype=BF16)


def _pool_block_diag(pool_w):
    n_grp = len(POOL_WINDOWS)
    same_group = np.eye(n_grp, dtype=np.float32)[None, :, None, :, None]
    return (pool_w[:, :, :, None, :] * same_group).reshape(pool_w.shape[0], C_POOL, C_POOL)


VEC_NORM_G = 0
VEC_CONV_B = VEC_NORM_G + D_MODEL
VEC_LN_G = VEC_CONV_B + C_CONV
VEC_LN_B = VEC_LN_G + C_CONV
VEC_POOL_SCALE = VEC_LN_B + C_CONV
VEC_Q_GAIN = VEC_POOL_SCALE + C_POOL
VEC_K_GAIN = VEC_Q_GAIN + C_ATT
VEC_LEN = VEC_K_GAIN + C_ATT


def _pack_vectors(norm_g, conv_b, ln_g, ln_b, pool_scale, q_norm_g, k_norm_g):
    parts = [norm_g, conv_b, ln_g, ln_b, pool_scale, jnp.tile(q_norm_g, (1, N_HEADS)), jnp.tile(k_norm_g, (1, N_HEADS))]
    row, off = 0.0, 0
    for p in parts:
        row = row + jnp.pad(p, ((0, 0), (off, VEC_LEN - off - p.shape[1])))
        off += p.shape[1]
    return row[:, None]


def _vector_views(vec_ref):
    bounds = (VEC_NORM_G, VEC_CONV_B, VEC_LN_G, VEC_LN_B, VEC_POOL_SCALE, VEC_Q_GAIN, VEC_K_GAIN, VEC_LEN)
    return tuple(vec_ref.at[:, lo:hi] for lo, hi in zip(bounds[:-1], bounds[1:]))


def _swap_halves(x):
    lane = lax.broadcasted_iota(jnp.int32, x.shape, 1)
    first_half = (lane % HEAD_DIM) < (HEAD_DIM // 2)
    return jnp.where(first_half, pltpu.roll(x, LANES - HEAD_DIM // 2, 1), pltpu.roll(x, HEAD_DIM // 2, 1))


def _pool_means(loads, pos):
    lane = lax.broadcasted_iota(jnp.int32, loads(0, 0).shape, 1)
    lo = lane < POOL_GC
    posf = (pos + 1).astype(F32)
    outs = []
    for slab in range(N_POOL_SLAB):
        w_lo, w_hi = POOL_WINDOWS[2 * slab], POOL_WINDOWS[2 * slab + 1]
        cur = loads(0, slab)
        s = cur
        for i in range(1, w_lo):
            s = s + loads(i, slab)
        s_lo = s
        for i in range(w_lo, w_hi):
            s = s + loads(i, slab)
        s_hi = s
        cnt_lo = jnp.minimum(posf, float(w_lo))
        cnt_hi = jnp.minimum(posf, float(w_hi))
        pooled = jnp.where(lo, s_lo / cnt_lo, s_hi / cnt_hi)
        outs.append(pooled - cur)
    return outs


N_LAYER_INPUTS = 18
N_LAYER_OUTPUTS = 8


def _layer_kernel(x_ref, x2_ref, vec_ref, win_ref, cw_ref, pw_ref, cos_ref, sin_ref, mavg_ref, wout_ref,
                  xs_ref, qr_s, kr_s, v_s, mixab_s, cgs_s, kt_ref, vt_ref,
                  y_ref, ko_ref, vo_ref, cst_ref, pst_ref,
                  ys_ref, okt_ref, ovt_ref,
                  h_s, proj, u_buf, b_buf,
                  st_q, st_k, st_v, st_att, ab0, ab1, ab2, ab3, ab4, mix,
                  att_s, qcol_s, sm_s, sl_s, so_s, kcar_s, vcar_s,
                  *, seq_len, win_len):
    ab = (ab0, ab1, ab2, ab3, ab4)
    ng_ref, cb_ref, lng_ref, lnb_ref, psc_ref, qg_ref, kg_ref = _vector_views(vec_ref)
    nt = seq_len // TQ
    b = pl.program_id(0)
    t = pl.program_id(1)
    n_batch = pl.num_programs(0) - 1
    blocks_per_sample = win_len // kt_ref.shape[1]
    row0 = pl.multiple_of(t * TQ, TQ)
    sq_s = h_s.at[:, 0:2 * C_ATT]
    d_s = h_s.at[:, 2 * C_ATT:2 * C_ATT + C_POOL]
    msq_s = proj.at[:, OFF_A_VAL:OFF_A_VAL + 2 * C_ATT]
    yb_s = proj.at[:, OFF_B_VAL:OFF_B_VAL + C_POOL]
    conv_s = st_att

    step = b * nt + t
    part = step % blocks_per_sample
    smp = step // blocks_per_sample

    def sample_start():
        _sample_start(part == 0, smp, qr_s, kr_s, v_s, qcol_s, sm_s, sl_s, so_s, kcar_s, vcar_s)

    def sample_window():
        _sample_window((blocks_per_sample - 1 - part) * kt_ref.shape[1], kt_ref, vt_ref, okt_ref, ovt_ref,
                       qcol_s, sm_s, sl_s, so_s, kcar_s, vcar_s, win_len=win_len)

    def sample_done():
        _sample_done(part == blocks_per_sample - 1, smp, att_s, sl_s, so_s)

    def phase2():
        d_mid = DILATIONS[1]
        per_stream = TQ // d_mid
        for s in range(N_ATT_SLAB):
            for r in range(d_mid):
                src_rows = pl.ds(pl.multiple_of(r * (seq_len // d_mid) + t * per_stream, per_stream), per_stream)
                st_att[s, pl.ds(r, per_stream, stride=d_mid), :] = ab[AB_O][s, src_rows, :]

        def att_chunk(i, c):
            r = pl.multiple_of(i * RC, RC)
            grow = pl.ds(pl.multiple_of(row0 + r, RC), RC)
            for s in range(N_ATT_SLAB):
                mcols = slice(C_CONV + C_POOL + s * LANES, C_CONV + C_POOL + (s + 1) * LANES)
                yc = st_att[s, pl.ds(r, RC), :] * mix[grow, mcols].astype(F32)
                mix[grow, mcols] = yc.astype(BF16)
            return c

        lax.fori_loop(0, TQ // RC, att_chunk, 0, unroll=True)
        y_ref[...] = x2_ref[...] + jnp.dot(mix[pl.ds(row0, TQ), :], wout_ref[...], preferred_element_type=F32)

    def phase1(with_phase2):
        @pl.when(t == 0)
        def _zero_halo():
            u_buf[:, 0:U_HALO, :] = jnp.zeros((N_CONV_SLAB, U_HALO, LANES), F32)
            b_buf[:, 0:B_HALO, :] = jnp.zeros((N_POOL_SLAB, B_HALO, LANES), F32)
            if not with_phase2:
                att_s[...] = jnp.zeros(att_s.shape, F32)

        sample_start()
        if with_phase2:
            phase2()
        sample_window()

        def norm_chunk(i, c):
            r = pl.multiple_of(i * RC, RC)
            x = x_ref[pl.ds(r, RC), :]
            ms = jnp.mean(x * x, axis=-1, keepdims=True)
            h_s[pl.ds(r, RC), :] = (x * lax.rsqrt(ms + EPS) * ng_ref[...]).astype(BF16)
            return c

        lax.fori_loop(0, TQ // RC, norm_chunk, 0, unroll=True)
        proj[...] = jnp.dot(h_s[...], win_ref[...], preferred_element_type=F32)

        def split_chunk(i, c):
            r = pl.multiple_of(i * RC, RC)
            rows = pl.ds(r, RC)
            grow = pl.ds(pl.multiple_of(row0 + r, RC), RC)
            for s in range(N_CONV_SLAB):
                cols = slice(s * LANES, (s + 1) * LANES)
                a_val = proj[rows, OFF_A_VAL + s * LANES:OFF_A_VAL + (s + 1) * LANES]
                a_glu = proj[rows, OFF_A_GLU + s * LANES:OFF_A_GLU + (s + 1) * LANES]
                u_buf[s, pl.ds(U_HALO + r, RC), :] = a_val * _sigmoid(a_glu)
                q = proj[rows, OFF_Q + s * LANES:OFF_Q + (s + 1) * LANES]
                k = proj[rows, OFF_K + s * LANES:OFF_K + (s + 1) * LANES]
                sq_s[rows, cols] = (q * q).astype(BF16)
                sq_s[rows, C_ATT + s * LANES:C_ATT + (s + 1) * LANES] = (k * k).astype(BF16)
                st_v[s, rows, :] = proj[rows, OFF_V + s * LANES:OFF_V + (s + 1) * LANES]
                mix[grow, C_CONV + C_POOL + s * LANES:C_CONV + C_POOL + (s + 1) * LANES] = _silu(
                    proj[rows, OFF_C_GATE + s * LANES:OFF_C_GATE + (s + 1) * LANES]).astype(BF16)
            for s in range(N_POOL_SLAB):
                b_buf[s, pl.ds(B_HALO + r, RC), :] = proj[rows, OFF_B_VAL + s * LANES:OFF_B_VAL + (s + 1) * LANES]
            pos = row0 + r + lax.broadcasted_iota(jnp.int32, (RC, 1), 0)
            dl = _pool_means(lambda sh, s: b_buf[s, pl.ds(r + B_HALO - sh, RC), :], pos)
            for s in range(N_POOL_SLAB):
                d_s[rows, s * LANES:(s + 1) * LANES] = dl[s].astype(BF16)
            return c

        lax.fori_loop(0, TQ // RC, split_chunk, 0, unroll=True)
        msq_s[:, 0:C_ATT] = jnp.dot(sq_s[:, 0:C_ATT], mavg_ref[...], preferred_element_type=F32)
        msq_s[:, C_ATT:2 * C_ATT] = jnp.dot(sq_s[:, C_ATT:2 * C_ATT], mavg_ref[...], preferred_element_type=F32)
        yb_s[...] = jnp.dot(d_s[...], pw_ref[...], preferred_element_type=F32)

        row_groups = TQ // CONV_GROUP
        for s in range(N_CONV_SLAB):
            cols = slice(s * LANES, (s + 1) * LANES)
            acc = [jnp.broadcast_to(cb_ref[:, cols], (CONV_GROUP, LANES))] * row_groups
            for a in range(CONV_GROUP):
                taps = list(range(a, CONV_W, CONV_GROUP))
                cws = [jnp.broadcast_to(cw_ref[w:w + 1, cols], (CONV_GROUP, LANES)) for w in taps]
                for j in range(row_groups + len(taps) - 1):
                    start = (U_HALO - CONV_HALO) + a + CONV_GROUP * j
                    window = u_buf[s, start:start + CONV_GROUP, :]
                    for m in range(len(taps)):
                        if 0 <= j - m < row_groups:
                            acc[j - m] = acc[j - m] + window * cws[m]
            for g in range(row_groups):
                conv_s[s, g * CONV_GROUP:(g + 1) * CONV_GROUP, :] = acc[g]

        def mixer_chunk(i, c):
            r = pl.multiple_of(i * RC, RC)
            rows = pl.ds(r, RC)
            grow = pl.ds(pl.multiple_of(row0 + r, RC), RC)
            conv = [conv_s[s, rows, :] for s in range(N_CONV_SLAB)]
            mu = jnp.sum(conv[0] + conv[1] + conv[2], axis=-1, keepdims=True) * (1.0 / C_CONV)
            cen = [cv - mu for cv in conv]
            var = jnp.sum(cen[0] * cen[0] + cen[1] * cen[1] + cen[2] * cen[2], axis=-1, keepdims=True) * (1.0 / C_CONV)
            rstd = lax.rsqrt(var + EPS)
            for s in range(N_CONV_SLAB):
                cols = slice(s * LANES, (s + 1) * LANES)
                ln = cen[s] * rstd * lng_ref[:, cols] + lnb_ref[:, cols]
                gate = proj[rows, OFF_A_GATE + s * LANES:OFF_A_GATE + (s + 1) * LANES]
                mix[grow, cols] = (_silu(ln) * _silu(gate)).astype(BF16)
            for s in range(N_POOL_SLAB):
                cols = slice(s * LANES, (s + 1) * LANES)
                gate = proj[rows, OFF_B_GATE + s * LANES:OFF_B_GATE + (s + 1) * LANES]
                yb = yb_s[rows, cols] * psc_ref[:, cols] * _silu(gate)
                mix[grow, C_CONV + s * LANES:C_CONV + (s + 1) * LANES] = yb.astype(BF16)
            cos = cos_ref[grow, :]
            sin = sin_ref[grow, :]
            for s in range(N_ATT_SLAB):
                cols = slice(s * LANES, (s + 1) * LANES)
                q = proj[rows, OFF_Q + s * LANES:OFF_Q + (s + 1) * LANES]
                qn = q * lax.rsqrt(msq_s[rows, cols] + EPS) * qg_ref[:, cols]
                qr = qn * cos + _swap_halves(qn) * sin
                st_q[s, rows, :] = qr * (HEAD_DIM ** -0.5 * LOG2_E)
                k = proj[rows, OFF_K + s * LANES:OFF_K + (s + 1) * LANES]
                kn = k * lax.rsqrt(msq_s[rows, C_ATT + s * LANES:C_ATT + (s + 1) * LANES] + EPS) * kg_ref[:, cols]
                st_k[s, rows, :] = kn * cos + _swap_halves(kn) * sin
            return c

        lax.fori_loop(0, TQ // RC, mixer_chunk, 0, unroll=True)

        d_mid = DILATIONS[1]
        per_stream = TQ // d_mid
        for s in range(N_ATT_SLAB):
            ko_ref[s * LANES:(s + 1) * LANES, :] = st_k[s].T
            vo_ref[s * LANES:(s + 1) * LANES, :] = st_v[s].T
            for st, dst in ((st_q, AB_Q), (st_k, AB_K), (st_v, AB_V)):
                for r in range(d_mid):
                    dst_rows = pl.ds(pl.multiple_of(r * (seq_len // d_mid) + t * per_stream, per_stream), per_stream)
                    ab[dst][s, dst_rows, :] = st[s, pl.ds(r, per_stream, stride=d_mid), :]

        @pl.when(t == nt - 1)
        def _write_state():
            for s in range(N_CONV_SLAB):
                cst_ref[:, s * LANES:(s + 1) * LANES] = u_buf[s, TQ + U_HALO - CONV_HALO:TQ + U_HALO, :]
            for s in range(N_POOL_SLAB):
                pst_ref[:, s * LANES:(s + 1) * LANES] = b_buf[s, TQ + B_HALO - POOL_BUF:TQ + B_HALO, :]

        u_buf[:, 0:U_HALO, :] = u_buf[:, TQ:TQ + U_HALO, :]
        b_buf[:, 0:B_HALO, :] = b_buf[:, TQ:TQ + B_HALO, :]
        sample_done()

    def attention():
        d_mid, d_far = DILATIONS[1], DILATIONS[2]
        ratio = d_far // d_mid
        stream_len = seq_len // d_mid
        piece = QB // d_mid
        lane = lax.broadcasted_iota(jnp.int32, (QB, LANES), 1)
        rowi = lax.broadcasted_iota(jnp.int32, (QB, LANES), 0)
        lo = lane < HEAD_DIM

        def masks(key_pos, query_pos):
            cur = key_pos <= query_pos
            prev = key_pos >= query_pos
            cur2 = jnp.concatenate([cur, cur], axis=0)
            return cur2, jnp.concatenate([jnp.concatenate([prev, prev], axis=0), cur2], axis=1)

        cur_ok2, prev_cur_ok2 = masks(lane, rowi)
        ncur_ok2, nprev_cur_ok2 = masks(d_mid * (lane % piece) + lane // piece, d_mid * (rowi % piece) + rowi // piece)

        def attend(q, keys, vals, mask):
            qa = jnp.where(lo, q, 0.0).astype(BF16)
            qb = jnp.where(lo, 0.0, q).astype(BF16)
            q2 = jnp.concatenate([qa, qb], axis=0)
            sc = lax.dot_general(q2, keys.astype(BF16), (((1,), (1,)), ((), ())), preferred_element_type=F32)
            sc = jnp.where(mask, sc, NEG)
            m = jnp.max(sc, axis=-1, keepdims=True)
            p = jnp.exp2(sc - m).astype(BF16)
            v1 = jnp.concatenate([vals.astype(BF16), jnp.ones(vals.shape, BF16)], axis=1)
            ol = jnp.dot(p, v1, preferred_element_type=F32)
            o_u = jnp.where(lo, ol[0:QB, 0:LANES], ol[QB:2 * QB, 0:LANES])
            l_u = jnp.where(lo, ol[0:QB, LANES:2 * LANES], ol[QB:2 * QB, LANES:2 * LANES])
            m_u = jnp.where(lo, m[0:QB], m[QB:2 * QB])
            return o_u, m_u, l_u

        def load(ref, s, pieces):
            tiles = [ref[s, p, :] for p in pieces]
            return tiles[0] if len(tiles) == 1 else jnp.concatenate(tiles, axis=0)

        def save(ref, s, pieces, val):
            n = val.shape[0] // len(pieces)
            for j, p in enumerate(pieces):
                ref[s, p, :] = val[j * n:(j + 1) * n]

        def block(rows, krows, mask, merge, with_lse=True):
            outs = []
            for s in range(N_ATT_SLAB):
                o_u, m_u, l_u = attend(load(ab[AB_Q], s, rows), load(ab[AB_K], s, krows), load(ab[AB_V], s, krows),
                                       mask)
                if merge:
                    lse_old = load(ab[AB_LSE], s, rows)
                    m_new = jnp.maximum(lse_old, m_u)
                    w_old = jnp.exp2(lse_old - m_new)
                    w_u = jnp.exp2(m_u - m_new)
                    den = w_old + l_u * w_u
                    o_n = (load(ab[AB_O], s, rows) * w_old + o_u * w_u) * (1.0 / den)
                else:
                    m_new, den = m_u, l_u
                    o_n = o_u * (1.0 / l_u)
                outs.append((o_n, m_new + jnp.log2(den) if with_lse else None))
            return outs

        def store(rows, outs):
            for s, (o_n, lse_n) in enumerate(outs):
                save(ab[AB_O], s, rows, o_n)
                if lse_n is not None:
                    save(ab[AB_LSE], s, rows, lse_n)

        def token_block(base):
            return [pl.ds(r * stream_len + base, piece) for r in range(d_mid)]

        store(token_block(0), block(token_block(0), token_block(0), ncur_ok2, merge=False))

        def near_unit(i, c):
            base = pl.multiple_of(i * piece, piece)
            rows = token_block(base)
            store(rows, block(rows, token_block(base - piece) + rows, nprev_cur_ok2, merge=False))
            return c

        lax.fori_loop(1, seq_len // QB, near_unit, 0, unroll=UNROLL_NEAR)

        def mid_first(ph, c):
            rows = [pl.ds(pl.multiple_of(ph * stream_len, QB), QB)]
            store(rows, block(rows, rows, cur_ok2, merge=True))
            return c

        lax.fori_loop(0, d_mid, mid_first, 0, unroll=True)
        later_blocks = stream_len // QB - 1

        def mid_unit(i, c):
            start = pl.multiple_of((i // later_blocks) * stream_len + (i % later_blocks + 1) * QB, QB)
            rows = [pl.ds(start, QB)]
            store(rows, block(rows, [pl.ds(start - QB, 2 * QB)], prev_cur_ok2, merge=True))
            return c

        lax.fori_loop(0, d_mid * later_blocks, mid_unit, 0, unroll=UNROLL_MID)

        def far_unit(r, c):
            rows = [pl.ds((r % d_mid) * stream_len + r // d_mid, QB, stride=ratio)]
            store(rows, block(rows, rows, cur_ok2, merge=True, with_lse=False))
            return c

        lax.fori_loop(0, d_far, far_unit, 0, unroll=UNROLL_FAR)

    @pl.when(b == 0)
    def _first_row():
        phase1(with_phase2=False)

    @pl.when(jnp.logical_and(b > 0, b < n_batch))
    def _steady_rows():
        phase1(with_phase2=True)

    @pl.when(b == n_batch)
    def _last_row():
        phase2()
        _sample_finish(t == nt - 1, xs_ref, wout_ref, ys_ref, att_s, mixab_s, cgs_s)

    pl.when(jnp.logical_and(t == nt - 1, b < n_batch))(attention)


def _layer_kernel_with_alias(*refs, n_alias, n_cast=0, **kw):
    n_in = N_LAYER_INPUTS + n_alias + n_cast
    outs = refs[n_in:]
    for src, dst in zip(refs[N_LAYER_INPUTS + n_alias:n_in], outs[N_LAYER_OUTPUTS:N_LAYER_OUTPUTS + n_cast]):
        dst[...] = src[...].astype(BF16)
    _layer_kernel(*refs[:N_LAYER_INPUTS], *outs[:N_LAYER_OUTPUTS], *outs[N_LAYER_OUTPUTS + n_cast:], **kw)


def _trunk_layer(layer, x, xs, decode_new, kt, vt, vecs, w_in, conv_w, pool_wbd, cos_t, sin_t, mavg, w_out,
                 kv_prev=None, next_f32=()):
    bsz, seq_len, _ = x.shape
    nsmp = xs.shape[0]
    depth, _, _, win_len = kt.shape
    assert seq_len % TQ == 0 and seq_len == QB * DILATIONS[-1] and DILATIONS[0] == 1
    assert win_len == WINDOW_KEYS * DILATIONS[-1]
    nt = seq_len // TQ
    blocks_per_sample, rem = divmod(bsz * nt, nsmp)
    assert rem == 0 and blocks_per_sample >= 1 and win_len % (blocks_per_sample * LANES) == 0
    win_blk = win_len // blocks_per_sample

    def p1(b, t):
        return jnp.minimum(b, bsz - 1), jnp.where(b == bsz, nt - 1, t)

    def p2(b, t):
        return jnp.maximum(b - 1, 0), jnp.where(b == 0, 0, t)

    def sample_block(b, t):
        pb, pt = p1(b, t)
        step = pb * nt + pt
        return (layer, step // blocks_per_sample, 0, blocks_per_sample - 1 - step % blocks_per_sample)

    def const(shape):
        nd = len(shape)
        return pl.BlockSpec(shape, lambda b, t: (0,) * nd)

    def resident(shape):
        nd = len(shape)
        return pl.BlockSpec(shape, lambda b, t: (0,) * nd, pipeline_mode=pl.Buffered(1))

    def per_layer(shape, **kw):
        nd = len(shape)
        return pl.BlockSpec((None,) + shape, lambda b, t: (layer,) + (0,) * nd, **kw)

    in_specs = [
        pl.BlockSpec((None, TQ, D_MODEL), lambda b, t: (*p1(b, t), 0)),
        pl.BlockSpec((None, TQ, D_MODEL), lambda b, t: (*p2(b, t), 0)),
        per_layer((1, VEC_LEN)),
        per_layer((D_MODEL, D_IN), pipeline_mode=pl.Buffered(1)),
        per_layer((CONV_W, C_CONV)),
        per_layer((C_POOL, C_POOL)),
        resident((seq_len, LANES)),
        resident((seq_len, LANES)),
        const((C_ATT, C_ATT)),
        per_layer((D_MODEL, D_MODEL), pipeline_mode=pl.Buffered(1)),
        const((nsmp, D_MODEL)),
        const((nsmp, C_ATT)),
        const((nsmp, C_ATT)),
        const((nsmp, C_ATT)),
        const((nsmp, C_CONV + C_POOL)),
        const((nsmp, C_ATT)),
        pl.BlockSpec((None, None, C_ATT, win_blk), sample_block),
        pl.BlockSpec((None, None, C_ATT, win_blk), sample_block),
    ]
    operands = [x, x, vecs, w_in, conv_w, pool_wbd, cos_t, sin_t, mavg, w_out, xs, *decode_new, kt, vt]
    assert len(operands) == N_LAYER_INPUTS
    kv_spec = pl.BlockSpec((None, None, C_ATT, TQ), lambda b, t: (layer, p1(b, t)[0], 0, p1(b, t)[1]))
    out_specs = [
        pl.BlockSpec((None, TQ, D_MODEL), lambda b, t: (*p2(b, t), 0)),
        kv_spec,
        kv_spec,
        pl.BlockSpec((None, CONV_HALO, C_CONV), lambda b, t: (p1(b, t)[0], 0, 0)),
        pl.BlockSpec((None, POOL_BUF, C_POOL), lambda b, t: (p1(b, t)[0], 0, 0)),
        const((nsmp, D_MODEL)),
        pl.BlockSpec((None, None, C_ATT, win_blk), sample_block),
        pl.BlockSpec((None, None, C_ATT, win_blk), sample_block),
    ]
    out_shape = [
        jax.ShapeDtypeStruct((bsz, seq_len, D_MODEL), F32),
        jax.ShapeDtypeStruct((depth, bsz, C_ATT, seq_len), F32),
        jax.ShapeDtypeStruct((depth, bsz, C_ATT, seq_len), F32),
        jax.ShapeDtypeStruct((bsz, CONV_HALO, C_CONV), F32),
        jax.ShapeDtypeStruct((bsz, POOL_BUF, C_POOL), F32),
        jax.ShapeDtypeStruct((nsmp, D_MODEL), F32),
        jax.ShapeDtypeStruct((depth, nsmp, C_ATT, win_len), F32),
        jax.ShapeDtypeStruct((depth, nsmp, C_ATT, win_len), F32),
    ]
    static = dict(seq_len=seq_len, win_len=win_len)
    kv_prev = () if kv_prev is None else tuple(kv_prev)
    kern = functools.partial(_layer_kernel_with_alias, n_alias=len(kv_prev), n_cast=len(next_f32), **static)
    in_specs += [pl.BlockSpec(memory_space=pl.ANY)] * len(kv_prev)
    operands += list(kv_prev)
    aliases = {N_LAYER_INPUTS + i: o for i, o in enumerate((1, 2, 6, 7)[:len(kv_prev)])}
    wsteps = bsz * nt
    for w in next_f32:
        wrows = w.shape[1] // wsteps
        assert w.shape[1] % wsteps == 0 and wrows % 16 == 0
        wspec = pl.BlockSpec((None, wrows, w.shape[2]),
                             lambda b, t: (layer + 1, jnp.minimum(b * nt + t, wsteps - 1), 0))
        in_specs.append(wspec)
        operands.append(w)
        out_specs.append(wspec)
        out_shape.append(jax.ShapeDtypeStruct(w.shape, BF16))
    scratch = [
        pltpu.VMEM((TQ, D_MODEL), BF16),
        pltpu.VMEM((TQ, D_IN), F32),
        pltpu.VMEM((N_CONV_SLAB, TQ + U_HALO, LANES), F32),
        pltpu.VMEM((N_POOL_SLAB, TQ + B_HALO, LANES), F32),
        *[pltpu.VMEM((N_ATT_SLAB, TQ, LANES), F32) for _ in range(4)],
        *[pltpu.VMEM((N_ATT_SLAB, seq_len, LANES), F32) for _ in range(AB_COUNT)],
        pltpu.VMEM((seq_len, D_MODEL), BF16),
        pltpu.VMEM((nsmp, C_ATT), F32),
        pltpu.VMEM((C_ATT, LANES), F32),
        pltpu.VMEM((N_HEADS, LANES), F32),
        pltpu.VMEM((N_HEADS, LANES), F32),
        pltpu.VMEM((C_ATT, LANES), F32),
        pltpu.VMEM((C_ATT, LANES), F32),
        pltpu.VMEM((C_ATT, LANES), F32),
    ]
    return pl.pallas_call(
        kern,
        out_shape=out_shape,
        grid=(bsz + 1, nt),
        in_specs=in_specs,
        out_specs=out_specs,
        scratch_shapes=scratch,
        input_output_aliases=aliases,
        compiler_params=pltpu.CompilerParams(
            dimension_semantics=("arbitrary", "arbitrary"),
            vmem_limit_bytes=VMEM_LIMIT_BYTES,
        ),
        name="trunk_layer",
    )(*operands)


def _row_to_col_tile(row):
    return jnp.broadcast_to(row, (LANES, row.shape[1])).T


def _key_multiplicity(lane0, blk, win_len):
    t = lane0 + lax.broadcasted_iota(jnp.int32, (1, blk), 1)
    delta = win_len - t
    cnt = jnp.zeros((1, blk), F32)
    for d in DILATIONS:
        hit = jnp.logical_and(delta % d == 0, delta <= d * WINDOW_KEYS)
        cnt = cnt + hit.astype(F32)
    return cnt


def _decode_prepare_kernel(x_ref, sc_ref, sp_ref, vec_ref, win_ref, cw_ref, pw_ref, cos_ref, sin_ref, mavg_ref,
                           nc_ref, np_ref, qr_s, kr_s, v_s, mixab_s, cg_s, *, pos):
    nsmp = x_ref.shape[0]
    ng_ref, cb_ref, lng_ref, lnb_ref, psc_ref, qg_ref, kg_ref = _vector_views(vec_ref)

    x = x_ref[...]
    ms = jnp.mean(x * x, axis=-1, keepdims=True)
    h = (x * lax.rsqrt(ms + EPS) * ng_ref[...]).astype(BF16)
    proj = jnp.dot(h, win_ref[...], preferred_element_type=F32)

    u = proj[:, OFF_A_VAL:OFF_A_VAL + C_CONV] * _sigmoid(proj[:, OFF_A_GLU:OFF_A_GLU + C_CONV])
    conv = u * cw_ref[CONV_HALO:CONV_W, :] + cb_ref[...]
    for w in range(CONV_HALO):
        conv = conv + sc_ref[w] * cw_ref[w:w + 1, :]
    nc_ref[0:CONV_HALO - 1] = sc_ref[1:CONV_HALO]
    nc_ref[CONV_HALO - 1] = u
    mu = jnp.mean(conv, axis=-1, keepdims=True)
    cen = conv - mu
    var = jnp.mean(cen * cen, axis=-1, keepdims=True)
    ln = cen * lax.rsqrt(var + EPS) * lng_ref[...] + lnb_ref[...]
    ya = _silu(ln) * _silu(proj[:, OFF_A_GATE:OFF_A_GATE + C_CONV])

    bval = proj[:, OFF_B_VAL:OFF_B_VAL + C_POOL]
    lane_p = lax.broadcasted_iota(jnp.int32, (nsmp, C_POOL), 1)
    pooled = jnp.zeros((nsmp, C_POOL), F32)
    acc = bval
    done = 1
    for wi, w in enumerate(POOL_WINDOWS):
        for i in range(done, w):
            acc = acc + sp_ref[POOL_BUF - i]
        done = w
        pooled = jnp.where(lane_p // POOL_GC == wi, acc / float(min(pos + 1, w)), pooled)
    np_ref[0:POOL_BUF - 1] = sp_ref[1:POOL_BUF]
    np_ref[POOL_BUF - 1] = bval
    dpool = (pooled - bval).astype(BF16)
    yb = (jnp.dot(dpool, pw_ref[...], preferred_element_type=F32) * psc_ref[...]
          * _silu(proj[:, OFF_B_GATE:OFF_B_GATE + C_POOL]))
    mixab_s[:, 0:C_CONV] = ya
    mixab_s[:, C_CONV:C_CONV + C_POOL] = yb
    cg_s[...] = _silu(proj[:, OFF_C_GATE:OFF_C_GATE + C_ATT])

    q = proj[:, OFF_Q:OFF_Q + C_ATT]
    k = proj[:, OFF_K:OFF_K + C_ATT]
    qn = q * lax.rsqrt(jnp.dot((q * q).astype(BF16), mavg_ref[...], preferred_element_type=F32) + EPS) * qg_ref[...]
    kn = k * lax.rsqrt(jnp.dot((k * k).astype(BF16), mavg_ref[...], preferred_element_type=F32) + EPS) * kg_ref[...]
    for s in range(N_ATT_SLAB):
        cols = slice(s * LANES, (s + 1) * LANES)
        qs, ks = qn[:, cols], kn[:, cols]
        qr_s[:, cols] = (qs * cos_ref[...] + _swap_halves(qs) * sin_ref[...]) * (HEAD_DIM ** -0.5)
        kr_s[:, cols] = ks * cos_ref[...] + _swap_halves(ks) * sin_ref[...]
    v_s[...] = proj[:, OFF_V:OFF_V + C_ATT]


def _decode_prepare(layer, pos, xs, state_conv, state_pool, vecs, w_in, conv_w, pool_wbd, cos_s, sin_s, mavg):
    nsmp = xs.shape[0]

    def const(shape):
        nd = len(shape)
        return pl.BlockSpec(shape, lambda i: (0,) * nd)

    def per_layer(shape):
        nd = len(shape)
        return pl.BlockSpec((None,) + shape, lambda i: (layer,) + (0,) * nd)

    in_specs = [
        const((nsmp, D_MODEL)), per_layer((CONV_HALO, nsmp, C_CONV)), per_layer((POOL_BUF, nsmp, C_POOL)),
        per_layer((1, VEC_LEN)), per_layer((D_MODEL, D_IN)), per_layer((CONV_W, C_CONV)), per_layer((C_POOL, C_POOL)),
        const((1, LANES)), const((1, LANES)), const((C_ATT, C_ATT)),
    ]
    out_dims = [(CONV_HALO, nsmp, C_CONV), (POOL_BUF, nsmp, C_POOL), (nsmp, C_ATT), (nsmp, C_ATT), (nsmp, C_ATT),
                (nsmp, C_CONV + C_POOL), (nsmp, C_ATT)]
    return pl.pallas_call(
        functools.partial(_decode_prepare_kernel, pos=pos),
        out_shape=[jax.ShapeDtypeStruct(d, F32) for d in out_dims],
        grid=(1,),
        in_specs=in_specs,
        out_specs=[const(d) for d in out_dims],
        compiler_params=pltpu.CompilerParams(dimension_semantics=("arbitrary",)),
        name="decode_prepare",
    )(xs, state_conv, state_pool, vecs, w_in, conv_w, pool_wbd, cos_s, sin_s, mavg)


def _sample_start(cond, smp, qr_s, kr_s, v_s, qcol_s, sm_s, sl_s, so_s, kcar_s, vcar_s):
    nsmp = qr_s.shape[0]
    n_pat = float(len(DILATIONS))

    @pl.when(cond)
    def _start():
        mine = lax.broadcasted_iota(jnp.int32, (nsmp, C_ATT), 0) == smp

        def col_tile(ref):
            return _row_to_col_tile(jnp.sum(jnp.where(mine, ref[...], 0.0), axis=0, keepdims=True))

        q_col = col_tile(qr_s)
        k_col = col_tile(kr_s)
        v_col = col_tile(v_s)
        qcol_s[...] = q_col
        kcar_s[...] = k_col
        vcar_s[...] = v_col
        so_s[...] = v_col * n_pat
        for hd in range(N_HEADS):
            hr = slice(hd * HEAD_DIM, (hd + 1) * HEAD_DIM)
            sm_s[hd:hd + 1, :] = jnp.sum(k_col[hr, :] * q_col[hr, :], axis=0, keepdims=True)
        sl_s[...] = jnp.full(sl_s.shape, n_pat, F32)


def _sample_window(lane0, kt_ref, vt_ref, okt_ref, ovt_ref, qcol_s, sm_s, sl_s, so_s, kcar_s, vcar_s, *, win_len):
    blk = kt_ref.shape[1]
    n_tiles = blk // LANES
    cnt = _key_multiplicity(lane0, blk, win_len)
    reach = cnt > 0.0
    last_lane = lax.broadcasted_iota(jnp.int32, (HEAD_DIM, blk), 1) == blk - 1

    for hd in range(N_HEADS):
        hr = slice(hd * HEAD_DIM, (hd + 1) * HEAD_DIM)
        kt = kt_ref[hr, :]
        vt = vt_ref[hr, :]
        s_win = jnp.sum(kt * jnp.concatenate([qcol_s[hr, :]] * n_tiles, axis=1), axis=0, keepdims=True)
        s_win = jnp.where(reach, s_win, NEG)
        m_old = sm_s[hd:hd + 1, :]
        m_new = jnp.maximum(m_old, jnp.max(s_win, axis=-1, keepdims=True))
        w_old = jnp.exp(m_old - m_new)
        p_win = cnt * jnp.exp(s_win - m_new[:, 0:1])
        sm_s[hd:hd + 1, :] = m_new
        sl_s[hd:hd + 1, :] = sl_s[hd:hd + 1, :] * w_old + jnp.sum(p_win, axis=-1, keepdims=True)
        so_s[hr, :] = so_s[hr, :] * w_old + jnp.sum(vt * p_win, axis=-1, keepdims=True)
        k_next = jnp.concatenate([kcar_s[hr, :]] * n_tiles, axis=1)
        v_next = jnp.concatenate([vcar_s[hr, :]] * n_tiles, axis=1)
        okt_ref[hr, :] = jnp.where(last_lane, k_next, pltpu.roll(kt, blk - 1, 1))
        ovt_ref[hr, :] = jnp.where(last_lane, v_next, pltpu.roll(vt, blk - 1, 1))
        kcar_s[hr, :] = jnp.broadcast_to(kt[:, 0:1], (HEAD_DIM, LANES))
        vcar_s[hr, :] = jnp.broadcast_to(vt[:, 0:1], (HEAD_DIM, LANES))


def _sample_done(cond, smp, att_s, sl_s, so_s):
    nsmp = att_s.shape[0]

    @pl.when(cond)
    def _done():
        cols = [so_s[hd * HEAD_DIM:(hd + 1) * HEAD_DIM, :] / sl_s[hd:hd + 1, :] for hd in range(N_HEADS)]
        att_row = jnp.concatenate(cols, axis=0).T[0:1, :]
        mine = lax.broadcasted_iota(jnp.int32, (nsmp, C_ATT), 0) == smp
        att_s[...] = jnp.where(mine, att_row, att_s[...])


def _sample_finish(cond, x_ref, wout_ref, y_ref, att_s, mixab_s, cg_s):
    @pl.when(cond)
    def _finish():
        yc = att_s[...] * cg_s[...]
        mixed = jnp.concatenate([mixab_s[...], yc], axis=-1).astype(BF16)
        y_ref[...] = x_ref[...] + jnp.dot(mixed, wout_ref[...], preferred_element_type=F32)


def _to_channel_major(a):
    depth, bsz, ntok, nh, hd = a.shape
    return jnp.transpose(a, (0, 1, 3, 4, 2)).reshape(depth, bsz, nh * hd, ntok)


def _from_channel_major(a):
    depth, bsz, _, ntok = a.shape
    return jnp.transpose(a.reshape(depth, bsz, N_HEADS, HEAD_DIM, ntok), (0, 1, 4, 2, 3))


def kernel(x_prompt, x_sample, state_conv, state_pool, cache_k_win, cache_v_win, norm_g, w_in, conv_w, conv_b,
           ln_g, ln_b, pool_w, pool_scale, q_norm_g, k_norm_g, w_out):
    depth = w_in.shape[0]
    seq_len = x_prompt.shape[1]
    nsmp = x_sample.shape[0]

    cos_p, sin_p = _rope_tables(np.arange(seq_len))
    cos_s, sin_s = _rope_tables(np.full((1,), PAST_LEN))
    mavg = _head_mean_matrix()
    vecs = _pack_vectors(norm_g, conv_b, ln_g, ln_b, pool_scale, q_norm_g, k_norm_g)
    pool_wbd = _pool_block_diag(pool_w).astype(BF16)
    w_in_b = w_in[:1].astype(BF16)
    w_out_b = w_out[:1].astype(BF16)

    kt = _to_channel_major(cache_k_win)
    vt = _to_channel_major(cache_v_win)
    xp = x_prompt
    xs = x_sample.reshape(nsmp, D_MODEL)
    state_conv_pm = jnp.swapaxes(state_conv, 1, 2)
    state_pool_pm = jnp.swapaxes(state_pool, 1, 2)
    kv = None
    conv_p, pool_p, conv_s, pool_s = [], [], [], []
    for layer in range(depth):
        weights = (vecs, w_in_b, conv_w, pool_wbd)
        ncs, nps, *decode_new = _decode_prepare(layer, PAST_LEN, xs, state_conv_pm, state_pool_pm, *weights,
                                                cos_s, sin_s, mavg)
        next_f32 = (w_in, w_out) if layer + 1 < depth else ()
        xp, kp, vp, cst, pst, xs, ks, vs, *next_b = _trunk_layer(
            layer, xp, xs, decode_new, kt, vt, *weights, cos_p, sin_p, mavg, w_out_b, kv_prev=kv,
            next_f32=next_f32)
        kv = (kp, vp, ks, vs)
        if next_b:
            w_in_b, w_out_b = next_b
        conv_p.append(cst)
        pool_p.append(pst)
        conv_s.append(ncs)
        pool_s.append(nps)

    return (xp, xs.reshape(nsmp, 1, D_MODEL), jnp.stack(conv_p), jnp.stack(pool_p),
            _from_channel_major(kv[0]), _from_channel_major(kv[1]),
            jnp.swapaxes(jnp.stack(conv_s), 1, 2), jnp.swapaxes(jnp.stack(pool_s), 1, 2),
            _from_channel_major(kv[2]), _from_channel_major(kv[3]))
```
